```python
import math
import jax, jax.numpy as jnp
from jax import lax
import numpy as np

D_MODEL = 1024
BATCH = 8
SEQ = 2048
DEPTH = 2
DEC_BATCH = 128
DEC_SEQ = 8
PAST_LEN = 16384
PAGE_SIZE = 128

N_MIXERS = 2
N_RWKV = (DEPTH + 1) // 2
N_SSD = DEPTH // 2
RWKV_HEAD = 64
RWKV_HEADS = D_MODEL // RWKV_HEAD
D_DECAY_LORA = 64
D_AAA_LORA = 64
D_GATE_LORA = 128
N_LERP = 6
GN_EPS = RWKV_HEAD * 1e-5
SSD_EXPAND = 2
D_INNER = SSD_EXPAND * D_MODEL
SSD_HEADDIM = 64
SSD_HEADS = D_INNER // SSD_HEADDIM
SSD_GROUPS = 4
HEADS_PER_GROUP = SSD_HEADS // SSD_GROUPS
D_STATE = 128
CONV_WIDTH = 4
CONV_DIM = D_INNER + 2 * SSD_GROUPS * D_STATE
D_IN_PROJ = D_INNER + CONV_DIM + SSD_HEADS
CHUNK = 128
RMS_EPS = 1e-5
N_EXPERT_GROUPS = 4
EXPERTS_PER_GROUP = 8
N_EXPERTS = N_EXPERT_GROUPS * EXPERTS_PER_GROUP
TOP_K_INNER = 2
D_EXPERT = 512
DEEPNORM_ALPHA = (2 * DEPTH) ** 0.25
DEEPNORM_BETA = (8 * DEPTH) ** -0.25
LN_EPS = 1e-5

kernel_name = 'rwkv7_mamba2_hmoe_deepnorm_step'


def layer_norm(x, g, b):
    xf = x.astype(jnp.float32)
    mu = jnp.mean(xf, -1, keepdims=True)
    var = jnp.mean(jnp.square(xf - mu), -1, keepdims=True)
    return ((xf - mu) * lax.rsqrt(var + LN_EPS) * g + b).astype(x.dtype)


def rwkv7_time_mix(x, shift_prev, wkv0, mu, w_rkv, w0, w1, w2, a0, a1, a2, g1, g2,
                   k_k, k_a, r_k, gn_w, gn_b, w_o):
    bsz, l, _ = x.shape
    f32 = jnp.float32
    x_prev = jnp.concatenate([shift_prev[:, None, :].astype(x.dtype), x[:, :-1]], axis=1)
    xs = x[None] + (x_prev - x)[None] * mu[:, None, None, :]
    r, k, v = jnp.einsum('sbld,sde->sble', xs[:3], w_rkv)
    xw, xa, xg = xs[3], xs[4], xs[5]
    w_raw = -jax.nn.softplus(-(w0 + jnp.tanh(xw @ w1) @ w2).astype(f32)) - 0.5
    decay = jnp.exp(-jnp.exp(w_raw))
    a = jax.nn.sigmoid((a0 + (xa @ a1) @ a2).astype(f32))
    g = jax.nn.sigmoid(xg @ g1) @ g2
    heads = lambda t: t.astype(f32).reshape(bsz, l, RWKV_HEADS, RWKV_HEAD)
    r, k, v, decay, a = heads(r), heads(k), heads(v), heads(decay), heads(a)
    kk = k * k_k.astype(f32).reshape(RWKV_HEADS, RWKV_HEAD)
    kk = kk / jnp.maximum(jnp.sqrt(jnp.sum(jnp.square(kk), -1, keepdims=True)), 1e-12)
    k = k * (1.0 + (a - 1.0) * k_a.astype(f32).reshape(RWKV_HEADS, RWKV_HEAD))

    def step(S, inp):
        r_t, w_t, k_t, v_t, kk_t, b_t = inp
        sa = jnp.einsum('bhvk,bhk->bhv', S, -kk_t)
        S = (S * w_t[:, :, None, :] + sa[..., None] * b_t[:, :, None, :]
             + v_t[..., None] * k_t[:, :, None, :])
        return S, jnp.einsum('bhvk,bhk->bhv', S, r_t)

    seq_first = lambda t: jnp.moveaxis(t, 1, 0)
    wkv_final, y = lax.scan(step, wkv0.astype(f32),
                            (seq_first(r), seq_first(decay), seq_first(k), seq_first(v),
                             seq_first(kk), seq_first(kk * a)))
    y = jnp.moveaxis(y, 0, 1)
    ym = jnp.mean(y, -1, keepdims=True)
    yv = jnp.mean(jnp.square(y - ym), -1, keepdims=True)
    y = ((y - ym) * lax.rsqrt(yv + GN_EPS)).reshape(bsz, l, D_MODEL) * gn_w + gn_b
    bonus = (jnp.sum(r * k * r_k.astype(f32), -1, keepdims=True) * v).reshape(bsz, l, D_MODEL)
    out = ((y + bonus).astype(x.dtype) * g) @ w_o
    return out, x[:, -1], wkv_final.astype(wkv0.dtype)


def ssd_scan(x, dt, A, B, C, h0):
    bsz, l = x.shape[:2]
    c = CHUNK if l % CHUNK == 0 else l
    nc = l // c
    x = x.reshape(bsz, nc, c, SSD_GROUPS, HEADS_PER_GROUP, SSD_HEADDIM)
    dt = dt.reshape(bsz, nc, c, SSD_GROUPS, HEADS_PER_GROUP)
    B = B.reshape(bsz, nc, c, SSD_GROUPS, D_STATE)
    C = C.reshape(bsz, nc, c, SSD_GROUPS, D_STATE)
    a_cum = jnp.cumsum(dt * A, axis=2)
    causal = jnp.tril(jnp.ones((c, c), bool))[None, None, :, :, None, None]
    seg = a_cum[:, :, :, None] - a_cum[:, :, None, :]
    decay_ts = jnp.where(causal, jnp.exp(jnp.where(causal, seg, 0.0)), 0.0)
    cb = jnp.einsum('bctgn,bcsgn->bctsg', C, B)
    y_diag = jnp.einsum('bctsg,bctsgj,bcsgj,bcsgjp->bctgjp', cb, decay_ts, dt, x)
    decay_end = jnp.exp(a_cum[:, :, -1:] - a_cum)
    chunk_states = jnp.einsum('bcsgn,bcsgj,bcsgjp->bcgjpn', B, decay_end * dt, x)
    chunk_decay = jnp.exp(a_cum[:, :, -1])

    def step(h, inp):
        st, dec = inp
        return h * dec[..., None, None] + st, h

    h_final, h_prev = lax.scan(step, h0, (jnp.moveaxis(chunk_states, 1, 0),
                                          jnp.moveaxis(chunk_decay, 1, 0)))
    h_prev = jnp.moveaxis(h_prev, 0, 1)
    y_off = jnp.einsum('bctgn,bcgjpn,bctgj->bctgjp', C, h_prev, jnp.exp(a_cum))
    y = (y_diag + y_off).reshape(bsz, l, SSD_GROUPS, HEADS_PER_GROUP, SSD_HEADDIM)
    return y, h_final


def ssd_mixer(x, conv_prev, ssm0, w_in, conv_w, conv_b, dt_bias, a_log, d_skip, norm_w, w_out):
    bsz, l, _ = x.shape
    f32 = jnp.float32
    zxbcdt = x @ w_in
    z = zxbcdt[..., :D_INNER]
    xbc = zxbcdt[..., D_INNER:D_INNER + CONV_DIM]
    dt_raw = zxbcdt[..., D_INNER + CONV_DIM:]
    xbc_pad = jnp.concatenate([conv_prev.astype(x.dtype), xbc], axis=1)
    conv = conv_b + sum(xbc_pad[:, i:i + l] * conv_w[i] for i in range(CONV_WIDTH))
    xbc = jax.nn.silu(conv).astype(f32)
    gn = SSD_GROUPS * D_STATE
    xs = xbc[..., :D_INNER].reshape(bsz, l, SSD_GROUPS, HEADS_PER_GROUP, SSD_HEADDIM)
    Bm = xbc[..., D_INNER:D_INNER + gn].reshape(bsz, l, SSD_GROUPS, D_STATE)
    Cm = xbc[..., D_INNER + gn:].reshape(bsz, l, SSD_GROUPS, D_STATE)
    dt = jax.nn.softplus(dt_raw.astype(f32) + dt_bias.astype(f32)).reshape(bsz, l, SSD_GROUPS, HEADS_PER_GROUP)
    A = -jnp.exp(a_log.astype(f32)).reshape(SSD_GROUPS, HEADS_PER_GROUP)
    h0 = ssm0.astype(f32).reshape(bsz, SSD_GROUPS, HEADS_PER_GROUP, SSD_HEADDIM, D_STATE)
    y, h_final = ssd_scan(xs, dt, A, Bm, Cm, h0)
    y = y + xs * d_skip.astype(f32).reshape(SSD_GROUPS, HEADS_PER_GROUP, 1)
    y = y.reshape(bsz, l, D_INNER) * jax.nn.silu(z.astype(f32))
    yg = y.reshape(bsz, l, SSD_GROUPS, D_INNER // SSD_GROUPS)
    yg = yg * lax.rsqrt(jnp.mean(jnp.square(yg), -1, keepdims=True) + RMS_EPS)
    y = (yg.reshape(bsz, l, D_INNER) * norm_w).astype(x.dtype)
    new_conv = xbc_pad[:, -(CONV_WIDTH - 1):].astype(conv_prev.dtype)
    new_ssm = h_final.reshape(bsz, SSD_HEADS, SSD_HEADDIM, D_STATE).astype(ssm0.dtype)
    return y @ w_out, new_conv, new_ssm


def hier_moe(x, w_rg, b_rg, w_re, b_re, w_gate, w_up, w_down):
    shp = x.shape
    t = x.reshape(-1, D_MODEL)
    f32 = jnp.float32
    pg = jax.nn.softmax((t @ w_rg).astype(f32) + b_rg.astype(f32), axis=-1)
    g_sel = jnp.argmax(pg, axis=-1)
    p_group = jnp.take_along_axis(pg, g_sel[:, None], axis=-1)[:, 0]
    el = ((t @ w_re).astype(f32) + b_re.astype(f32)).reshape(-1, N_EXPERT_GROUPS, EXPERTS_PER_GROUP)
    el_sel = jnp.take_along_axis(el, g_sel[:, None, None], axis=1)[:, 0]
    pe = jax.nn.softmax(el_sel, axis=-1)
    top_p, top_i = lax.top_k(pe, TOP_K_INNER)
    top_p = top_p / jnp.sum(top_p, -1, keepdims=True)
    w_inner = jnp.sum(jax.nn.one_hot(top_i, EXPERTS_PER_GROUP, dtype=f32) * top_p[..., None], axis=1)
    combine = (jax.nn.one_hot(g_sel, N_EXPERT_GROUPS, dtype=f32)[:, :, None]
               * w_inner[:, None, :] * p_group[:, None, None]).astype(x.dtype)
    y = jnp.zeros_like(t)
    for gi in range(N_EXPERT_GROUPS):
        h = jax.nn.silu(jnp.einsum('td,edf->tef', t, w_gate[gi])) * jnp.einsum('td,edf->tef', t, w_up[gi])
        y = y + jnp.einsum('tef,efd,te->td', h, w_down[gi], combine[:, gi])
    return y.reshape(shp)


def setup_inputs(seed: int = 0) -> dict:
    key = jax.random.key(seed)
    ks = iter(jax.random.split(key, 48))
    nrm = lambda shape, s: jax.random.normal(next(ks), shape, jnp.float32) * s
    uni = lambda shape, lo, hi: jax.random.uniform(next(ks), shape, jnp.float32, lo, hi)
    beta = DEEPNORM_BETA
    w_rkv = nrm((N_RWKV, 3, D_MODEL, D_MODEL), D_MODEL ** -0.5)
    w_rkv = w_rkv * jnp.array([1.0, 1.0, beta], jnp.float32)[None, :, None, None]
    dt0 = jnp.exp(uni((N_SSD, SSD_HEADS), math.log(1e-3), math.log(1e-1)))
    inp = {
        'x_prompt': nrm((BATCH, SEQ, D_MODEL), 1.0),
        'x_sample': nrm((DEC_BATCH, DEC_SEQ, D_MODEL), 1.0),
        'state_rwkv_wkv': nrm((N_RWKV, DEC_BATCH, RWKV_HEADS, RWKV_HEAD, RWKV_HEAD), 0.1),
        'state_rwkv_shift': nrm((N_RWKV, DEC_BATCH, D_MODEL), 1.0),
        'state_ssd_ssm': nrm((N_SSD, DEC_BATCH, SSD_HEADS, SSD_HEADDIM, D_STATE), 0.1),
        'state_ssd_conv': nrm((N_SSD, DEC_BATCH, CONV_WIDTH - 1, CONV_DIM), 1.0),
        'rwkv_mu': uni((N_RWKV, N_LERP, D_MODEL), 0.0, 1.0),
        'rwkv_w_rkv': w_rkv,
        'rwkv_w0': uni((N_RWKV, D_MODEL), -6.0, 1.0),
        'rwkv_w1': nrm((N_RWKV, D_MODEL, D_DECAY_LORA), D_MODEL ** -0.5),
        'rwkv_w2': nrm((N_RWKV, D_DECAY_LORA, D_MODEL), 0.1 * D_DECAY_LORA ** -0.5),
        'rwkv_a0': nrm((N_RWKV, D_MODEL), 0.1),
        'rwkv_a1': nrm((N_RWKV, D_MODEL, D_AAA_LORA), D_MODEL ** -0.5),
        'rwkv_a2': nrm((N_RWKV, D_AAA_LORA, D_MODEL), 0.1 * D_AAA_LORA ** -0.5),
        'rwkv_g1': nrm((N_RWKV, D_MODEL, D_GATE_LORA), D_MODEL ** -0.5),
        'rwkv_g2': nrm((N_RWKV, D_GATE_LORA, D_MODEL), D_GATE_LORA ** -0.5),
        'rwkv_k_k': 0.85 + nrm((N_RWKV, D_MODEL), 0.02),
        'rwkv_k_a': 1.0 + nrm((N_RWKV, D_MODEL), 0.02),
        'rwkv_r_k': nrm((N_RWKV, RWKV_HEADS, RWKV_HEAD), 0.1),
        'rwkv_gn_w': 1.0 + nrm((N_RWKV, D_MODEL), 0.02),
        'rwkv_gn_b': nrm((N_RWKV, D_MODEL), 0.02),
        'rwkv_w_o': nrm((N_RWKV, D_MODEL, D_MODEL), beta * D_MODEL ** -0.5),
        'ssd_w_in': nrm((N_SSD, D_MODEL, D_IN_PROJ), D_MODEL ** -0.5),
        'ssd_conv_w': nrm((N_SSD, CONV_WIDTH, CONV_DIM), CONV_WIDTH ** -0.5),
        'ssd_conv_b': nrm((N_SSD, CONV_DIM), 0.02),
        'ssd_dt_bias': dt0 + jnp.log(-jnp.expm1(-dt0)),
        'ssd_a_log': jnp.log(uni((N_SSD, SSD_HEADS), 1.0, 16.0)),
        'ssd_d': 1.0 + nrm((N_SSD, SSD_HEADS), 0.02),
        'ssd_norm_w': 1.0 + nrm((N_SSD, D_INNER), 0.02),
        'ssd_w_out': nrm((N_SSD, D_INNER, D_MODEL), beta * D_INNER ** -0.5),
        'ln_gain': 1.0 + nrm((DEPTH, 2, D_MODEL), 0.02),
        'ln_bias': nrm((DEPTH, 2, D_MODEL), 0.02),
        'moe_w_rg': nrm((DEPTH, D_MODEL, N_EXPERT_GROUPS), D_MODEL ** -0.5),
        'moe_b_rg': nrm((DEPTH, N_EXPERT_GROUPS), 0.01),
        'moe_w_re': nrm((DEPTH, D_MODEL, N_EXPERTS), D_MODEL ** -0.5),
        'moe_b_re': nrm((DEPTH, N_EXPERTS), 0.01),
        'moe_w_gate': nrm((DEPTH, N_EXPERT_GROUPS, EXPERTS_PER_GROUP, D_MODEL, D_EXPERT), D_MODEL ** -0.5),
        'moe_w_up': nrm((DEPTH, N_EXPERT_GROUPS, EXPERTS_PER_GROUP, D_MODEL, D_EXPERT), D_MODEL ** -0.5),
        'moe_w_down': nrm((DEPTH, N_EXPERT_GROUPS, EXPERTS_PER_GROUP, D_EXPERT, D_MODEL), beta * D_EXPERT ** -0.5),
    }
    return inp


def reference(x_prompt, x_sample, state_rwkv_wkv, state_rwkv_shift, state_ssd_ssm, state_ssd_conv,
              rwkv_mu, rwkv_w_rkv, rwkv_w0, rwkv_w1, rwkv_w2, rwkv_a0, rwkv_a1, rwkv_a2,
              rwkv_g1, rwkv_g2, rwkv_k_k, rwkv_k_a, rwkv_r_k, rwkv_gn_w, rwkv_gn_b, rwkv_w_o,
              ssd_w_in, ssd_conv_w, ssd_conv_b, ssd_dt_bias, ssd_a_log, ssd_d, ssd_norm_w, ssd_w_out,
              ln_gain, ln_bias, moe_w_rg, moe_b_rg, moe_w_re, moe_b_re, moe_w_gate, moe_w_up, moe_w_down):
    yp, ys = x_prompt, x_sample
    bp = x_prompt.shape[0]
    wkv_p, wkv_s, sh_p, sh_s, ssm_p, ssm_s, cv_p, cv_s = [], [], [], [], [], [], [], []
    for i in range(DEPTH):
        j = i // N_MIXERS
        if i % N_MIXERS == 0:
            prm = (rwkv_mu[j], rwkv_w_rkv[j], rwkv_w0[j], rwkv_w1[j], rwkv_w2[j], rwkv_a0[j],
                   rwkv_a1[j], rwkv_a2[j], rwkv_g1[j], rwkv_g2[j], rwkv_k_k[j], rwkv_k_a[j],
                   rwkv_r_k[j], rwkv_gn_w[j], rwkv_gn_b[j], rwkv_w_o[j])
            zero_shift = jnp.zeros((bp, D_MODEL), yp.dtype)
            zero_wkv = jnp.zeros((bp, RWKV_HEADS, RWKV_HEAD, RWKV_HEAD), state_rwkv_wkv.dtype)
            mix_p, s_p, w_p = rwkv7_time_mix(yp, zero_shift, zero_wkv, *prm)
            mix_s, s_s, w_s = rwkv7_time_mix(ys, state_rwkv_shift[j], state_rwkv_wkv[j], *prm)
            wkv_p.append(w_p); wkv_s.append(w_s); sh_p.append(s_p); sh_s.append(s_s)
        else:
            prm = (ssd_w_in[j], ssd_conv_w[j], ssd_conv_b[j], ssd_dt_bias[j], ssd_a_log[j],
                   ssd_d[j], ssd_norm_w[j], ssd_w_out[j])
            zero_conv = jnp.zeros((bp, CONV_WIDTH - 1, CONV_DIM), state_ssd_conv.dtype)
            zero_ssm = jnp.zeros((bp, SSD_HEADS, SSD_HEADDIM, D_STATE), state_ssd_ssm.dtype)
            mix_p, c_p, h_p = ssd_mixer(yp, zero_conv, zero_ssm, *prm)
            mix_s, c_s, h_s = ssd_mixer(ys, state_ssd_conv[j], state_ssd_ssm[j], *prm)
            cv_p.append(c_p); cv_s.append(c_s); ssm_p.append(h_p); ssm_s.append(h_s)
        yp = layer_norm(DEEPNORM_ALPHA * yp + mix_p, ln_gain[i, 0], ln_bias[i, 0])
        ys = layer_norm(DEEPNORM_ALPHA * ys + mix_s, ln_gain[i, 0], ln_bias[i, 0])
        moe = (moe_w_rg[i], moe_b_rg[i], moe_w_re[i], moe_b_re[i], moe_w_gate[i], moe_w_up[i], moe_w_down[i])
        yp = layer_norm(DEEPNORM_ALPHA * yp + hier_moe(yp, *moe), ln_gain[i, 1], ln_bias[i, 1])
        ys = layer_norm(DEEPNORM_ALPHA * ys + hier_moe(ys, *moe), ln_gain[i, 1], ln_bias[i, 1])
    new_wkv_p = jnp.stack(wkv_p); new_wkv_s = jnp.stack(wkv_s)
    new_sh_p = jnp.stack(sh_p); new_sh_s = jnp.stack(sh_s)
    new_ssm_p = jnp.stack(ssm_p); new_ssm_s = jnp.stack(ssm_s)
    new_cv_p = jnp.stack(cv_p); new_cv_s = jnp.stack(cv_s)
    return (yp, ys, new_wkv_p, new_wkv_s, new_sh_p, new_sh_s, new_ssm_p, new_ssm_s, new_cv_p, new_cv_s)
```

```python
import functools
import math

import jax
import jax.numpy as jnp
from jax import lax
from jax.experimental import pallas as pl
from jax.experimental.pallas import tpu as pltpu

F32 = jnp.float32
BF16 = jnp.bfloat16
I32 = jnp.int32

LANES = 128
HEAD = 64
GN_EPS = HEAD * 1e-5
RMS_EPS = 1e-5
LN_EPS = 1e-5
WKV_CHUNK = 64
SSD_CHUNK = 128
SSD_GROUPS = 4
D_STATE = 128
CONV_WIDTH = 4
EXPERT_GROUPS = 4
EXPERTS_PER_GROUP = 8
VMEM_LIMIT = 56 * 1024 * 1024

_HI = lax.Precision.HIGHEST


def _params(sem):
    return pltpu.CompilerParams(dimension_semantics=sem, vmem_limit_bytes=VMEM_LIMIT)


def _mm(a, b):
    return jnp.dot(a.astype(BF16), b.astype(BF16), preferred_element_type=F32)


def _mm_nt(a, b):
    return lax.dot_general(a.astype(BF16), b.astype(BF16), (((1,), (1,)), ((), ())),
                           preferred_element_type=F32)


def _mm_tn(a, b):
    return lax.dot_general(a.astype(BF16), b.astype(BF16), (((0,), (0,)), ((), ())),
                           preferred_element_type=F32)


def _mm_hi(a, b):
    return jnp.dot(a, b, precision=_HI, preferred_element_type=F32)


def _mm_nt_hi(a, b):
    return lax.dot_general(a, b, (((1,), (1,)), ((), ())), precision=_HI,
                           preferred_element_type=F32)


def _mm_tn_hi(a, b):
    return lax.dot_general(a, b, (((0,), (0,)), ((), ())), precision=_HI,
                           preferred_element_type=F32)


def _sigmoid(x):
    return 1.0 / (1.0 + jnp.exp(-x))


def _silu(x):
    return x * _sigmoid(x)


def _softplus(x):
    return jnp.maximum(x, 0.0) + jnp.log(1.0 + jnp.exp(-jnp.abs(x)))


def _layer_norm(x, g, b):
    mu = jnp.mean(x, axis=-1, keepdims=True)
    xc = x - mu
    var = jnp.mean(xc * xc, axis=-1, keepdims=True)
    return xc * lax.rsqrt(var + LN_EPS) * g + b


def _iota2(shape, dim):
    return lax.broadcasted_iota(I32, shape, dim)


def _rwkv_pre_kernel(x_ref, xp_ref, mu_ref, wrkv_ref, w0_ref, w1_ref, w2_ref, a0_ref, a1_ref,
                     a2_ref, g1_ref, g2_ref, r_ref, k_ref, v_ref, lw_ref, a_ref, g_ref):
    x = x_ref[...]
    xx = xp_ref[...] - x

    def mix(j):
        return x + xx * mu_ref[j:j + 1, :]

    r_ref[...] = _mm(mix(0), wrkv_ref[0])
    k_ref[...] = _mm(mix(1), wrkv_ref[1])
    v_ref[...] = _mm(mix(2), wrkv_ref[2])
    ww = w0_ref[...] + _mm(jnp.tanh(_mm(mix(3), w1_ref[...])), w2_ref[...])
    w_raw = -_softplus(-ww) - 0.5
    lw_ref[...] = -jnp.exp(w_raw)
    a_ref[...] = _sigmoid(a0_ref[...] + _mm(_mm(mix(4), a1_ref[...]), a2_ref[...]))
    g_ref[...] = _mm(_sigmoid(_mm(mix(5), g1_ref[...])), g2_ref[...])


def _rwkv_pre(x, xprev, mu, wrkv, w0, w1, w2, a0, a1, a2, g1, g2, tm):
    t, d = x.shape
    row = pl.BlockSpec((tm, d), lambda i: (i, 0))

    def full(arr):
        nd = arr.ndim
        return pl.BlockSpec(arr.shape, lambda i, _n=nd: (0,) * _n)

    weights = (mu, wrkv, w0, w1, w2, a0, a1, a2, g1, g2)
    return pl.pallas_call(
        _rwkv_pre_kernel,
        grid=(t // tm,),
        in_specs=[row, row] + [full(w) for w in weights],
        out_specs=[row] * 6,
        out_shape=[jax.ShapeDtypeStruct((t, d), F32)] * 6,
        compiler_params=_params(("parallel",)),
        name="rwkv_pre",
    )(x, xprev, *weights)


def _wkv_kernel(r_ref, k_ref, v_ref, lw_ref, a_ref, kkw_ref, kaw_ref, rk_ref, gnw_ref, gnb_ref,
                s0_ref, o_ref, sout_ref, s_sc, *, chunk):
    L = chunk
    c = pl.program_id(2)

    @pl.when(c == 0)
    def _():
        s_sc[...] = s0_ref[...]

    r = r_ref[...]
    k = k_ref[...]
    v = v_ref[...]
    lw = lw_ref[...]
    a = a_ref[...]

    lane = _iota2((1, LANES), 1)
    m0 = (lane < HEAD).astype(F32)
    m1 = 1.0 - m0
    same_head = (_iota2((LANES, LANES), 0) // HEAD == _iota2((LANES, LANES), 1) // HEAD).astype(F32)

    def headsum(t):
        return _mm_hi(t, same_head)

    def stack(t):
        return jnp.concatenate([t * m0, t * m1], axis=0)

    kk = k * kkw_ref[...]
    kk = kk / jnp.maximum(jnp.sqrt(headsum(kk * kk)), 1e-12)
    k2 = k * (1.0 + (a - 1.0) * kaw_ref[...])
    bvec = kk * a
    bonus = headsum(r * k2 * rk_ref[...]) * v

    tril_incl = (_iota2((L, L), 1) <= _iota2((L, L), 0)).astype(F32)
    cl = _mm_hi(tril_incl, lw)
    total = cl[L - 1:L, :]
    w_in = jnp.exp(cl)
    w_inv = jnp.exp(-cl)
    a_s = stack(-kk * jnp.exp(cl - lw))
    r_s = stack(r * w_in)
    b_s = stack(bvec * w_inv)
    k_s = stack(k2 * w_inv)
    w_end = jnp.exp(total - cl)
    bk_end = jnp.concatenate([stack(bvec * w_end), stack(k2 * w_end)], axis=0)
    v_s = stack(v)

    s = s_sc[...]
    cross = _mm_nt_hi(jnp.concatenate([a_s, r_s], axis=0), jnp.concatenate([b_s, k_s], axis=0))
    n2 = 2 * L
    row = _iota2((n2, n2), 0)
    col = _iota2((n2, n2), 1)
    strict = col < row
    incl = col <= row
    n_ab = jnp.where(strict, cross[:n2, :n2], 0.0)
    a_ak = jnp.where(strict, cross[:n2, n2:], 0.0)
    a_rb = jnp.where(incl, cross[n2:, :n2], 0.0)
    a_rk = jnp.where(incl, cross[n2:, n2:], 0.0)

    u = _mm_nt_hi(a_s, s) + _mm_hi(a_ak, v_s)
    p = n_ab
    steps = int(math.log2(L))
    for j in range(steps):
        u = u + _mm_hi(p, u)
        if j + 1 < steps:
            p = _mm_hi(p, p)

    y_s = _mm_nt_hi(r_s, s) + _mm_hi(a_rb, u) + _mm_hi(a_rk, v_s)
    y = y_s[:L] + y_s[L:]
    s_new = s * jnp.exp(total) + _mm_tn_hi(jnp.concatenate([u, v_s], axis=0), bk_end)
    s_sc[...] = s_new

    ym = headsum(y) * (1.0 / HEAD)
    yc = y - ym
    yv = headsum(yc * yc) * (1.0 / HEAD)
    o_ref[...] = yc * lax.rsqrt(yv + GN_EPS) * gnw_ref[...] + gnb_ref[...] + bonus

    @pl.when(c == pl.num_programs(2) - 1)
    def _():
        sout_ref[...] = s_new


def _wkv(r, k, v, lw, a, kkw, kaw, rk, gnw, gnb, s0, *, batch, seq, row0, chunk):
    d = r.shape[1]
    pairs = d // LANES
    nc = seq // chunk
    blk0 = row0 // chunk
    tok = pl.BlockSpec((chunk, LANES), lambda b, p, c: (blk0 + b * nc + c, p))
    out = pl.BlockSpec((chunk, LANES), lambda b, p, c: (b * nc + c, p))
    par = pl.BlockSpec((1, LANES), lambda b, p, c: (0, p))
    st = pl.BlockSpec((None, None, LANES, LANES), lambda b, p, c: (b, p, 0, 0))
    return pl.pallas_call(
        functools.partial(_wkv_kernel, chunk=chunk),
        grid=(batch, pairs, nc),
        in_specs=[tok] * 5 + [par] * 5 + [st],
        out_specs=[out, st],
        out_shape=[jax.ShapeDtypeStruct((batch * seq, d), F32),
                   jax.ShapeDtypeStruct((batch, pairs, LANES, LANES), F32)],
        scratch_shapes=[pltpu.VMEM((LANES, LANES), F32)],
        compiler_params=_params(("parallel", "parallel", "arbitrary")),
        name="wkv_l%d" % chunk,
    )(r, k, v, lw, a, kkw, kaw, rk, gnw, gnb, s0)


def _pair_state(s):
    b, h = s.shape[:2]
    s = s.reshape(b, h // 2, 2, HEAD, HEAD)
    eye = jnp.eye(2, dtype=s.dtype)
    out = s[:, :, :, :, None, :] * eye[None, None, :, None, :, None]
    return out.reshape(b, h // 2, LANES, LANES)


def _unpair_state(sp):
    b, p = sp.shape[:2]
    sp = sp.reshape(b, p, 2, HEAD, 2, HEAD)
    return jnp.stack([sp[:, :, 0, :, 0, :], sp[:, :, 1, :, 1, :]], axis=2).reshape(b, 2 * p, HEAD, HEAD)


def _proj_ln_kernel(op_ref, os_ref, g_ref, x_ref, w_ref, lng_ref, lnb_ref, out_ref, *, alpha, gated,
                    prompt_tiles):
    o = jnp.where(pl.program_id(0) < prompt_tiles, op_ref[...], os_ref[...])
    if gated:
        o = o * g_ref[...]
    y = _mm(o, w_ref[...])
    out_ref[...] = _layer_norm(alpha * x_ref[...] + y, lng_ref[...], lnb_ref[...])


def _proj_ln(o_p, o_s, g, x, w, lng, lnb, *, alpha, tm):
    t, d = x.shape
    kin = o_p.shape[1]
    npt = o_p.shape[0] // tm
    gated = g is not None
    g_spec = pl.BlockSpec((tm, kin), lambda i: (i, 0))
    if not gated:
        g, g_spec = lng, pl.BlockSpec((1, d), lambda i: (0, 0))
    return pl.pallas_call(
        functools.partial(_proj_ln_kernel, alpha=alpha, gated=gated, prompt_tiles=npt),
        grid=(t // tm,),
        in_specs=[pl.BlockSpec((tm, kin), lambda i: (jnp.minimum(i, npt - 1), 0)),
                  pl.BlockSpec((tm, kin), lambda i: (jnp.maximum(i - npt, 0), 0)),
                  g_spec,
                  pl.BlockSpec((tm, d), lambda i: (i, 0)),
                  pl.BlockSpec(w.shape, lambda i: (0, 0)),
                  pl.BlockSpec((1, d), lambda i: (0, 0)),
                  pl.BlockSpec((1, d), lambda i: (0, 0))],
        out_specs=pl.BlockSpec((tm, d), lambda i: (i, 0)),
        out_shape=jax.ShapeDtypeStruct((t, d), F32),
        compiler_params=_params(("parallel",)),
        name="proj_ln",
    )(o_p, o_s, g, x, w, lng, lnb)


def _router_kernel(x_ref, w_ref, b_ref, route_ref):
    ng, epg = EXPERT_GROUPS, EXPERTS_PER_GROUP
    logits = _mm_hi(x_ref[...], w_ref[...]) + b_ref[...]
    lane = _iota2(logits.shape, 1)
    neg = jnp.float32(-jnp.inf)
    big = jnp.int32(1 << 20)

    is_g = lane < ng
    lg = jnp.where(is_g, logits, neg)
    mg = jnp.max(lg, axis=-1, keepdims=True)
    g_sel = jnp.min(jnp.where(lg == mg, lane, big), axis=-1, keepdims=True)
    zg = jnp.sum(jnp.where(is_g, jnp.exp(lg - mg), 0.0), axis=-1, keepdims=True)
    p_group = 1.0 / zg

    e_lane = lane - ng
    in_sel = (e_lane >= g_sel * epg) & (e_lane < (g_sel + 1) * epg)
    le = jnp.where(in_sel, logits, neg)
    m1 = jnp.max(le, axis=-1, keepdims=True)
    i1 = jnp.min(jnp.where(le == m1, lane, big), axis=-1, keepdims=True)
    ze = jnp.sum(jnp.where(in_sel, jnp.exp(le - m1), 0.0), axis=-1, keepdims=True)
    le2 = jnp.where(lane == i1, neg, le)
    m2 = jnp.max(le2, axis=-1, keepdims=True)
    i2 = jnp.min(jnp.where(le2 == m2, lane, big), axis=-1, keepdims=True)
    pe1 = 1.0 / ze
    pe2 = jnp.exp(m2 - m1) / ze
    den = pe1 + pe2
    c1 = pe1 / den * p_group
    c2 = pe2 / den * p_group
    route_ref[...] = jnp.where(lane == 0, (i1 - ng).astype(F32),
                     jnp.where(lane == 1, (i2 - ng).astype(F32),
                     jnp.where(lane == 2, c1, jnp.where(lane == 3, c2, 0.0))))


def _router(x, w, b, tm):
    t, d = x.shape
    return pl.pallas_call(
        _router_kernel,
        grid=(t // tm,),
        in_specs=[pl.BlockSpec((tm, d), lambda i: (i, 0)),
                  pl.BlockSpec((d, LANES), lambda i: (0, 0)),
                  pl.BlockSpec((1, LANES), lambda i: (0, 0))],
        out_specs=pl.BlockSpec((tm, LANES), lambda i: (i, 0)),
        out_shape=jax.ShapeDtypeStruct((t, LANES), F32),
        compiler_params=_params(("parallel",)),
        name="moe_router",
    )(x, w, b)


def _slots_kernel(route_ref, pos_ref, ends_ref, cnt_sc, off_sc, *, tile_rows):
    ph = pl.program_id(0)
    i = pl.program_id(1)
    route = route_ref[...]
    tm = route.shape[0]
    lane = _iota2((tm, LANES), 1)
    e1 = route[:, 0:1].astype(I32)
    e2 = route[:, 1:2].astype(I32)
    oh1 = lane == e1
    oh2 = lane == e2
    oh = (oh1 | oh2).astype(F32)
    colsum = jnp.sum(oh, axis=0, keepdims=True)

    @pl.when((ph == 0) & (i == 0))
    def _():
        cnt_sc[...] = jnp.zeros_like(cnt_sc)

    @pl.when(ph == 0)
    def _():
        cnt_sc[...] += colsum

    @pl.when((ph == 1) & (i == 0))
    def _():
        padded = jnp.ceil(cnt_sc[...] * (1.0 / tile_rows)) * tile_rows
        before = (_iota2((LANES, LANES), 0) < _iota2((LANES, LANES), 1)).astype(F32)
        starts = _mm_hi(padded, before)
        off_sc[...] = starts
        ends_ref[...] = (starts + padded).astype(I32)

    @pl.when(ph == 1)
    def _():
        earlier = (_iota2((tm, tm), 1) < _iota2((tm, tm), 0)).astype(F32)
        base = off_sc[...] + _mm(earlier, oh)
        p1 = jnp.sum(jnp.where(oh1, base, 0.0), axis=-1, keepdims=True)
        p2 = jnp.sum(jnp.where(oh2, base, 0.0), axis=-1, keepdims=True)
        pos_ref[...] = jnp.where(lane == 0, p1, jnp.where(lane == 1, p2, 0.0)).astype(I32)
        off_sc[...] += colsum


def _slots(route, tm, tile_rows):
    t = route.shape[0]
    return pl.pallas_call(
        functools.partial(_slots_kernel, tile_rows=tile_rows),
        grid=(2, t // tm),
        in_specs=[pl.BlockSpec((tm, LANES), lambda ph, i: (i, 0))],
        out_specs=[pl.BlockSpec((tm, LANES), lambda ph, i: (i * ph, 0)),
                   pl.BlockSpec((1, LANES), lambda ph, i: (0, 0))],
        out_shape=[jax.ShapeDtypeStruct((t, LANES), I32),
                   jax.ShapeDtypeStruct((1, LANES), I32)],
        scratch_shapes=[pltpu.VMEM((1, LANES), F32), pltpu.VMEM((1, LANES), F32)],
        compiler_params=_params(("arbitrary", "arbitrary")),
        name="moe_slots",
    )(route)


def _scatter_kernel(pos_ref, x_ref, xs_in_ref, xs_ref, sem):
    del xs_in_ref
    i = pl.program_id(0)
    tm = x_ref.shape[0]

    def copy(t, c):
        p = pos_ref[2 * (i * tm + t) + c]
        return pltpu.make_async_copy(x_ref.at[pl.ds(t, 1)], xs_ref.at[pl.ds(p, 1)], sem)

    def start(t, carry):
        copy(t, 0).start()
        copy(t, 1).start()
        return carry

    def wait(t, carry):
        copy(t, 0).wait()
        copy(t, 1).wait()
        return carry

    lax.fori_loop(0, tm, start, 0)
    lax.fori_loop(0, tm, wait, 0)


def _scatter(pos_flat, x, n_slots, tm):
    t, d = x.shape
    xs0 = jnp.zeros((n_slots, d), F32)
    grid_spec = pltpu.PrefetchScalarGridSpec(
        num_scalar_prefetch=1,
        grid=(t // tm,),
        in_specs=[pl.BlockSpec((tm, d), lambda i, pos: (i, 0)),
                  pl.BlockSpec(memory_space=pl.ANY)],
        out_specs=pl.BlockSpec(memory_space=pl.ANY),
        scratch_shapes=[pltpu.SemaphoreType.DMA(())],
    )
    return pl.pallas_call(
        _scatter_kernel,
        grid_spec=grid_spec,
        out_shape=jax.ShapeDtypeStruct((n_slots, d), F32),
        input_output_aliases={2: 0},
        compiler_params=_params(("arbitrary",)),
        name="moe_scatter",
    )(pos_flat, x, xs0)


def _experts_kernel(te_ref, nu_ref, xs_ref, wg_ref, wu_ref, wd_ref, out_ref, wg_sc, wu_sc, wd_sc):
    j = pl.program_id(0)
    used = j < nu_ref[0]
    prev = te_ref[jnp.maximum(j - 1, 0)]

    @pl.when(used & ((j == 0) | (te_ref[j] != prev)))
    def _():
        wg_sc[...] = wg_ref[...].astype(BF16)
        wu_sc[...] = wu_ref[...].astype(BF16)
        wd_sc[...] = wd_ref[...].astype(BF16)

    @pl.when(used)
    def _():
        xb = xs_ref[...].astype(BF16)
        h = _silu(jnp.dot(xb, wg_sc[...], preferred_element_type=F32))
        h = h * jnp.dot(xb, wu_sc[...], preferred_element_type=F32)
        out_ref[...] = jnp.dot(h.astype(BF16), wd_sc[...], preferred_element_type=F32)

    @pl.when(jnp.logical_not(used))
    def _():
        out_ref[...] = jnp.zeros_like(out_ref)


def _experts(tile_expert, n_used, xs, wg, wu, wd, tile_rows):
    n_slots, d = xs.shape
    f = wg.shape[-1]

    def rows(j, te, nu):
        return (jnp.minimum(j, nu[0] - 1), 0)

    def wsel(j, te, nu):
        return (te[jnp.minimum(j, nu[0] - 1)], 0, 0)

    grid_spec = pltpu.PrefetchScalarGridSpec(
        num_scalar_prefetch=2,
        grid=(n_slots // tile_rows,),
        in_specs=[pl.BlockSpec((tile_rows, d), rows),
                  pl.BlockSpec((None, d, f), wsel),
                  pl.BlockSpec((None, d, f), wsel),
                  pl.BlockSpec((None, f, d), wsel)],
        out_specs=pl.BlockSpec((tile_rows, d), lambda j, te, nu: (j, 0)),
        scratch_shapes=[pltpu.VMEM((d, f), BF16), pltpu.VMEM((d, f), BF16), pltpu.VMEM((f, d), BF16)],
    )
    return pl.pallas_call(
        _experts_kernel,
        grid_spec=grid_spec,
        out_shape=jax.ShapeDtypeStruct((n_slots, d), F32),
        compiler_params=_params(("arbitrary",)),
        name="moe_experts",
    )(tile_expert, n_used, xs, wg, wu, wd)


def _combine_kernel(pos_ref, x_ref, route_ref, lng_ref, lnb_ref, ys_ref, out_ref, o1_sc, o2_sc, sem,
                    *, alpha):
    i = pl.program_id(0)
    tm = x_ref.shape[0]

    def copy(t, c):
        p = pos_ref[2 * (i * tm + t) + c]
        dst = o1_sc if c == 0 else o2_sc
        return pltpu.make_async_copy(ys_ref.at[pl.ds(p, 1)], dst.at[pl.ds(t, 1)], sem)

    def start(t, carry):
        copy(t, 0).start()
        copy(t, 1).start()
        return carry

    def wait(t, carry):
        copy(t, 0).wait()
        copy(t, 1).wait()
        return carry

    lax.fori_loop(0, tm, start, 0)
    lax.fori_loop(0, tm, wait, 0)
    route = route_ref[...]
    y = route[:, 2:3] * o1_sc[...] + route[:, 3:4] * o2_sc[...]
    out_ref[...] = _layer_norm(alpha * x_ref[...] + y, lng_ref[...], lnb_ref[...])


def _combine(pos_flat, x, route, lng, lnb, ys, *, alpha, tm):
    t, d = x.shape
    grid_spec = pltpu.PrefetchScalarGridSpec(
        num_scalar_prefetch=1,
        grid=(t // tm,),
        in_specs=[pl.BlockSpec((tm, d), lambda i, pos: (i, 0)),
                  pl.BlockSpec((tm, LANES), lambda i, pos: (i, 0)),
                  pl.BlockSpec((1, d), lambda i, pos: (0, 0)),
                  pl.BlockSpec((1, d), lambda i, pos: (0, 0)),
                  pl.BlockSpec(memory_space=pl.ANY)],
        out_specs=pl.BlockSpec((tm, d), lambda i, pos: (i, 0)),
        scratch_shapes=[pltpu.VMEM((tm, d), F32), pltpu.VMEM((tm, d), F32),
                        pltpu.SemaphoreType.DMA(())],
    )
    return pl.pallas_call(
        functools.partial(_combine_kernel, alpha=alpha),
        grid_spec=grid_spec,
        out_shape=jax.ShapeDtypeStruct((t, d), F32),
        compiler_params=_params(("arbitrary",)),
        name="moe_combine",
    )(pos_flat, x, route, lng, lnb, ys)


def _hier_moe_ln(x, w_rg, b_rg, w_re, b_re, w_gate, w_up, w_down, lng, lnb, *, alpha, tm, tile_rows):
    t, d = x.shape
    ng, epg = EXPERT_GROUPS, EXPERTS_PER_GROUP
    n_exp = ng * epg
    pad = LANES - ng - n_exp
    w_router = jnp.pad(jnp.concatenate([w_rg, w_re], axis=1), ((0, 0), (0, pad)))
    b_router = jnp.pad(jnp.concatenate([b_rg, b_re]), (0, pad))[None, :]
    route = _router(x, w_router, b_router, tm)
    pos, ends = _slots(route, tm, tile_rows)
    pos_flat = pos[:, :2].reshape(-1)
    n_slots = 2 * t + n_exp * tile_rows
    n_tiles = n_slots // tile_rows
    seg_end = ends[0, :n_exp]
    tile_start = jnp.arange(n_tiles, dtype=I32) * tile_rows
    tile_expert = jnp.minimum(jnp.sum(tile_start[:, None] >= seg_end[None, :], axis=1), n_exp - 1).astype(I32)
    n_used = (seg_end[n_exp - 1:] // tile_rows).astype(I32)
    xs = _scatter(pos_flat, x, n_slots, tm)
    f = w_gate.shape[-1]
    ys = _experts(tile_expert, n_used, xs, w_gate.reshape(n_exp, d, f), w_up.reshape(n_exp, d, f),
                  w_down.reshape(n_exp, f, d), tile_rows)
    return _combine(pos_flat, x, route, lng, lnb, ys, alpha=alpha, tm=tm)


def _ssd_in_kernel(x_ref, wz_ref, wx_ref, wdt_ref, z_ref, xbc_ref, dt_ref):
    xb = x_ref[...].astype(BF16)
    z_ref[...] = jnp.dot(xb, wz_ref[...], preferred_element_type=F32)
    xbc_ref[...] = jnp.dot(xb, wx_ref[...], preferred_element_type=F32)
    dt_ref[...] = jnp.dot(xb, wdt_ref[...], preferred_element_type=F32)


def _ssd_in(x, wz, wx, wdt, tm):
    t, d = x.shape
    outs = [wz.shape[1], wx.shape[1], wdt.shape[1]]
    return pl.pallas_call(
        _ssd_in_kernel,
        grid=(t // tm,),
        in_specs=[pl.BlockSpec((tm, d), lambda i: (i, 0))]
                 + [pl.BlockSpec(w.shape, lambda i: (0, 0)) for w in (wz, wx, wdt)],
        out_specs=[pl.BlockSpec((tm, n), lambda i: (i, 0)) for n in outs],
        out_shape=[jax.ShapeDtypeStruct((t, n), F32) for n in outs],
        compiler_params=_params(("parallel",)),
        name="ssd_in",
    )(x, wz, wx, wdt)


def _ssd_kernel(xbc_ref, dt_ref, z_ref, conv0_ref, h0_ref, cw_ref, cb_ref, dtb_ref, alog_ref,
                dexp_ref, nw_ref, expand_ref, y_ref, convo_ref, hout_ref, pad_sc, ht_sc, *, chunk):
    c = chunk
    ci = pl.program_id(1)
    d_inner = z_ref.shape[1]
    gn = SSD_GROUPS * D_STATE
    per_group = d_inner // SSD_GROUPS
    tail = 8

    @pl.when(ci == 0)
    def _():
        pad_sc[0:tail, :] = conv0_ref[...]
        ht_sc[...] = h0_ref[...].T

    pad_sc[tail:tail + c, :] = xbc_ref[...]
    acc = cb_ref[...]
    for i in range(CONV_WIDTH):
        off = tail - (CONV_WIDTH - 1) + i
        acc = acc + pad_sc[off:off + c, :] * cw_ref[i:i + 1, :]
    xbc = _silu(acc)
    last_rows = pad_sc[c:c + tail, :]
    pad_sc[0:tail, :] = last_rows
    xs = xbc[:, :d_inner]

    dt = _softplus(dt_ref[...] + dtb_ref[...])
    da = dt * (-jnp.exp(alog_ref[...]))
    tril_incl = (_iota2((c, c), 1) <= _iota2((c, c), 0)).astype(F32)
    acum = _mm_hi(tril_incl, da)
    a_last = acum[c - 1:c, :]

    def to_cols(t):
        if c < LANES:
            t = jnp.concatenate([t, jnp.zeros((LANES - c, LANES), F32)], axis=0)
        return t.T[:, :c]

    acum_t = to_cols(acum)
    dt_t = to_cols(dt)
    expand = expand_ref[...]
    ea = _mm_hi(jnp.exp(acum), expand)
    dd = _mm_hi(jnp.exp(a_last - acum) * dt, expand)
    ea_last = _mm_hi(jnp.exp(a_last), expand)
    causal = _iota2((c, c), 1) <= _iota2((c, c), 0)
    lane = _iota2((1, LANES), 1)
    m0 = (lane < HEAD).astype(F32)
    m1 = 1.0 - m0

    ys = []
    for g in range(SSD_GROUPS):
        b_g = xbc[:, d_inner + g * D_STATE:d_inner + (g + 1) * D_STATE]
        c_g = xbc[:, d_inner + gn + g * D_STATE:d_inner + gn + (g + 1) * D_STATE]
        cb = _mm_nt(c_g, b_g)
        gsl = slice(g * per_group, (g + 1) * per_group)
        ht_g = ht_sc[:, gsl]
        y_off = _mm(c_g, ht_g) * ea[:, gsl]
        ht_sc[:, gsl] = ht_g * ea_last[:, gsl] + _mm_tn(b_g, xs[:, gsl] * dd[:, gsl])
        pieces = []
        for q in range(per_group // LANES):
            j0 = (g * per_group) // HEAD + 2 * q
            ms = []
            for j in (j0, j0 + 1):
                seg = acum[:, j:j + 1] - acum_t[j:j + 1, :]
                ms.append(jnp.where(causal, cb * jnp.exp(jnp.where(causal, seg, 0.0)) * dt_t[j:j + 1, :], 0.0))
            lo = g * per_group + q * LANES
            x_pair = xs[:, lo:lo + LANES]
            x_stack = jnp.concatenate([x_pair * m0, x_pair * m1], axis=0)
            pieces.append(_mm(jnp.concatenate(ms, axis=1), x_stack))
        y_g = jnp.concatenate(pieces, axis=1) + y_off + xs[:, gsl] * dexp_ref[:, gsl]
        y_g = y_g * _silu(z_ref[:, gsl])
        y_g = y_g * lax.rsqrt(jnp.mean(y_g * y_g, axis=-1, keepdims=True) + RMS_EPS)
        ys.append(y_g * nw_ref[:, gsl])
    y_ref[...] = jnp.concatenate(ys, axis=1)

    @pl.when(ci == pl.num_programs(1) - 1)
    def _():
        convo_ref[...] = last_rows
        hout_ref[...] = ht_sc[...].T


def _ssd(xbc, dt, z, conv0, h0, cw, cb, dtb, alog, dexp, nw, expand, *, batch, seq, row0, chunk):
    d_inner = z.shape[1]
    conv_dim = xbc.shape[1]
    nc = seq // chunk
    blk0 = row0 // chunk

    def tok(n):
        return pl.BlockSpec((chunk, n), lambda b, c: (blk0 + b * nc + c, 0))

    def par(arr):
        return pl.BlockSpec(arr.shape, lambda b, c: (0, 0))

    return pl.pallas_call(
        functools.partial(_ssd_kernel, chunk=chunk),
        grid=(batch, nc),
        in_specs=[tok(conv_dim), tok(LANES), tok(d_inner),
                  pl.BlockSpec((None, 8, conv_dim), lambda b, c: (b, 0, 0)),
                  pl.BlockSpec((None, d_inner, D_STATE), lambda b, c: (b, 0, 0)),
                  par(cw), par(cb), par(dtb), par(alog), par(dexp), par(nw), par(expand)],
        out_specs=[pl.BlockSpec((chunk, d_inner), lambda b, c: (b * nc + c, 0)),
                   pl.BlockSpec((None, 8, conv_dim), lambda b, c: (b, 0, 0)),
                   pl.BlockSpec((None, d_inner, D_STATE), lambda b, c: (b, 0, 0))],
        out_shape=[jax.ShapeDtypeStruct((batch * seq, d_inner), F32),
                   jax.ShapeDtypeStruct((batch, 8, conv_dim), F32),
                   jax.ShapeDtypeStruct((batch, d_inner, D_STATE), F32)],
        scratch_shapes=[pltpu.VMEM((chunk + 8, conv_dim), F32), pltpu.VMEM((D_STATE, d_inner), F32)],
        compiler_params=_params(("parallel", "arbitrary")),
        name="ssd_l%d" % chunk,
    )(xbc, dt, z, conv0, h0, cw, cb, dtb, alog, dexp, nw, expand)


def _rwkv_layer(x, xp3, xs3, shift_s, wkv_s, prm, lng, lnb, *, alpha, tm):
    (mu, w_rkv, w0, w1, w2, a0, a1, a2, g1, g2, k_k, k_a, r_k, gn_w, gn_b, w_o) = prm
    bp, sp, d = xp3.shape
    bs, ss, _ = xs3.shape
    tp = bp * sp
    heads = d // HEAD
    prev_p = jnp.concatenate([jnp.zeros((bp, 1, d), F32), xp3[:, :-1]], axis=1).reshape(tp, d)
    prev_s = jnp.concatenate([shift_s[:, None, :], xs3[:, :-1]], axis=1).reshape(bs * ss, d)
    xprev = jnp.concatenate([prev_p, prev_s], axis=0)
    row = lambda v: v.reshape(1, d)
    r, k, v, lw, a, g = _rwkv_pre(x, xprev, mu, w_rkv.astype(BF16), row(w0), w1.astype(BF16),
                                  w2.astype(BF16), row(a0), a1.astype(BF16), a2.astype(BF16),
                                  g1.astype(BF16), g2.astype(BF16), tm)
    pv = (row(k_k), row(k_a), row(r_k), row(gn_w), row(gn_b))
    zero_state = jnp.zeros((bp, heads // 2, LANES, LANES), F32)
    o_p, sp_out = _wkv(r, k, v, lw, a, *pv, zero_state, batch=bp, seq=sp, row0=0,
                       chunk=min(WKV_CHUNK, sp))
    o_s, ss_out = _wkv(r, k, v, lw, a, *pv, _pair_state(wkv_s), batch=bs, seq=ss, row0=tp,
                       chunk=min(WKV_CHUNK, ss))
    x1 = _proj_ln(o_p, o_s, g, x, w_o.astype(BF16), lng, lnb, alpha=alpha, tm=tm)
    return x1, _unpair_state(sp_out), _unpair_state(ss_out)


def _ssd_layer(x, bp, sp, bs, ss, conv_s, ssm_s, prm, lng, lnb, *, alpha, tm):
    (w_in, conv_w, conv_b, dt_bias, a_log, d_skip, norm_w, w_out) = prm
    t, d = x.shape
    tp = bp * sp
    heads = a_log.shape[0]
    d_inner = heads * HEAD
    conv_dim = conv_w.shape[1]
    wz = w_in[:, :d_inner].astype(BF16)
    wx = w_in[:, d_inner:d_inner + conv_dim].astype(BF16)
    wdt = jnp.pad(w_in[:, d_inner + conv_dim:], ((0, 0), (0, LANES - heads))).astype(BF16)
    z, xbc, dt = _ssd_in(x, wz, wx, wdt, tm)
    lane_pad = lambda v: jnp.pad(v, (0, LANES - heads))[None, :]
    expand = (jnp.arange(LANES)[:, None] == (jnp.arange(d_inner) // HEAD)[None, :]).astype(F32)
    common = (conv_w, conv_b[None, :], lane_pad(dt_bias), lane_pad(a_log),
              jnp.repeat(d_skip, HEAD)[None, :], norm_w[None, :], expand)
    conv_tail = CONV_WIDTH - 1
    pad_conv = lambda cs: jnp.pad(cs, ((0, 0), (8 - conv_tail, 0), (0, 0)))
    y_p, conv_p, h_p = _ssd(xbc, dt, z, jnp.zeros((bp, 8, conv_dim), F32),
                            jnp.zeros((bp, d_inner, D_STATE), F32), *common,
                            batch=bp, seq=sp, row0=0, chunk=SSD_CHUNK if sp % SSD_CHUNK == 0 else sp)
    y_s, conv_s_out, h_s = _ssd(xbc, dt, z, pad_conv(conv_s), ssm_s.reshape(bs, d_inner, D_STATE), *common,
                                batch=bs, seq=ss, row0=tp, chunk=SSD_CHUNK if ss % SSD_CHUNK == 0 else ss)
    x1 = _proj_ln(y_p, y_s, None, x, w_out.astype(BF16), lng, lnb, alpha=alpha, tm=tm)
    shape_h = lambda h, b: h.reshape(b, heads, HEAD, D_STATE)
    return (x1, conv_p[:, 8 - conv_tail:], conv_s_out[:, 8 - conv_tail:], shape_h(h_p, bp), shape_h(h_s, bs))


def kernel(x_prompt, x_sample, state_rwkv_wkv, state_rwkv_shift, state_ssd_ssm, state_ssd_conv,
           rwkv_mu, rwkv_w_rkv, rwkv_w0, rwkv_w1, rwkv_w2, rwkv_a0, rwkv_a1, rwkv_a2,
           rwkv_g1, rwkv_g2, rwkv_k_k, rwkv_k_a, rwkv_r_k, rwkv_gn_w, rwkv_gn_b, rwkv_w_o,
           ssd_w_in, ssd_conv_w, ssd_conv_b, ssd_dt_bias, ssd_a_log, ssd_d, ssd_norm_w, ssd_w_out,
           ln_gain, ln_bias, moe_w_rg, moe_b_rg, moe_w_re, moe_b_re, moe_w_gate, moe_w_up, moe_w_down):
    bp, sp, d = x_prompt.shape
    bs, ss, _ = x_sample.shape
    depth = ln_gain.shape[0]
    alpha = (2 * depth) ** 0.25
    tp, ts = bp * sp, bs * ss
    tm = 256
    tile_rows = 256
    x = jnp.concatenate([x_prompt.reshape(tp, d), x_sample.reshape(ts, d)], axis=0)
    wkv_p, wkv_s, sh_p, sh_s, ssm_p, ssm_s, cv_p, cv_s = [], [], [], [], [], [], [], []
    for i in range(depth):
        j = i // 2
        lng = lambda n: ln_gain[i, n][None, :]
        lnb = lambda n: ln_bias[i, n][None, :]
        xp3 = x[:tp].reshape(bp, sp, d)
        xs3 = x[tp:].reshape(bs, ss, d)
        if i % 2 == 0:
            prm = (rwkv_mu[j], rwkv_w_rkv[j], rwkv_w0[j], rwkv_w1[j], rwkv_w2[j], rwkv_a0[j],
                   rwkv_a1[j], rwkv_a2[j], rwkv_g1[j], rwkv_g2[j], rwkv_k_k[j], rwkv_k_a[j],
                   rwkv_r_k[j].reshape(-1), rwkv_gn_w[j], rwkv_gn_b[j], rwkv_w_o[j])
            sh_p.append(xp3[:, -1])
            sh_s.append(xs3[:, -1])
            x, w_p, w_s = _rwkv_layer(x, xp3, xs3, state_rwkv_shift[j], state_rwkv_wkv[j], prm,
                                      lng(0), lnb(0), alpha=alpha, tm=tm)
            wkv_p.append(w_p)
            wkv_s.append(w_s)
        else:
            prm = (ssd_w_in[j], ssd_conv_w[j], ssd_conv_b[j], ssd_dt_bias[j], ssd_a_log[j],
                   ssd_d[j], ssd_norm_w[j], ssd_w_out[j])
            x, c_p, c_s, h_p, h_s = _ssd_layer(x, bp, sp, bs, ss, state_ssd_conv[j], state_ssd_ssm[j],
                                               prm, lng(0), lnb(0), alpha=alpha, tm=tm)
            cv_p.append(c_p)
            cv_s.append(c_s)
            ssm_p.append(h_p)
            ssm_s.append(h_s)
        x = _hier_moe_ln(x, moe_w_rg[i], moe_b_rg[i], moe_w_re[i], moe_b_re[i], moe_w_gate[i],
                         moe_w_up[i], moe_w_down[i], lng(1), lnb(1), alpha=alpha, tm=tm,
                         tile_rows=tile_rows)
    yp = x[:tp].reshape(bp, sp, d)
    ys = x[tp:].reshape(bs, ss, d)
    return (yp, ys, jnp.stack(wkv_p), jnp.stack(wkv_s), jnp.stack(sh_p), jnp.stack(sh_s),
            jnp.stack(ssm_p), jnp.stack(ssm_s), jnp.stack(cv_p), jnp.stack(cv_s))
```

```python
import functools
import math

import jax
import jax.numpy as jnp
from jax import lax
from jax.experimental import pallas as pl
from jax.experimental.pallas import tpu as pltpu

F32 = jnp.float32
BF16 = jnp.bfloat16
I32 = jnp.int32

LANES = 128
HEAD = 64
GN_EPS = HEAD * 1e-5
RMS_EPS = 1e-5
LN_EPS = 1e-5
WKV_CHUNK = 64
WKV_PAIRS_PER_STEP = 4
SSD_CHUNK = 128
SSD_GROUPS = 4
D_STATE = 128
CONV_WIDTH = 4
EXPERT_GROUPS = 4
EXPERTS_PER_GROUP = 8
VMEM_LIMIT = 56 * 1024 * 1024

_HI = lax.Precision.HIGHEST


def _params(sem):
    return pltpu.CompilerParams(dimension_semantics=sem, vmem_limit_bytes=VMEM_LIMIT)


def _mm(a, b):
    return jnp.dot(a.astype(BF16), b.astype(BF16), preferred_element_type=F32)


def _mm_nt(a, b):
    return lax.dot_general(a.astype(BF16), b.astype(BF16), (((1,), (1,)), ((), ())),
                           preferred_element_type=F32)


def _mm_tn(a, b):
    return lax.dot_general(a.astype(BF16), b.astype(BF16), (((0,), (0,)), ((), ())),
                           preferred_element_type=F32)


def _mm_hi(a, b):
    return jnp.dot(a, b, precision=_HI, preferred_element_type=F32)


def _split3(x):
    hi = x.astype(BF16)
    r1 = x - hi.astype(F32)
    mid = r1.astype(BF16)
    lo = (r1 - mid.astype(F32)).astype(BF16)
    return hi, mid, lo


def _mm_sel_l(sel, x):
    sb = sel.astype(BF16)
    return sum(jnp.dot(sb, p, preferred_element_type=F32) for p in _split3(x))


def _mm_sel_r(x, sel):
    sb = sel.astype(BF16)
    return sum(jnp.dot(p, sb, preferred_element_type=F32) for p in _split3(x))


def _sigmoid(x):
    return 1.0 / (1.0 + jnp.exp(-x))


def _silu(x):
    return x * _sigmoid(x)


def _softplus(x):
    return jnp.maximum(x, 0.0) + jnp.log(1.0 + jnp.exp(-jnp.abs(x)))


def _layer_norm(x, g, b):
    mu = jnp.mean(x, axis=-1, keepdims=True)
    xc = x - mu
    var = jnp.mean(xc * xc, axis=-1, keepdims=True)
    return xc * lax.rsqrt(var + LN_EPS) * g + b


def _iota2(shape, dim):
    return lax.broadcasted_iota(I32, shape, dim)


def _part_specs(tm, width, prompt_tiles, sample_block0=0):
    return [pl.BlockSpec((tm, width), lambda i: (jnp.minimum(i, prompt_tiles - 1), 0)),
            pl.BlockSpec((tm, width), lambda i: (sample_block0 + jnp.maximum(i - prompt_tiles, 0), 0))]


def _rwkv_pre_kernel(xa_ref, xb_ref, tail_ref, xprev_s_ref, mu_ref, wrkv_ref, w0_ref, w1_ref, w2_ref,
                     a0_ref, a1_ref, a2_ref, g1_ref, g2_ref, r_ref, k_ref, v_ref, lw_ref, a_ref, g_ref,
                     *, prompt_tiles, seq):
    i = pl.program_id(0)
    tm = xa_ref.shape[0]
    is_prompt = i < prompt_tiles
    xa = xa_ref[...]
    x = jnp.where(is_prompt, xa, xb_ref[...])
    first = jnp.where((i * tm) % seq == 0, 0.0, tail_ref[7:8, :])
    shifted = jnp.where(_iota2(xa.shape, 0) == 0, first, pltpu.roll(xa, 1, axis=0))
    xx = jnp.where(is_prompt, shifted, xprev_s_ref[...]) - x

    def mix(j):
        return x + xx * mu_ref[j:j + 1, :]

    r_ref[...] = _mm(mix(0), wrkv_ref[0])
    k_ref[...] = _mm(mix(1), wrkv_ref[1])
    v_ref[...] = _mm(mix(2), wrkv_ref[2])
    ww = w0_ref[...] + _mm(jnp.tanh(_mm(mix(3), w1_ref[...])), w2_ref[...])
    w_raw = -_softplus(-ww) - 0.5
    lw_ref[...] = -jnp.exp(w_raw)
    a_ref[...] = _sigmoid(a0_ref[...] + _mm(_mm(mix(4), a1_ref[...]), a2_ref[...]))
    g_ref[...] = _mm(_sigmoid(_mm(mix(5), g1_ref[...])), g2_ref[...])


def _rwkv_pre(x_p, x_s, xprev_s, seq, mu, wrkv, w0, w1, w2, a0, a1, a2, g1, g2, tm):
    (tp, d), ts = x_p.shape, x_s.shape[0]
    assert seq % tm == 0 and ts % tm == 0 and tm % 8 == 0
    npt = tp // tm
    row = pl.BlockSpec((tm, d), lambda i: (i, 0))
    tail = pl.BlockSpec((8, d), lambda i: (jnp.maximum(jnp.minimum(i, npt - 1) * (tm // 8) - 1, 0), 0))

    def full(arr):
        nd = arr.ndim
        return pl.BlockSpec(arr.shape, lambda i, _n=nd: (0,) * _n)

    weights = (mu, wrkv, w0, w1, w2, a0, a1, a2, g1, g2)
    return pl.pallas_call(
        functools.partial(_rwkv_pre_kernel, prompt_tiles=npt, seq=seq),
        grid=((tp + ts) // tm,),
        in_specs=_part_specs(tm, d, npt) + [tail, _part_specs(tm, d, npt)[1]] + [full(w) for w in weights],
        out_specs=[row] * 6,
        out_shape=[jax.ShapeDtypeStruct((tp + ts, d), F32)] * 6,
        compiler_params=_params(("parallel",)),
        name="rwkv_pre",
    )(x_p, x_s, x_p, xprev_s, *weights)


def _wkv_kernel(r_ref, k_ref, v_ref, lw_ref, a_ref, kkw_ref, kaw_ref, rk_ref, gnw_ref, gnb_ref,
                *rest, chunk, pairs_per_step, has_init):
    if has_init:
        s0_ref, o_ref, sout_ref, s_sc = rest
    else:
        o_ref, sout_ref, s_sc = rest
    L = chunk
    c = pl.program_id(2)

    @pl.when(c == 0)
    def _():
        if has_init:
            zero = jnp.zeros((HEAD, HEAD), F32)
            for q in range(pairs_per_step):
                top = jnp.concatenate([s0_ref[2 * q], zero], axis=1)
                bot = jnp.concatenate([zero, s0_ref[2 * q + 1]], axis=1)
                s_sc[q] = jnp.concatenate([top, bot], axis=0)
        else:
            s_sc[...] = jnp.zeros_like(s_sc)

    lane = _iota2((1, LANES), 1)
    m0 = (lane < HEAD).astype(F32)
    m1 = 1.0 - m0
    same_head = (_iota2((LANES, LANES), 0) // HEAD == _iota2((LANES, LANES), 1) // HEAD).astype(F32)
    tril_incl = (_iota2((L, L), 1) <= _iota2((L, L), 0)).astype(F32)
    n2 = 2 * L
    row = _iota2((n2, n2), 0)
    col = _iota2((n2, n2), 1)
    strict = col < row
    incl = col <= row
    steps = int(math.log2(L))

    def headsum(t):
        return _mm_sel_r(t, same_head)

    def stack(t):
        return jnp.concatenate([t * m0, t * m1], axis=0)

    pairs = range(pairs_per_step)
    lanes = [slice(q * LANES, (q + 1) * LANES) for q in pairs]

    def prepare(ls):
        r = r_ref[:, ls]
        k = k_ref[:, ls]
        v = v_ref[:, ls]
        lw = lw_ref[:, ls]
        a = a_ref[:, ls]
        kk = k * kkw_ref[:, ls]
        kk = kk / jnp.maximum(jnp.sqrt(headsum(kk * kk)), 1e-12)
        k2 = k * (1.0 + (a - 1.0) * kaw_ref[:, ls])
        bvec = kk * a
        bonus = headsum(r * k2 * rk_ref[:, ls]) * v
        cl = _mm_sel_l(tril_incl, lw)
        total = cl[L - 1:L, :]
        w_in = jnp.exp(cl)
        w_inv = jnp.exp(-cl)
        w_end = jnp.exp(total - cl)
        return dict(
            a_s=stack(-kk * jnp.exp(cl - lw)), r_s=stack(r * w_in), b_s=stack(bvec * w_inv),
            k_s=stack(k2 * w_inv), v_s=stack(v), total=total, bonus=bonus,
            bk_end=jnp.concatenate([stack(bvec * w_end), stack(k2 * w_end)], axis=0))

    ops = [prepare(ls) for ls in lanes]
    states = [s_sc[q] for q in pairs]
    cross = [_mm_nt(jnp.concatenate([o["a_s"], o["r_s"]], axis=0),
                    jnp.concatenate([o["b_s"], o["k_s"]], axis=0)) for o in ops]
    n_ab = [jnp.where(strict, x[:n2, :n2], 0.0) for x in cross]
    a_ak = [jnp.where(strict, x[:n2, n2:], 0.0) for x in cross]
    a_rb = [jnp.where(incl, x[n2:, :n2], 0.0) for x in cross]
    a_rk = [jnp.where(incl, x[n2:, n2:], 0.0) for x in cross]

    u = [_mm_nt(o["a_s"], s) + _mm(m, o["v_s"]) for o, s, m in zip(ops, states, a_ak)]
    p = n_ab
    for j in range(steps):
        u = [ui + _mm(pi, ui) for ui, pi in zip(u, p)]
        if j + 1 < steps:
            p = [_mm(pi, pi) for pi in p]

    y_s = [_mm_nt(o["r_s"], s) + _mm(mb, ui) + _mm(mk, o["v_s"])
           for o, s, mb, ui, mk in zip(ops, states, a_rb, u, a_rk)]
    s_new = [s * jnp.exp(o["total"]) + _mm_tn(jnp.concatenate([ui, o["v_s"]], axis=0), o["bk_end"])
             for o, s, ui in zip(ops, states, u)]
    for q in pairs:
        s_sc[q] = s_new[q]
        y = y_s[q][:L] + y_s[q][L:]
        ym = headsum(y) * (1.0 / HEAD)
        yc = y - ym
        yv = headsum(yc * yc) * (1.0 / HEAD)
        ls = lanes[q]
        o_ref[:, ls] = yc * lax.rsqrt(yv + GN_EPS) * gnw_ref[:, ls] + gnb_ref[:, ls] + ops[q]["bonus"]

    @pl.when(c == pl.num_programs(2) - 1)
    def _():
        for q in pairs:
            sout_ref[2 * q] = s_new[q][:HEAD, :HEAD]
            sout_ref[2 * q + 1] = s_new[q][HEAD:, HEAD:]


def _wkv(r, k, v, lw, a, kkw, kaw, rk, gnw, gnb, s0, *, batch, seq, row0, chunk, pairs_per_step):
    d = r.shape[1]
    pps = pairs_per_step
    width = pps * LANES
    nc = seq // chunk
    blk0 = row0 // chunk
    tok = pl.BlockSpec((chunk, width), lambda b, p, c: (blk0 + b * nc + c, p))
    out = pl.BlockSpec((chunk, width), lambda b, p, c: (b * nc + c, p))
    par = pl.BlockSpec((1, width), lambda b, p, c: (0, p))
    st = pl.BlockSpec((None, 2 * pps, HEAD, HEAD), lambda b, p, c: (b, p, 0, 0))
    has_init = s0 is not None
    return pl.pallas_call(
        functools.partial(_wkv_kernel, chunk=chunk, pairs_per_step=pps, has_init=has_init),
        grid=(batch, d // width, nc),
        in_specs=[tok] * 5 + [par] * 5 + ([st] if has_init else []),
        out_specs=[out, st],
        out_shape=[jax.ShapeDtypeStruct((batch * seq, d), F32),
                   jax.ShapeDtypeStruct((batch, d // HEAD, HEAD, HEAD), F32)],
        scratch_shapes=[pltpu.VMEM((pps, LANES, LANES), F32)],
        compiler_params=_params(("parallel", "parallel", "arbitrary")),
        name="wkv_l%d" % chunk,
    )(r, k, v, lw, a, kkw, kaw, rk, gnw, gnb, *([s0] if has_init else []))


def _proj_ln_kernel(op_ref, os_ref, g_ref, xa_ref, xb_ref, w_ref, lng_ref, lnb_ref, out_ref, *, alpha,
                    gated, prompt_tiles):
    is_prompt = pl.program_id(0) < prompt_tiles
    o = jnp.where(is_prompt, op_ref[...], os_ref[...])
    if gated:
        o = o * g_ref[...]
    y = _mm(o, w_ref[...])
    x = jnp.where(is_prompt, xa_ref[...], xb_ref[...])
    out_ref[...] = _layer_norm(alpha * x + y, lng_ref[...], lnb_ref[...])


def _proj_ln(o_p, o_s, g, x_parts, w, lng, lnb, *, alpha, tm):
    x_a, x_b, sample_block0 = x_parts
    d = x_a.shape[1]
    kin = o_p.shape[1]
    npt = o_p.shape[0] // tm
    t = o_p.shape[0] + o_s.shape[0]
    gated = g is not None
    g_spec = pl.BlockSpec((tm, kin), lambda i: (i, 0))
    if not gated:
        g, g_spec = lng, pl.BlockSpec((1, d), lambda i: (0, 0))
    return pl.pallas_call(
        functools.partial(_proj_ln_kernel, alpha=alpha, gated=gated, prompt_tiles=npt),
        grid=(t // tm,),
        in_specs=_part_specs(tm, kin, npt) + [g_spec] + _part_specs(tm, d, npt, sample_block0)
                 + [pl.BlockSpec(w.shape, lambda i: (0, 0)),
                    pl.BlockSpec((1, d), lambda i: (0, 0)),
                    pl.BlockSpec((1, d), lambda i: (0, 0))],
        out_specs=pl.BlockSpec((tm, d), lambda i: (i, 0)),
        out_shape=jax.ShapeDtypeStruct((t, d), F32),
        compiler_params=_params(("parallel",)),
        name="proj_ln",
    )(o_p, o_s, g, x_a, x_b, w, lng, lnb)


def _router_kernel(x_ref, w_ref, b_ref, route_ref):
    ng, epg = EXPERT_GROUPS, EXPERTS_PER_GROUP
    logits = _mm_hi(x_ref[...], w_ref[...]) + b_ref[...]
    lane = _iota2(logits.shape, 1)
    neg = jnp.float32(-jnp.inf)
    big = jnp.int32(1 << 20)

    is_g = lane < ng
    lg = jnp.where(is_g, logits, neg)
    mg = jnp.max(lg, axis=-1, keepdims=True)
    g_sel = jnp.min(jnp.where(lg == mg, lane, big), axis=-1, keepdims=True)
    zg = jnp.sum(jnp.where(is_g, jnp.exp(lg - mg), 0.0), axis=-1, keepdims=True)
    p_group = 1.0 / zg

    e_lane = lane - ng
    in_sel = (e_lane >= g_sel * epg) & (e_lane < (g_sel + 1) * epg)
    le = jnp.where(in_sel, logits, neg)
    m1 = jnp.max(le, axis=-1, keepdims=True)
    i1 = jnp.min(jnp.where(le == m1, lane, big), axis=-1, keepdims=True)
    ze = jnp.sum(jnp.where(in_sel, jnp.exp(le - m1), 0.0), axis=-1, keepdims=True)
    le2 = jnp.where(lane == i1, neg, le)
    m2 = jnp.max(le2, axis=-1, keepdims=True)
    i2 = jnp.min(jnp.where(le2 == m2, lane, big), axis=-1, keepdims=True)
    pe1 = 1.0 / ze
    pe2 = jnp.exp(m2 - m1) / ze
    den = pe1 + pe2
    c1 = pe1 / den * p_group
    c2 = pe2 / den * p_group
    route_ref[...] = jnp.where(lane == 0, (i1 - ng).astype(F32),
                     jnp.where(lane == 1, (i2 - ng).astype(F32),
                     jnp.where(lane == 2, c1, jnp.where(lane == 3, c2, 0.0))))


def _router(x, w, b, tm):
    t, d = x.shape
    return pl.pallas_call(
        _router_kernel,
        grid=(t // tm,),
        in_specs=[pl.BlockSpec((tm, d), lambda i: (i, 0)),
                  pl.BlockSpec((d, LANES), lambda i: (0, 0)),
                  pl.BlockSpec((1, LANES), lambda i: (0, 0))],
        out_specs=pl.BlockSpec((tm, LANES), lambda i: (i, 0)),
        out_shape=jax.ShapeDtypeStruct((t, LANES), F32),
        compiler_params=_params(("parallel",)),
        name="moe_router",
    )(x, w, b)


def _slots_kernel(route_ref, pos_ref, ends_ref, cnt_sc, off_sc, *, tile_rows):
    ph = pl.program_id(0)
    i = pl.program_id(1)
    route = route_ref[...]
    tm = route.shape[0]
    lane = _iota2((tm, LANES), 1)
    e1 = route[:, 0:1].astype(I32)
    e2 = route[:, 1:2].astype(I32)
    oh1 = lane == e1
    oh2 = lane == e2
    oh = (oh1 | oh2).astype(F32)
    colsum = jnp.sum(oh, axis=0, keepdims=True)

    @pl.when((ph == 0) & (i == 0))
    def _():
        cnt_sc[...] = jnp.zeros_like(cnt_sc)

    @pl.when(ph == 0)
    def _():
        cnt_sc[...] += colsum

    @pl.when((ph == 1) & (i == 0))
    def _():
        padded = jnp.ceil(cnt_sc[...] * (1.0 / tile_rows)) * tile_rows
        before = (_iota2((LANES, LANES), 0) < _iota2((LANES, LANES), 1)).astype(F32)
        starts = _mm_sel_r(padded, before)
        off_sc[...] = starts
        ends_ref[...] = (starts + padded).astype(I32)

    @pl.when(ph == 1)
    def _():
        earlier = (_iota2((tm, tm), 1) < _iota2((tm, tm), 0)).astype(F32)
        base = off_sc[...] + _mm(earlier, oh)
        p1 = jnp.sum(jnp.where(oh1, base, 0.0), axis=-1, keepdims=True)
        p2 = jnp.sum(jnp.where(oh2, base, 0.0), axis=-1, keepdims=True)
        pos_ref[...] = jnp.where(lane == 0, p1, jnp.where(lane == 1, p2, 0.0)).astype(I32)
        off_sc[...] += colsum


def _slots(route, tm, tile_rows):
    t = route.shape[0]
    return pl.pallas_call(
        functools.partial(_slots_kernel, tile_rows=tile_rows),
        grid=(2, t // tm),
        in_specs=[pl.BlockSpec((tm, LANES), lambda ph, i: (i, 0))],
        out_specs=[pl.BlockSpec((tm, LANES), lambda ph, i: (i * ph, 0)),
                   pl.BlockSpec((1, LANES), lambda ph, i: (0, 0))],
        out_shape=[jax.ShapeDtypeStruct((t, LANES), I32),
                   jax.ShapeDtypeStruct((1, LANES), I32)],
        scratch_shapes=[pltpu.VMEM((1, LANES), F32), pltpu.VMEM((1, LANES), F32)],
        compiler_params=_params(("arbitrary", "arbitrary")),
        name="moe_slots",
    )(route)


DMA_UNROLL = 8


def _scatter_kernel(pos_ref, x_ref, xs_in_ref, xs_ref, sem):
    del xs_in_ref
    i = pl.program_id(0)
    tm = x_ref.shape[0]

    def copy(t, c):
        p = pos_ref[2 * (i * tm + t) + c]
        return pltpu.make_async_copy(x_ref.at[pl.ds(t, 1)], xs_ref.at[pl.ds(p, 1)], sem)

    def start(t, carry):
        copy(t, 0).start(priority=0)
        copy(t, 1).start(priority=1)
        return carry

    def wait(t, carry):
        copy(t, 0).wait()
        copy(t, 1).wait()
        return carry

    lax.fori_loop(0, tm, start, 0, unroll=DMA_UNROLL)
    lax.fori_loop(0, tm, wait, 0, unroll=DMA_UNROLL)


def _scatter(pos_flat, x, n_slots, tm):
    t, d = x.shape
    xs0 = jnp.zeros((n_slots, d), F32)
    grid_spec = pltpu.PrefetchScalarGridSpec(
        num_scalar_prefetch=1,
        grid=(t // tm,),
        in_specs=[pl.BlockSpec((tm, d), lambda i, pos: (i, 0)), pl.BlockSpec(memory_space=pl.ANY)],
        out_specs=pl.BlockSpec(memory_space=pl.ANY),
        scratch_shapes=[pltpu.SemaphoreType.DMA(())],
    )
    return pl.pallas_call(
        _scatter_kernel,
        grid_spec=grid_spec,
        out_shape=jax.ShapeDtypeStruct((n_slots, d), F32),
        input_output_aliases={2: 0},
        compiler_params=_params(("arbitrary",)),
        name="moe_scatter",
    )(pos_flat, x, xs0)


def _experts_kernel(te_ref, nu_ref, xs_ref, wg_ref, wu_ref, wd_ref, out_ref, wg_sc, wu_sc, wd_sc):
    j = pl.program_id(0)
    used = j < nu_ref[0]
    prev = te_ref[jnp.maximum(j - 1, 0)]

    @pl.when(used & ((j == 0) | (te_ref[j] != prev)))
    def _():
        wg_sc[...] = wg_ref[...].astype(BF16)
        wu_sc[...] = wu_ref[...].astype(BF16)
        wd_sc[...] = wd_ref[...].astype(BF16)

    @pl.when(used)
    def _():
        xb = xs_ref[...].astype(BF16)
        h = _silu(jnp.dot(xb, wg_sc[...], preferred_element_type=F32))
        h = h * jnp.dot(xb, wu_sc[...], preferred_element_type=F32)
        out_ref[...] = jnp.dot(h.astype(BF16), wd_sc[...], preferred_element_type=F32)

    @pl.when(jnp.logical_not(used))
    def _():
        out_ref[...] = jnp.zeros_like(out_ref)


def _experts(tile_expert, n_used, xs, wg, wu, wd, tile_rows):
    n_slots, d = xs.shape
    f = wg.shape[-1]

    def rows(j, te, nu):
        return (jnp.minimum(j, nu[0] - 1), 0)

    def wsel(j, te, nu):
        return (te[jnp.minimum(j, nu[0] - 1)], 0, 0)

    grid_spec = pltpu.PrefetchScalarGridSpec(
        num_scalar_prefetch=2,
        grid=(n_slots // tile_rows,),
        in_specs=[pl.BlockSpec((tile_rows, d), rows),
                  pl.BlockSpec((None, d, f), wsel),
                  pl.BlockSpec((None, d, f), wsel),
                  pl.BlockSpec((None, f, d), wsel)],
        out_specs=pl.BlockSpec((tile_rows, d), lambda j, te, nu: (j, 0)),
        scratch_shapes=[pltpu.VMEM((d, f), BF16), pltpu.VMEM((d, f), BF16), pltpu.VMEM((f, d), BF16)],
    )
    return pl.pallas_call(
        _experts_kernel,
        grid_spec=grid_spec,
        out_shape=jax.ShapeDtypeStruct((n_slots, d), F32),
        compiler_params=_params(("arbitrary",)),
        name="moe_experts",
    )(tile_expert, n_used, xs, wg, wu, wd)


def _combine_kernel(pos_ref, x_ref, route_ref, lng_ref, lnb_ref, ys_ref, *rest, alpha, split_tiles):
    if split_tiles is None:
        out_ref, o_sc, sem = rest
    else:
        out_ref, out_s_ref, o_sc, sem = rest
    i = pl.program_id(0)
    tm = x_ref.shape[0]

    def copy(t, c):
        p = pos_ref[2 * (i * tm + t) + c]
        return pltpu.make_async_copy(ys_ref.at[pl.ds(p, 1)], o_sc.at[c, pl.ds(t, 1)], sem)

    def start(t, carry):
        copy(t, 0).start(priority=0)
        copy(t, 1).start(priority=1)
        return carry

    def wait(t, carry):
        copy(t, 0).wait()
        copy(t, 1).wait()
        return carry

    lax.fori_loop(0, tm, start, 0, unroll=DMA_UNROLL)
    lax.fori_loop(0, tm, wait, 0, unroll=DMA_UNROLL)
    route = route_ref[...]
    y = route[:, 2:3] * o_sc[0] + route[:, 3:4] * o_sc[1]
    res = _layer_norm(alpha * x_ref[...] + y, lng_ref[...], lnb_ref[...])
    if split_tiles is None:
        out_ref[...] = res
    else:
        @pl.when(i < split_tiles)
        def _():
            out_ref[...] = res

        @pl.when(i >= split_tiles)
        def _():
            out_s_ref[...] = res


def _combine(pos_flat, x, route, lng, lnb, ys, *, alpha, tm, split_rows=None):
    t, d = x.shape
    out_specs = pl.BlockSpec((tm, d), lambda i, pos: (i, 0))
    out_shape = jax.ShapeDtypeStruct((t, d), F32)
    npt = None
    if split_rows is not None:
        npt = split_rows // tm
        out_specs = [pl.BlockSpec((tm, d), lambda i, pos: (jnp.minimum(i, npt - 1), 0)),
                     pl.BlockSpec((tm, d), lambda i, pos: (jnp.maximum(i - npt, 0), 0))]
        out_shape = [jax.ShapeDtypeStruct((split_rows, d), F32),
                     jax.ShapeDtypeStruct((t - split_rows, d), F32)]
    grid_spec = pltpu.PrefetchScalarGridSpec(
        num_scalar_prefetch=1,
        grid=(t // tm,),
        in_specs=[pl.BlockSpec((tm, d), lambda i, pos: (i, 0)),
                  pl.BlockSpec((tm, LANES), lambda i, pos: (i, 0)),
                  pl.BlockSpec((1, d), lambda i, pos: (0, 0)),
                  pl.BlockSpec((1, d), lambda i, pos: (0, 0)),
                  pl.BlockSpec(memory_space=pl.ANY)],
        out_specs=out_specs,
        scratch_shapes=[pltpu.VMEM((2, tm, d), F32), pltpu.SemaphoreType.DMA(())],
    )
    return pl.pallas_call(
        functools.partial(_combine_kernel, alpha=alpha, split_tiles=npt),
        grid_spec=grid_spec,
        out_shape=out_shape,
        compiler_params=_params(("arbitrary",)),
        name="moe_combine",
    )(pos_flat, x, route, lng, lnb, ys)


def _hier_moe_ln(x, w_rg, b_rg, w_re, b_re, w_gate, w_up, w_down, lng, lnb, *, alpha, tm, tile_rows,
                 split_rows=None):
    t, d = x.shape
    ng, epg = EXPERT_GROUPS, EXPERTS_PER_GROUP
    n_exp = ng * epg
    pad = LANES - ng - n_exp
    w_router = jnp.pad(jnp.concatenate([w_rg, w_re], axis=1), ((0, 0), (0, pad)))
    b_router = jnp.pad(jnp.concatenate([b_rg, b_re]), (0, pad))[None, :]
    route = _router(x, w_router, b_router, tm)
    pos, ends = _slots(route, tm, tile_rows)
    pos_flat = pos[:, :2].reshape(-1)
    n_slots = 2 * t + n_exp * tile_rows
    n_tiles = n_slots // tile_rows
    seg_end = ends[0, :n_exp]
    tile_start = jnp.arange(n_tiles, dtype=I32) * tile_rows
    tile_expert = jnp.minimum(jnp.sum(tile_start[:, None] >= seg_end[None, :], axis=1), n_exp - 1).astype(I32)
    n_used = (seg_end[n_exp - 1:] // tile_rows).astype(I32)
    xs = _scatter(pos_flat, x, n_slots, tm)
    f = w_gate.shape[-1]
    ys = _experts(tile_expert, n_used, xs, w_gate.reshape(n_exp, d, f), w_up.reshape(n_exp, d, f),
                  w_down.reshape(n_exp, f, d), tile_rows)
    return _combine(pos_flat, x, route, lng, lnb, ys, alpha=alpha, tm=tm, split_rows=split_rows)


def _ssd_in_kernel(x_ref, wz_ref, wx_ref, wdt_ref, z_ref, xbc_ref, dt_ref):
    xb = x_ref[...].astype(BF16)
    z_ref[...] = jnp.dot(xb, wz_ref[...], preferred_element_type=F32)
    xbc_ref[...] = jnp.dot(xb, wx_ref[...], preferred_element_type=F32)
    dt_ref[...] = jnp.dot(xb, wdt_ref[...], preferred_element_type=F32)


def _ssd_in(x, wz, wx, wdt, tm):
    t, d = x.shape
    outs = [wz.shape[1], wx.shape[1], wdt.shape[1]]
    return pl.pallas_call(
        _ssd_in_kernel,
        grid=(t // tm,),
        in_specs=[pl.BlockSpec((tm, d), lambda i: (i, 0))]
                 + [pl.BlockSpec(w.shape, lambda i: (0, 0)) for w in (wz, wx, wdt)],
        out_specs=[pl.BlockSpec((tm, n), lambda i: (i, 0)) for n in outs],
        out_shape=[jax.ShapeDtypeStruct((t, n), F32) for n in outs],
        compiler_params=_params(("parallel",)),
        name="ssd_in",
    )(x, wz, wx, wdt)


def _ssd_kernel(xbc_ref, dt_ref, z_ref, conv0_ref, h0_ref, cw_ref, cb_ref, dtb_ref, alog_ref,
                dexp_ref, nw_ref, expand_ref, y_ref, convo_ref, hout_ref, pad_sc, ht_sc, *, chunk):
    c = chunk
    ci = pl.program_id(1)
    d_inner = z_ref.shape[1]
    gn = SSD_GROUPS * D_STATE
    per_group = d_inner // SSD_GROUPS
    tail = 8

    @pl.when(ci == 0)
    def _():
        pad_sc[0:tail, :] = conv0_ref[...]
        ht_sc[...] = h0_ref[...].T

    pad_sc[tail:tail + c, :] = xbc_ref[...]
    acc = cb_ref[...]
    for i in range(CONV_WIDTH):
        off = tail - (CONV_WIDTH - 1) + i
        acc = acc + pad_sc[off:off + c, :] * cw_ref[i:i + 1, :]
    xbc = _silu(acc)
    last_rows = pad_sc[c:c + tail, :]
    pad_sc[0:tail, :] = last_rows
    xs = xbc[:, :d_inner]

    dt = _softplus(dt_ref[...] + dtb_ref[...])
    da = dt * (-jnp.exp(alog_ref[...]))
    tril_incl = (_iota2((c, c), 1) <= _iota2((c, c), 0)).astype(F32)
    acum = _mm_sel_l(tril_incl, da)
    a_last = acum[c - 1:c, :]

    def to_cols(t):
        if c < LANES:
            t = jnp.concatenate([t, jnp.zeros((LANES - c, LANES), F32)], axis=0)
        return t.T[:, :c]

    acum_t = to_cols(acum)
    dt_t = to_cols(dt)
    expand = expand_ref[...]
    ea = _mm_sel_r(jnp.exp(acum), expand)
    dd = _mm_sel_r(jnp.exp(a_last - acum) * dt, expand)
    ea_last = _mm_sel_r(jnp.exp(a_last), expand)
    causal = _iota2((c, c), 1) <= _iota2((c, c), 0)
    lane = _iota2((1, LANES), 1)
    m0 = (lane < HEAD).astype(F32)
    m1 = 1.0 - m0

    ys = []
    for g in range(SSD_GROUPS):
        b_g = xbc[:, d_inner + g * D_STATE:d_inner + (g + 1) * D_STATE]
        c_g = xbc[:, d_inner + gn + g * D_STATE:d_inner + gn + (g + 1) * D_STATE]
        cb = _mm_nt(c_g, b_g)
        gsl = slice(g * per_group, (g + 1) * per_group)
        ht_g = ht_sc[:, gsl]
        y_off = _mm(c_g, ht_g) * ea[:, gsl]
        ht_sc[:, gsl] = ht_g * ea_last[:, gsl] + _mm_tn(b_g, xs[:, gsl] * dd[:, gsl])
        pieces = []
        for q in range(per_group // LANES):
            j0 = (g * per_group) // HEAD + 2 * q
            ms = []
            for j in (j0, j0 + 1):
                seg = acum[:, j:j + 1] - acum_t[j:j + 1, :]
                ms.append(jnp.where(causal, cb * jnp.exp(jnp.where(causal, seg, 0.0)) * dt_t[j:j + 1, :], 0.0))
            lo = g * per_group + q * LANES
            x_pair = xs[:, lo:lo + LANES]
            x_stack = jnp.concatenate([x_pair * m0, x_pair * m1], axis=0)
            pieces.append(_mm(jnp.concatenate(ms, axis=1), x_stack))
        y_g = jnp.concatenate(pieces, axis=1) + y_off + xs[:, gsl] * dexp_ref[:, gsl]
        y_g = y_g * _silu(z_ref[:, gsl])
        y_g = y_g * lax.rsqrt(jnp.mean(y_g * y_g, axis=-1, keepdims=True) + RMS_EPS)
        ys.append(y_g * nw_ref[:, gsl])
    y_ref[...] = jnp.concatenate(ys, axis=1)

    @pl.when(ci == pl.num_programs(1) - 1)
    def _():
        convo_ref[...] = last_rows
        hout_ref[...] = ht_sc[...].T


def _ssd(xbc, dt, z, conv0, h0, cw, cb, dtb, alog, dexp, nw, expand, *, batch, seq, row0, chunk):
    d_inner = z.shape[1]
    conv_dim = xbc.shape[1]
    nc = seq // chunk
    blk0 = row0 // chunk

    def tok(n):
        return pl.BlockSpec((chunk, n), lambda b, c: (blk0 + b * nc + c, 0))

    def par(arr):
        return pl.BlockSpec(arr.shape, lambda b, c: (0, 0))

    return pl.pallas_call(
        functools.partial(_ssd_kernel, chunk=chunk),
        grid=(batch, nc),
        in_specs=[tok(conv_dim), tok(LANES), tok(d_inner),
                  pl.BlockSpec((None, 8, conv_dim), lambda b, c: (b, 0, 0)),
                  pl.BlockSpec((None, d_inner, D_STATE), lambda b, c: (b, 0, 0)),
                  par(cw), par(cb), par(dtb), par(alog), par(dexp), par(nw), par(expand)],
        out_specs=[pl.BlockSpec((chunk, d_inner), lambda b, c: (b * nc + c, 0)),
                   pl.BlockSpec((None, 8, conv_dim), lambda b, c: (b, 0, 0)),
                   pl.BlockSpec((None, d_inner, D_STATE), lambda b, c: (b, 0, 0))],
        out_shape=[jax.ShapeDtypeStruct((batch * seq, d_inner), F32),
                   jax.ShapeDtypeStruct((batch, 8, conv_dim), F32),
                   jax.ShapeDtypeStruct((batch, d_inner, D_STATE), F32)],
        scratch_shapes=[pltpu.VMEM((chunk + 8, conv_dim), F32), pltpu.VMEM((D_STATE, d_inner), F32)],
        compiler_params=_params(("parallel", "arbitrary")),
        name="ssd_l%d" % chunk,
    )(xbc, dt, z, conv0, h0, cw, cb, dtb, alog, dexp, nw, expand)


def _rwkv_layer(xp3, xs3, shift_s, wkv_s, prm, lng, lnb, *, alpha, tm):
    (mu, w_rkv, w0, w1, w2, a0, a1, a2, g1, g2, k_k, k_a, r_k, gn_w, gn_b, w_o) = prm
    bp, sp, d = xp3.shape
    bs, ss, _ = xs3.shape
    tp = bp * sp
    heads = d // HEAD
    x_p = xp3.reshape(tp, d)
    x_s = xs3.reshape(bs * ss, d)
    prev_s = jnp.concatenate([shift_s[:, None, :], xs3[:, :-1]], axis=1).reshape(bs * ss, d)
    row = lambda v: v.reshape(1, d)
    r, k, v, lw, a, g = _rwkv_pre(x_p, x_s, prev_s, sp, mu, w_rkv.astype(BF16), row(w0), w1.astype(BF16),
                                  w2.astype(BF16), row(a0), a1.astype(BF16), a2.astype(BF16),
                                  g1.astype(BF16), g2.astype(BF16), tm)
    pv = (row(k_k), row(k_a), row(r_k), row(gn_w), row(gn_b))
    o_p, sp_out = _wkv(r, k, v, lw, a, *pv, None, batch=bp, seq=sp, row0=0,
                       chunk=min(WKV_CHUNK, sp), pairs_per_step=min(WKV_PAIRS_PER_STEP, heads // 2))
    o_s, ss_out = _wkv(r, k, v, lw, a, *pv, wkv_s, batch=bs, seq=ss, row0=tp,
                       chunk=min(WKV_CHUNK, ss), pairs_per_step=heads // 2)
    x1 = _proj_ln(o_p, o_s, g, (x_p, x_s, 0), w_o.astype(BF16), lng, lnb, alpha=alpha, tm=tm)
    return x1, sp_out, ss_out


def _ssd_layer(x, bp, sp, bs, ss, conv_s, ssm_s, prm, lng, lnb, *, alpha, tm):
    (w_in, conv_w, conv_b, dt_bias, a_log, d_skip, norm_w, w_out) = prm
    t, d = x.shape
    tp = bp * sp
    heads = a_log.shape[0]
    d_inner = heads * HEAD
    conv_dim = conv_w.shape[1]
    wz = w_in[:, :d_inner].astype(BF16)
    wx = w_in[:, d_inner:d_inner + conv_dim].astype(BF16)
    wdt = jnp.pad(w_in[:, d_inner + conv_dim:], ((0, 0), (0, LANES - heads))).astype(BF16)
    z, xbc, dt = _ssd_in(x, wz, wx, wdt, tm)
    lane_pad = lambda v: jnp.pad(v, (0, LANES - heads))[None, :]
    expand = (jnp.arange(LANES)[:, None] == (jnp.arange(d_inner) // HEAD)[None, :]).astype(F32)
    common = (conv_w, conv_b[None, :], lane_pad(dt_bias), lane_pad(a_log),
              jnp.repeat(d_skip, HEAD)[None, :], norm_w[None, :], expand)
    conv_tail = CONV_WIDTH - 1
    pad_conv = lambda cs: jnp.pad(cs, ((0, 0), (8 - conv_tail, 0), (0, 0)))
    y_p, conv_p, h_p = _ssd(xbc, dt, z, jnp.zeros((bp, 8, conv_dim), F32),
                            jnp.zeros((bp, d_inner, D_STATE), F32), *common,
                            batch=bp, seq=sp, row0=0, chunk=SSD_CHUNK if sp % SSD_CHUNK == 0 else sp)
    y_s, conv_s_out, h_s = _ssd(xbc, dt, z, pad_conv(conv_s), ssm_s.reshape(bs, d_inner, D_STATE), *common,
                                batch=bs, seq=ss, row0=tp, chunk=SSD_CHUNK if ss % SSD_CHUNK == 0 else ss)
    x1 = _proj_ln(y_p, y_s, None, (x, x, tp // tm), w_out.astype(BF16), lng, lnb, alpha=alpha, tm=tm)
    shape_h = lambda h, b: h.reshape(b, heads, HEAD, D_STATE)
    return (x1, conv_p[:, 8 - conv_tail:], conv_s_out[:, 8 - conv_tail:], shape_h(h_p, bp), shape_h(h_s, bs))


def kernel(x_prompt, x_sample, state_rwkv_wkv, state_rwkv_shift, state_ssd_ssm, state_ssd_conv,
           rwkv_mu, rwkv_w_rkv, rwkv_w0, rwkv_w1, rwkv_w2, rwkv_a0, rwkv_a1, rwkv_a2,
           rwkv_g1, rwkv_g2, rwkv_k_k, rwkv_k_a, rwkv_r_k, rwkv_gn_w, rwkv_gn_b, rwkv_w_o,
           ssd_w_in, ssd_conv_w, ssd_conv_b, ssd_dt_bias, ssd_a_log, ssd_d, ssd_norm_w, ssd_w_out,
           ln_gain, ln_bias, moe_w_rg, moe_b_rg, moe_w_re, moe_b_re, moe_w_gate, moe_w_up, moe_w_down):
    bp, sp, d = x_prompt.shape
    bs, ss, _ = x_sample.shape
    depth = ln_gain.shape[0]
    alpha = (2 * depth) ** 0.25
    tp, ts = bp * sp, bs * ss
    tm = 256
    tile_rows = 256
    x = None
    xp3, xs3 = x_prompt, x_sample
    wkv_p, wkv_s, sh_p, sh_s, ssm_p, ssm_s, cv_p, cv_s = [], [], [], [], [], [], [], []
    for i in range(depth):
        j = i // 2
        lng = lambda n: ln_gain[i, n][None, :]
        lnb = lambda n: ln_bias[i, n][None, :]
        if i % 2 == 0:
            if x is not None:
                xp3, xs3 = x[:tp].reshape(bp, sp, d), x[tp:].reshape(bs, ss, d)
            prm = (rwkv_mu[j], rwkv_w_rkv[j], rwkv_w0[j], rwkv_w1[j], rwkv_w2[j], rwkv_a0[j],
                   rwkv_a1[j], rwkv_a2[j], rwkv_g1[j], rwkv_g2[j], rwkv_k_k[j], rwkv_k_a[j],
                   rwkv_r_k[j].reshape(-1), rwkv_gn_w[j], rwkv_gn_b[j], rwkv_w_o[j])
            sh_p.append(xp3[:, -1])
            sh_s.append(xs3[:, -1])
            x, w_p, w_s = _rwkv_layer(xp3, xs3, state_rwkv_shift[j], state_rwkv_wkv[j], prm,
                                      lng(0), lnb(0), alpha=alpha, tm=tm)
            wkv_p.append(w_p)
            wkv_s.append(w_s)
        else:
            prm = (ssd_w_in[j], ssd_conv_w[j], ssd_conv_b[j], ssd_dt_bias[j], ssd_a_log[j],
                   ssd_d[j], ssd_norm_w[j], ssd_w_out[j])
            x, c_p, c_s, h_p, h_s = _ssd_layer(x, bp, sp, bs, ss, state_ssd_conv[j], state_ssd_ssm[j],
                                               prm, lng(0), lnb(0), alpha=alpha, tm=tm)
            cv_p.append(c_p)
            cv_s.append(c_s)
            ssm_p.append(h_p)
            ssm_s.append(h_s)
        x = _hier_moe_ln(x, moe_w_rg[i], moe_b_rg[i], moe_w_re[i], moe_b_re[i], moe_w_gate[i],
                         moe_w_up[i], moe_w_down[i], lng(1), lnb(1), alpha=alpha, tm=tm,
                         tile_rows=tile_rows, split_rows=tp if i == depth - 1 else None)
    yp = x[0].reshape(bp, sp, d)
    ys = x[1].reshape(bs, ss, d)
    return (yp, ys, jnp.stack(wkv_p), jnp.stack(wkv_s), jnp.stack(sh_p), jnp.stack(sh_s),
            jnp.stack(ssm_p), jnp.stack(ssm_s), jnp.stack(cv_p), jnp.stack(cv_s))
```

```python
import functools
import math

import jax
import jax.numpy as jnp
from jax import lax
from jax.experimental import pallas as pl
from jax.experimental.pallas import tpu as pltpu

F32 = jnp.float32
BF16 = jnp.bfloat16
I32 = jnp.int32

LANES = 128
HEAD = 64
GN_EPS = HEAD * 1e-5
RMS_EPS = 1e-5
LN_EPS = 1e-5
WKV_CHUNK = 64
WKV_PAIRS_PER_STEP = 8
SSD_CHUNK = 128
SSD_GROUPS = 4
D_STATE = 128
CONV_WIDTH = 4
EXPERT_GROUPS = 4
EXPERTS_PER_GROUP = 8
VMEM_LIMIT = 56 * 1024 * 1024

_HI = lax.Precision.HIGHEST


def _params(sem):
    return pltpu.CompilerParams(dimension_semantics=sem, vmem_limit_bytes=VMEM_LIMIT)


def _mm(a, b):
    return jnp.dot(a.astype(BF16), b.astype(BF16), preferred_element_type=F32)


def _mm_nt(a, b):
    return lax.dot_general(a.astype(BF16), b.astype(BF16), (((1,), (1,)), ((), ())),
                           preferred_element_type=F32)


def _mm_tn(a, b):
    return lax.dot_general(a.astype(BF16), b.astype(BF16), (((0,), (0,)), ((), ())),
                           preferred_element_type=F32)


def _mm_hi(a, b):
    return jnp.dot(a, b, precision=_HI, preferred_element_type=F32)


def _split3(x):
    hi = x.astype(BF16)
    r1 = x - hi.astype(F32)
    mid = r1.astype(BF16)
    lo = (r1 - mid.astype(F32)).astype(BF16)
    return hi, mid, lo


def _mm_sel_l(sel, x):
    k = x.shape[1]
    parts = jnp.concatenate([p.astype(F32) for p in _split3(x)], axis=1).astype(BF16)
    y = jnp.dot(sel.astype(BF16), parts, preferred_element_type=F32)
    return y[:, :k] + y[:, k:2 * k] + y[:, 2 * k:]


def _mm_sel_r(x, sel):
    n = x.shape[0]
    if n % 8:
        return sum(jnp.dot(p, sel.astype(BF16), preferred_element_type=F32) for p in _split3(x))
    parts = jnp.concatenate([p.astype(F32) for p in _split3(x)], axis=0).astype(BF16)
    y = jnp.dot(parts, sel.astype(BF16), preferred_element_type=F32)
    return y[:n] + y[n:2 * n] + y[2 * n:]


def _sigmoid(x):
    return 1.0 / (1.0 + jnp.exp(-x))


def _silu(x):
    return x * _sigmoid(x)


def _softplus(x):
    return jnp.maximum(x, 0.0) + jnp.log(1.0 + jnp.exp(-jnp.abs(x)))


def _layer_norm(x, g, b):
    mu = jnp.mean(x, axis=-1, keepdims=True)
    xc = x - mu
    var = jnp.mean(xc * xc, axis=-1, keepdims=True)
    return xc * lax.rsqrt(var + LN_EPS) * g + b


def _iota2(shape, dim):
    return lax.broadcasted_iota(I32, shape, dim)


def _part_specs(tm, width, prompt_tiles, sample_block0=0):
    return [pl.BlockSpec((tm, width), lambda i: (jnp.minimum(i, prompt_tiles - 1), 0)),
            pl.BlockSpec((tm, width), lambda i: (sample_block0 + jnp.maximum(i - prompt_tiles, 0), 0))]


def _rwkv_pre_kernel(xa_ref, xb_ref, tail_ref, xprev_s_ref, mu_ref, wrkv_ref, w0_ref, w1_ref, w2_ref,
                     a0_ref, a1_ref, a2_ref, g1_ref, g2_ref, r_ref, k_ref, v_ref, lw_ref, a_ref, g_ref,
                     *, prompt_tiles, seq):
    i = pl.program_id(0)
    tm = xa_ref.shape[0]
    is_prompt = i < prompt_tiles
    xa = xa_ref[...]
    x = jnp.where(is_prompt, xa, xb_ref[...])
    first = jnp.where((i * tm) % seq == 0, 0.0, tail_ref[7:8, :])
    shifted = jnp.where(_iota2(xa.shape, 0) == 0, first, pltpu.roll(xa, 1, axis=0))
    xx = jnp.where(is_prompt, shifted, xprev_s_ref[...]) - x

    def mix(j):
        return x + xx * mu_ref[j:j + 1, :]

    r_ref[...] = _mm(mix(0), wrkv_ref[0])
    k_ref[...] = _mm(mix(1), wrkv_ref[1])
    v_ref[...] = _mm(mix(2), wrkv_ref[2])
    ww = w0_ref[...] + _mm(jnp.tanh(_mm(mix(3), w1_ref[...])), w2_ref[...])
    w_raw = -_softplus(-ww) - 0.5
    lw_ref[...] = -jnp.exp(w_raw)
    a_ref[...] = _sigmoid(a0_ref[...] + _mm(_mm(mix(4), a1_ref[...]), a2_ref[...]))
    g_ref[...] = _mm(_sigmoid(_mm(mix(5), g1_ref[...])), g2_ref[...])


def _rwkv_pre(x_p, x_s, xprev_s, seq, mu, wrkv, w0, w1, w2, a0, a1, a2, g1, g2, tm):
    (tp, d), ts = x_p.shape, x_s.shape[0]
    assert seq % tm == 0 and ts % tm == 0 and tm % 8 == 0
    npt = tp // tm
    row = pl.BlockSpec((tm, d), lambda i: (i, 0))
    tail = pl.BlockSpec((8, d), lambda i: (jnp.maximum(jnp.minimum(i, npt - 1) * (tm // 8) - 1, 0), 0))

    def full(arr):
        nd = arr.ndim
        return pl.BlockSpec(arr.shape, lambda i, _n=nd: (0,) * _n)

    weights = (mu, wrkv, w0, w1, w2, a0, a1, a2, g1, g2)
    return pl.pallas_call(
        functools.partial(_rwkv_pre_kernel, prompt_tiles=npt, seq=seq),
        grid=((tp + ts) // tm,),
        in_specs=_part_specs(tm, d, npt) + [tail, _part_specs(tm, d, npt)[1]] + [full(w) for w in weights],
        out_specs=[row] * 6,
        out_shape=[jax.ShapeDtypeStruct((tp + ts, d), F32)] * 6,
        compiler_params=_params(("parallel",)),
        name="rwkv_pre",
    )(x_p, x_s, x_p, xprev_s, *weights)


def _wkv_kernel(r_ref, k_ref, v_ref, lw_ref, a_ref, kkw_ref, kaw_ref, rk_ref, gnw_ref, gnb_ref,
                *rest, chunk, pairs_per_step, has_init):
    if has_init:
        s0_ref, o_ref, sout_ref, s_sc = rest
    else:
        o_ref, sout_ref, s_sc = rest
    L = chunk
    c = pl.program_id(2)

    @pl.when(c == 0)
    def _():
        if has_init:
            zero = jnp.zeros((HEAD, HEAD), F32)
            for q in range(pairs_per_step):
                top = jnp.concatenate([s0_ref[2 * q], zero], axis=1)
                bot = jnp.concatenate([zero, s0_ref[2 * q + 1]], axis=1)
                s_sc[q] = jnp.concatenate([top, bot], axis=0)
        else:
            s_sc[...] = jnp.zeros_like(s_sc)

    lane = _iota2((1, LANES), 1)
    m0 = (lane < HEAD).astype(F32)
    m1 = 1.0 - m0
    same_head = (_iota2((LANES, LANES), 0) // HEAD == _iota2((LANES, LANES), 1) // HEAD).astype(F32)
    tril_incl = (_iota2((L, L), 1) <= _iota2((L, L), 0)).astype(F32)
    n2 = 2 * L
    row = _iota2((n2, n2), 0)
    col = _iota2((n2, n2), 1)
    strict = col < row
    incl = col <= row
    steps = int(math.log2(L))

    def headsum(t):
        return _mm_sel_r(t, same_head)

    def stack(t):
        return jnp.concatenate([t * m0, t * m1], axis=0)

    pairs = range(pairs_per_step)
    lanes = [slice(q * LANES, (q + 1) * LANES) for q in pairs]

    def prepare(ls):
        r = r_ref[:, ls]
        k = k_ref[:, ls]
        v = v_ref[:, ls]
        lw = lw_ref[:, ls]
        a = a_ref[:, ls]
        kk = k * kkw_ref[:, ls]
        k2 = k * (1.0 + (a - 1.0) * kaw_ref[:, ls])
        sums = headsum(jnp.concatenate([kk * kk, r * k2 * rk_ref[:, ls]], axis=0))
        kk = kk / jnp.maximum(jnp.sqrt(sums[:L]), 1e-12)
        bonus = sums[L:] * v
        bvec = kk * a
        cl = _mm_sel_l(tril_incl, lw)
        total = cl[L - 1:L, :]
        w_in = jnp.exp(cl)
        w_inv = jnp.exp(-cl)
        w_end = jnp.exp(total - cl)
        return dict(
            ar=jnp.concatenate([stack(-kk * jnp.exp(cl - lw)), stack(r * w_in)], axis=0).astype(BF16),
            bk=jnp.concatenate([stack(bvec * w_inv), stack(k2 * w_inv)], axis=0).astype(BF16),
            bk_end=jnp.concatenate([stack(bvec * w_end), stack(k2 * w_end)], axis=0),
            v_s=stack(v), total=total, bonus=bonus)

    ops = [prepare(ls) for ls in lanes]
    states = [s_sc[q] for q in pairs]
    cross = [_mm_nt(o["ar"], o["bk"]) for o in ops]
    from_s = [_mm_nt(o["ar"], s) for o, s in zip(ops, states)]
    n_ab = [jnp.where(strict, x[:n2, :n2], 0.0) for x in cross]
    a_rb = [jnp.where(incl, x[n2:, :n2], 0.0) for x in cross]
    to_v = [jnp.concatenate([jnp.where(strict, x[:n2, n2:], 0.0), jnp.where(incl, x[n2:, n2:], 0.0)], axis=0)
            for x in cross]
    from_v = [_mm(m, o["v_s"]) for m, o in zip(to_v, ops)]

    u = [fs[:n2] + fv[:n2] for fs, fv in zip(from_s, from_v)]
    p = n_ab
    for j in range(steps):
        if j + 1 < steps:
            both = [_mm(pi, jnp.concatenate([ui, pi], axis=1)) for ui, pi in zip(u, p)]
            u = [ui + b[:, :LANES] for ui, b in zip(u, both)]
            p = [b[:, LANES:] for b in both]
        else:
            u = [ui + _mm(pi, ui) for ui, pi in zip(u, p)]

    y_s = [fs[n2:] + fv[n2:] + _mm(mb, ui) for fs, fv, mb, ui in zip(from_s, from_v, a_rb, u)]
    s_new = [s * jnp.exp(o["total"]) + _mm_tn(jnp.concatenate([ui, o["v_s"]], axis=0), o["bk_end"])
             for o, s, ui in zip(ops, states, u)]
    for q in pairs:
        s_sc[q] = s_new[q]
        y = y_s[q][:L] + y_s[q][L:]
        ym = headsum(y) * (1.0 / HEAD)
        yc = y - ym
        yv = headsum(yc * yc) * (1.0 / HEAD)
        ls = lanes[q]
        o_ref[:, ls] = yc * lax.rsqrt(yv + GN_EPS) * gnw_ref[:, ls] + gnb_ref[:, ls] + ops[q]["bonus"]

    @pl.when(c == pl.num_programs(2) - 1)
    def _():
        for q in pairs:
            sout_ref[2 * q] = s_new[q][:HEAD, :HEAD]
            sout_ref[2 * q + 1] = s_new[q][HEAD:, HEAD:]


def _wkv(r, k, v, lw, a, kkw, kaw, rk, gnw, gnb, s0, *, batch, seq, row0, chunk, pairs_per_step):
    d = r.shape[1]
    pps = pairs_per_step
    width = pps * LANES
    nc = seq // chunk
    blk0 = row0 // chunk
    tok = pl.BlockSpec((chunk, width), lambda b, p, c: (blk0 + b * nc + c, p))
    out = pl.BlockSpec((chunk, width), lambda b, p, c: (b * nc + c, p))
    par = pl.BlockSpec((1, width), lambda b, p, c: (0, p))
    st = pl.BlockSpec((None, 2 * pps, HEAD, HEAD), lambda b, p, c: (b, p, 0, 0))
    has_init = s0 is not None
    return pl.pallas_call(
        functools.partial(_wkv_kernel, chunk=chunk, pairs_per_step=pps, has_init=has_init),
        grid=(batch, d // width, nc),
        in_specs=[tok] * 5 + [par] * 5 + ([st] if has_init else []),
        out_specs=[out, st],
        out_shape=[jax.ShapeDtypeStruct((batch * seq, d), F32),
                   jax.ShapeDtypeStruct((batch, d // HEAD, HEAD, HEAD), F32)],
        scratch_shapes=[pltpu.VMEM((pps, LANES, LANES), F32)],
        compiler_params=_params(("parallel", "parallel", "arbitrary")),
        name="wkv_l%d" % chunk,
    )(r, k, v, lw, a, kkw, kaw, rk, gnw, gnb, *([s0] if has_init else []))


def _proj_ln_kernel(op_ref, os_ref, g_ref, xa_ref, xb_ref, w_ref, lng_ref, lnb_ref, out_ref, *, alpha,
                    gated, prompt_tiles):
    is_prompt = pl.program_id(0) < prompt_tiles
    o = jnp.where(is_prompt, op_ref[...], os_ref[...])
    if gated:
        o = o * g_ref[...]
    y = _mm(o, w_ref[...])
    x = jnp.where(is_prompt, xa_ref[...], xb_ref[...])
    out_ref[...] = _layer_norm(alpha * x + y, lng_ref[...], lnb_ref[...])


def _proj_ln(o_p, o_s, g, x_parts, w, lng, lnb, *, alpha, tm):
    x_a, x_b, sample_block0 = x_parts
    d = x_a.shape[1]
    kin = o_p.shape[1]
    npt = o_p.shape[0] // tm
    t = o_p.shape[0] + o_s.shape[0]
    gated = g is not None
    g_spec = pl.BlockSpec((tm, kin), lambda i: (i, 0))
    if not gated:
        g, g_spec = lng, pl.BlockSpec((1, d), lambda i: (0, 0))
    return pl.pallas_call(
        functools.partial(_proj_ln_kernel, alpha=alpha, gated=gated, prompt_tiles=npt),
        grid=(t // tm,),
        in_specs=_part_specs(tm, kin, npt) + [g_spec] + _part_specs(tm, d, npt, sample_block0)
                 + [pl.BlockSpec(w.shape, lambda i: (0, 0)),
                    pl.BlockSpec((1, d), lambda i: (0, 0)),
                    pl.BlockSpec((1, d), lambda i: (0, 0))],
        out_specs=pl.BlockSpec((tm, d), lambda i: (i, 0)),
        out_shape=jax.ShapeDtypeStruct((t, d), F32),
        compiler_params=_params(("parallel",)),
        name="proj_ln",
    )(o_p, o_s, g, x_a, x_b, w, lng, lnb)


def _router_kernel(x_ref, w_ref, b_ref, route_ref):
    ng, epg = EXPERT_GROUPS, EXPERTS_PER_GROUP
    logits = _mm_hi(x_ref[...], w_ref[...]) + b_ref[...]
    lane = _iota2(logits.shape, 1)
    neg = jnp.float32(-jnp.inf)
    big = jnp.int32(1 << 20)

    is_g = lane < ng
    lg = jnp.where(is_g, logits, neg)
    mg = jnp.max(lg, axis=-1, keepdims=True)
    g_sel = jnp.min(jnp.where(lg == mg, lane, big), axis=-1, keepdims=True)
    zg = jnp.sum(jnp.where(is_g, jnp.exp(lg - mg), 0.0), axis=-1, keepdims=True)
    p_group = 1.0 / zg

    e_lane = lane - ng
    in_sel = (e_lane >= g_sel * epg) & (e_lane < (g_sel + 1) * epg)
    le = jnp.where(in_sel, logits, neg)
    m1 = jnp.max(le, axis=-1, keepdims=True)
    i1 = jnp.min(jnp.where(le == m1, lane, big), axis=-1, keepdims=True)
    ze = jnp.sum(jnp.where(in_sel, jnp.exp(le - m1), 0.0), axis=-1, keepdims=True)
    le2 = jnp.where(lane == i1, neg, le)
    m2 = jnp.max(le2, axis=-1, keepdims=True)
    i2 = jnp.min(jnp.where(le2 == m2, lane, big), axis=-1, keepdims=True)
    pe1 = 1.0 / ze
    pe2 = jnp.exp(m2 - m1) / ze
    den = pe1 + pe2
    c1 = pe1 / den * p_group
    c2 = pe2 / den * p_group
    route_ref[...] = jnp.where(lane == 0, (i1 - ng).astype(F32),
                     jnp.where(lane == 1, (i2 - ng).astype(F32),
                     jnp.where(lane == 2, c1, jnp.where(lane == 3, c2, 0.0))))


def _router(x, w, b, tm):
    t, d = x.shape
    return pl.pallas_call(
        _router_kernel,
        grid=(t // tm,),
        in_specs=[pl.BlockSpec((tm, d), lambda i: (i, 0)),
                  pl.BlockSpec((d, LANES), lambda i: (0, 0)),
                  pl.BlockSpec((1, LANES), lambda i: (0, 0))],
        out_specs=pl.BlockSpec((tm, LANES), lambda i: (i, 0)),
        out_shape=jax.ShapeDtypeStruct((t, LANES), F32),
        compiler_params=_params(("parallel",)),
        name="moe_router",
    )(x, w, b)


def _slots_kernel(route_ref, pos_ref, ends_ref, cnt_sc, off_sc, *, tile_rows):
    ph = pl.program_id(0)
    i = pl.program_id(1)
    route = route_ref[...]
    tm = route.shape[0]
    lane = _iota2((tm, LANES), 1)
    e1 = route[:, 0:1].astype(I32)
    e2 = route[:, 1:2].astype(I32)
    oh1 = lane == e1
    oh2 = lane == e2
    oh = (oh1 | oh2).astype(F32)
    colsum = jnp.sum(oh, axis=0, keepdims=True)

    @pl.when((ph == 0) & (i == 0))
    def _():
        cnt_sc[...] = jnp.zeros_like(cnt_sc)

    @pl.when(ph == 0)
    def _():
        cnt_sc[...] += colsum

    @pl.when((ph == 1) & (i == 0))
    def _():
        padded = jnp.ceil(cnt_sc[...] * (1.0 / tile_rows)) * tile_rows
        before = (_iota2((LANES, LANES), 0) < _iota2((LANES, LANES), 1)).astype(F32)
        starts = _mm_sel_r(padded, before)
        off_sc[...] = starts
        ends_ref[...] = (starts + padded).astype(I32)

    @pl.when(ph == 1)
    def _():
        earlier = (_iota2((tm, tm), 1) < _iota2((tm, tm), 0)).astype(F32)
        base = off_sc[...] + _mm(earlier, oh)
        p1 = jnp.sum(jnp.where(oh1, base, 0.0), axis=-1, keepdims=True)
        p2 = jnp.sum(jnp.where(oh2, base, 0.0), axis=-1, keepdims=True)
        pos_ref[...] = jnp.where(lane == 0, p1, jnp.where(lane == 1, p2, 0.0)).astype(I32)
        off_sc[...] += colsum


def _slots(route, tm, tile_rows):
    t = route.shape[0]
    return pl.pallas_call(
        functools.partial(_slots_kernel, tile_rows=tile_rows),
        grid=(2, t // tm),
        in_specs=[pl.BlockSpec((tm, LANES), lambda ph, i: (i, 0))],
        out_specs=[pl.BlockSpec((tm, LANES), lambda ph, i: (i * ph, 0)),
                   pl.BlockSpec((1, LANES), lambda ph, i: (0, 0))],
        out_shape=[jax.ShapeDtypeStruct((t, LANES), I32),
                   jax.ShapeDtypeStruct((1, LANES), I32)],
        scratch_shapes=[pltpu.VMEM((1, LANES), F32), pltpu.VMEM((1, LANES), F32)],
        compiler_params=_params(("arbitrary", "arbitrary")),
        name="moe_slots",
    )(route)


DMA_UNROLL = 8


def _scatter_kernel(pos_ref, x_ref, xs_in_ref, xs_ref, sem):
    del xs_in_ref
    i = pl.program_id(0)
    tm = x_ref.shape[0]

    def copy(t, c):
        p = pos_ref[2 * (i * tm + t) + c]
        return pltpu.make_async_copy(x_ref.at[pl.ds(t, 1)], xs_ref.at[pl.ds(p, 1)], sem)

    def start(t, carry):
        copy(t, 0).start(priority=0)
        copy(t, 1).start(priority=1)
        return carry

    def wait(t, carry):
        copy(t, 0).wait()
        copy(t, 1).wait()
        return carry

    lax.fori_loop(0, tm, start, 0, unroll=DMA_UNROLL)
    lax.fori_loop(0, tm, wait, 0, unroll=DMA_UNROLL)


def _scatter(pos_flat, x, n_slots, tm):
    t, d = x.shape
    xs0 = jnp.zeros((n_slots, d), F32)
    grid_spec = pltpu.PrefetchScalarGridSpec(
        num_scalar_prefetch=1,
        grid=(t // tm,),
        in_specs=[pl.BlockSpec((tm, d), lambda i, pos: (i, 0)), pl.BlockSpec(memory_space=pl.ANY)],
        out_specs=pl.BlockSpec(memory_space=pl.ANY),
        scratch_shapes=[pltpu.SemaphoreType.DMA(())],
    )
    return pl.pallas_call(
        _scatter_kernel,
        grid_spec=grid_spec,
        out_shape=jax.ShapeDtypeStruct((n_slots, d), F32),
        input_output_aliases={2: 0},
        compiler_params=_params(("arbitrary",)),
        name="moe_scatter",
    )(pos_flat, x, xs0)


def _experts_kernel(te_ref, nu_ref, xs_ref, wg_ref, wu_ref, wd_ref, out_ref, wg_sc, wu_sc, wd_sc):
    j = pl.program_id(0)
    used = j < nu_ref[0]
    prev = te_ref[jnp.maximum(j - 1, 0)]

    @pl.when(used & ((j == 0) | (te_ref[j] != prev)))
    def _():
        wg_sc[...] = wg_ref[...].astype(BF16)
        wu_sc[...] = wu_ref[...].astype(BF16)
        wd_sc[...] = wd_ref[...].astype(BF16)

    @pl.when(used)
    def _():
        xb = xs_ref[...].astype(BF16)
        h = _silu(jnp.dot(xb, wg_sc[...], preferred_element_type=F32))
        h = h * jnp.dot(xb, wu_sc[...], preferred_element_type=F32)
        out_ref[...] = jnp.dot(h.astype(BF16), wd_sc[...], preferred_element_type=F32)

    @pl.when(jnp.logical_not(used))
    def _():
        out_ref[...] = jnp.zeros_like(out_ref)


def _experts(tile_expert, n_used, xs, wg, wu, wd, tile_rows, expert0):
    n_slots, d = xs.shape
    f = wg.shape[-1]

    def last_used(j, nu):
        return jnp.maximum(jnp.minimum(j, nu[0] - 1), 0)

    def rows(j, te, nu):
        return (last_used(j, nu), 0)

    def wsel(j, te, nu):
        return (expert0 + te[last_used(j, nu)], 0, 0)

    grid_spec = pltpu.PrefetchScalarGridSpec(
        num_scalar_prefetch=2,
        grid=(n_slots // tile_rows,),
        in_specs=[pl.BlockSpec((tile_rows, d), rows),
                  pl.BlockSpec((None, d, f), wsel),
                  pl.BlockSpec((None, d, f), wsel),
                  pl.BlockSpec((None, f, d), wsel)],
        out_specs=pl.BlockSpec((tile_rows, d), lambda j, te, nu: (j, 0)),
        scratch_shapes=[pltpu.VMEM((d, f), BF16), pltpu.VMEM((d, f), BF16), pltpu.VMEM((f, d), BF16)],
    )
    return pl.pallas_call(
        _experts_kernel,
        grid_spec=grid_spec,
        out_shape=jax.ShapeDtypeStruct((n_slots, d), F32),
        compiler_params=_params(("arbitrary",)),
        name="moe_experts",
    )(tile_expert, n_used, xs, wg, wu, wd)


def _combine_kernel(pos_ref, x_ref, route_ref, lng_ref, lnb_ref, ys_ref, *rest, alpha, split_tiles):
    if split_tiles is None:
        out_ref, o_sc, sem = rest
    else:
        out_ref, out_s_ref, o_sc, sem = rest
    i = pl.program_id(0)
    tm = x_ref.shape[0]

    def copy(t, c):
        p = pos_ref[2 * (i * tm + t) + c]
        return pltpu.make_async_copy(ys_ref.at[pl.ds(p, 1)], o_sc.at[c, pl.ds(t, 1)], sem)

    def start(t, carry):
        copy(t, 0).start(priority=0)
        copy(t, 1).start(priority=1)
        return carry

    def wait(t, carry):
        copy(t, 0).wait()
        copy(t, 1).wait()
        return carry

    lax.fori_loop(0, tm, start, 0, unroll=DMA_UNROLL)
    lax.fori_loop(0, tm, wait, 0, unroll=DMA_UNROLL)
    route = route_ref[...]
    y = route[:, 2:3] * o_sc[0] + route[:, 3:4] * o_sc[1]
    res = _layer_norm(alpha * x_ref[...] + y, lng_ref[...], lnb_ref[...])
    if split_tiles is None:
        out_ref[...] = res
    else:
        @pl.when(i < split_tiles)
        def _():
            out_ref[...] = res

        @pl.when(i >= split_tiles)
        def _():
            out_s_ref[...] = res


def _combine(pos_flat, x, route, lng, lnb, ys, *, alpha, tm, split_rows=None):
    t, d = x.shape
    out_specs = pl.BlockSpec((tm, d), lambda i, pos: (i, 0))
    out_shape = jax.ShapeDtypeStruct((t, d), F32)
    npt = None
    if split_rows is not None:
        npt = split_rows // tm
        out_specs = [pl.BlockSpec((tm, d), lambda i, pos: (jnp.minimum(i, npt - 1), 0)),
                     pl.BlockSpec((tm, d), lambda i, pos: (jnp.maximum(i - npt, 0), 0))]
        out_shape = [jax.ShapeDtypeStruct((split_rows, d), F32),
                     jax.ShapeDtypeStruct((t - split_rows, d), F32)]
    grid_spec = pltpu.PrefetchScalarGridSpec(
        num_scalar_prefetch=1,
        grid=(t // tm,),
        in_specs=[pl.BlockSpec((tm, d), lambda i, pos: (i, 0)),
                  pl.BlockSpec((tm, LANES), lambda i, pos: (i, 0)),
                  pl.BlockSpec((1, d), lambda i, pos: (0, 0)),
                  pl.BlockSpec((1, d), lambda i, pos: (0, 0)),
                  pl.BlockSpec(memory_space=pl.ANY)],
        out_specs=out_specs,
        scratch_shapes=[pltpu.VMEM((2, tm, d), F32), pltpu.SemaphoreType.DMA(())],
    )
    return pl.pallas_call(
        functools.partial(_combine_kernel, alpha=alpha, split_tiles=npt),
        grid_spec=grid_spec,
        out_shape=out_shape,
        compiler_params=_params(("arbitrary",)),
        name="moe_combine",
    )(pos_flat, x, route, lng, lnb, ys)


def _hier_moe_ln(x, w_rg, b_rg, w_re, b_re, w_gate, w_up, w_down, lng, lnb, *, alpha, tm, tile_rows,
                 layer=0, split_rows=None):
    t, d = x.shape
    ng, epg = EXPERT_GROUPS, EXPERTS_PER_GROUP
    n_exp = ng * epg
    pad = LANES - ng - n_exp
    w_router = jnp.pad(jnp.concatenate([w_rg, w_re], axis=1), ((0, 0), (0, pad)))
    b_router = jnp.pad(jnp.concatenate([b_rg, b_re]), (0, pad))[None, :]
    route = _router(x, w_router, b_router, tm)
    pos, ends = _slots(route, tm, tile_rows)
    pos_flat = pos[:, :2].reshape(-1)
    n_slots = 2 * t + n_exp * tile_rows
    n_tiles = n_slots // tile_rows
    seg_end = ends[0, :n_exp]
    tile_start = jnp.arange(n_tiles, dtype=I32) * tile_rows
    tile_expert = jnp.minimum(jnp.sum(tile_start[:, None] >= seg_end[None, :], axis=1), n_exp - 1).astype(I32)
    n_used = (seg_end[n_exp - 1:] // tile_rows).astype(I32)
    xs = _scatter(pos_flat, x, n_slots, tm)
    f = w_gate.shape[-1]
    ys = _experts(tile_expert, n_used, xs, w_gate.reshape(-1, d, f), w_up.reshape(-1, d, f),
                  w_down.reshape(-1, f, d), tile_rows, layer * n_exp)
    return _combine(pos_flat, x, route, lng, lnb, ys, alpha=alpha, tm=tm, split_rows=split_rows)


def _ssd_in_kernel(x_ref, wz_ref, wx_ref, wdt_ref, z_ref, xbc_ref, dt_ref):
    xb = x_ref[...].astype(BF16)
    z_ref[...] = jnp.dot(xb, wz_ref[...], preferred_element_type=F32)
    xbc_ref[...] = jnp.dot(xb, wx_ref[...], preferred_element_type=F32)
    dt_ref[...] = jnp.dot(xb, wdt_ref[...], preferred_element_type=F32)


def _ssd_in(x, wz, wx, wdt, tm):
    t, d = x.shape
    outs = [wz.shape[1], wx.shape[1], wdt.shape[1]]
    return pl.pallas_call(
        _ssd_in_kernel,
        grid=(t // tm,),
        in_specs=[pl.BlockSpec((tm, d), lambda i: (i, 0))]
                 + [pl.BlockSpec(w.shape, lambda i: (0, 0)) for w in (wz, wx, wdt)],
        out_specs=[pl.BlockSpec((tm, n), lambda i: (i, 0)) for n in outs],
        out_shape=[jax.ShapeDtypeStruct((t, n), F32) for n in outs],
        compiler_params=_params(("parallel",)),
        name="ssd_in",
    )(x, wz, wx, wdt)


def _ssd_kernel(xbc_ref, dt_ref, z_ref, conv0_ref, h0_ref, cw_ref, cb_ref, dtb_ref, alog_ref,
                dexp_ref, nw_ref, expand_ref, y_ref, convo_ref, hout_ref, pad_sc, ht_sc, *, chunk):
    c = chunk
    ci = pl.program_id(1)
    d_inner = z_ref.shape[1]
    gn = SSD_GROUPS * D_STATE
    per_group = d_inner // SSD_GROUPS
    tail = 8

    @pl.when(ci == 0)
    def _():
        pad_sc[0:tail, :] = conv0_ref[...]
        ht_sc[...] = h0_ref[...].T

    pad_sc[tail:tail + c, :] = xbc_ref[...]
    acc = cb_ref[...]
    for i in range(CONV_WIDTH):
        off = tail - (CONV_WIDTH - 1) + i
        acc = acc + pad_sc[off:off + c, :] * cw_ref[i:i + 1, :]
    xbc = _silu(acc)
    last_rows = pad_sc[c:c + tail, :]
    pad_sc[0:tail, :] = last_rows
    xs = xbc[:, :d_inner]

    dt = _softplus(dt_ref[...] + dtb_ref[...])
    da = dt * (-jnp.exp(alog_ref[...]))
    tril_incl = (_iota2((c, c), 1) <= _iota2((c, c), 0)).astype(F32)
    acum = _mm_sel_l(tril_incl, da)
    a_last = acum[c - 1:c, :]

    def to_cols(t):
        if c < LANES:
            t = jnp.concatenate([t, jnp.zeros((LANES - c, LANES), F32)], axis=0)
        return t.T[:, :c]

    acum_t = to_cols(acum)
    dt_t = to_cols(dt)
    expand = expand_ref[...]
    ea = _mm_sel_r(jnp.exp(acum), expand)
    dd = _mm_sel_r(jnp.exp(a_last - acum) * dt, expand)
    ea_last = _mm_sel_r(jnp.exp(a_last), expand)
    causal = _iota2((c, c), 1) <= _iota2((c, c), 0)
    lane = _iota2((1, LANES), 1)
    m0 = (lane < HEAD).astype(F32)
    m1 = 1.0 - m0

    ys = []
    for g in range(SSD_GROUPS):
        b_g = xbc[:, d_inner + g * D_STATE:d_inner + (g + 1) * D_STATE]
        c_g = xbc[:, d_inner + gn + g * D_STATE:d_inner + gn + (g + 1) * D_STATE]
        cb = _mm_nt(c_g, b_g)
        gsl = slice(g * per_group, (g + 1) * per_group)
        ht_g = ht_sc[:, gsl]
        y_off = _mm(c_g, ht_g) * ea[:, gsl]
        ht_sc[:, gsl] = ht_g * ea_last[:, gsl] + _mm_tn(b_g, xs[:, gsl] * dd[:, gsl])
        pieces = []
        for q in range(per_group // LANES):
            j0 = (g * per_group) // HEAD + 2 * q
            ms = []
            for j in (j0, j0 + 1):
                seg = acum[:, j:j + 1] - acum_t[j:j + 1, :]
                ms.append(jnp.where(causal, cb * jnp.exp(jnp.where(causal, seg, 0.0)) * dt_t[j:j + 1, :], 0.0))
            lo = g * per_group + q * LANES
            x_pair = xs[:, lo:lo + LANES]
            x_stack = jnp.concatenate([x_pair * m0, x_pair * m1], axis=0)
            pieces.append(_mm(jnp.concatenate(ms, axis=1), x_stack))
        y_g = jnp.concatenate(pieces, axis=1) + y_off + xs[:, gsl] * dexp_ref[:, gsl]
        y_g = y_g * _silu(z_ref[:, gsl])
        y_g = y_g * lax.rsqrt(jnp.mean(y_g * y_g, axis=-1, keepdims=True) + RMS_EPS)
        ys.append(y_g * nw_ref[:, gsl])
    y_ref[...] = jnp.concatenate(ys, axis=1)

    @pl.when(ci == pl.num_programs(1) - 1)
    def _():
        convo_ref[...] = last_rows
        hout_ref[...] = ht_sc[...].T


def _ssd(xbc, dt, z, conv0, h0, cw, cb, dtb, alog, dexp, nw, expand, *, batch, seq, row0, chunk):
    d_inner = z.shape[1]
    conv_dim = xbc.shape[1]
    nc = seq // chunk
    blk0 = row0 // chunk

    def tok(n):
        return pl.BlockSpec((chunk, n), lambda b, c: (blk0 + b * nc + c, 0))

    def par(arr):
        return pl.BlockSpec(arr.shape, lambda b, c: (0, 0))

    return pl.pallas_call(
        functools.partial(_ssd_kernel, chunk=chunk),
        grid=(batch, nc),
        in_specs=[tok(conv_dim), tok(LANES), tok(d_inner),
                  pl.BlockSpec((None, 8, conv_dim), lambda b, c: (b, 0, 0)),
                  pl.BlockSpec((None, d_inner, D_STATE), lambda b, c: (b, 0, 0)),
                  par(cw), par(cb), par(dtb), par(alog), par(dexp), par(nw), par(expand)],
        out_specs=[pl.BlockSpec((chunk, d_inner), lambda b, c: (b * nc + c, 0)),
                   pl.BlockSpec((None, 8, conv_dim), lambda b, c: (b, 0, 0)),
                   pl.BlockSpec((None, d_inner, D_STATE), lambda b, c: (b, 0, 0))],
        out_shape=[jax.ShapeDtypeStruct((batch * seq, d_inner), F32),
                   jax.ShapeDtypeStruct((batch, 8, conv_dim), F32),
                   jax.ShapeDtypeStruct((batch, d_inner, D_STATE), F32)],
        scratch_shapes=[pltpu.VMEM((chunk + 8, conv_dim), F32), pltpu.VMEM((D_STATE, d_inner), F32)],
        compiler_params=_params(("parallel", "arbitrary")),
        name="ssd_l%d" % chunk,
    )(xbc, dt, z, conv0, h0, cw, cb, dtb, alog, dexp, nw, expand)


def _rwkv_layer(xp3, xs3, shift_s, wkv_s, prm, lng, lnb, *, alpha, tm):
    (mu, w_rkv, w0, w1, w2, a0, a1, a2, g1, g2, k_k, k_a, r_k, gn_w, gn_b, w_o) = prm
    bp, sp, d = xp3.shape
    bs, ss, _ = xs3.shape
    tp = bp * sp
    heads = d // HEAD
    x_p = xp3.reshape(tp, d)
    x_s = xs3.reshape(bs * ss, d)
    prev_s = jnp.concatenate([shift_s[:, None, :], xs3[:, :-1]], axis=1).reshape(bs * ss, d)
    row = lambda v: v.reshape(1, d)
    r, k, v, lw, a, g = _rwkv_pre(x_p, x_s, prev_s, sp, mu, w_rkv.astype(BF16), row(w0), w1.astype(BF16),
                                  w2.astype(BF16), row(a0), a1.astype(BF16), a2.astype(BF16),
                                  g1.astype(BF16), g2.astype(BF16), tm)
    pv = (row(k_k), row(k_a), row(r_k), row(gn_w), row(gn_b))
    o_p, sp_out = _wkv(r, k, v, lw, a, *pv, None, batch=bp, seq=sp, row0=0,
                       chunk=min(WKV_CHUNK, sp), pairs_per_step=min(WKV_PAIRS_PER_STEP, heads // 2))
    o_s, ss_out = _wkv(r, k, v, lw, a, *pv, wkv_s, batch=bs, seq=ss, row0=tp,
                       chunk=min(WKV_CHUNK, ss), pairs_per_step=heads // 2)
    x1 = _proj_ln(o_p, o_s, g, (x_p, x_s, 0), w_o.astype(BF16), lng, lnb, alpha=alpha, tm=tm)
    return x1, sp_out, ss_out


def _ssd_layer(x, bp, sp, bs, ss, conv_s, ssm_s, prm, lng, lnb, *, alpha, tm):
    (w_in, conv_w, conv_b, dt_bias, a_log, d_skip, norm_w, w_out) = prm
    t, d = x.shape
    tp = bp * sp
    heads = a_log.shape[0]
    d_inner = heads * HEAD
    conv_dim = conv_w.shape[1]
    wz = w_in[:, :d_inner].astype(BF16)
    wx = w_in[:, d_inner:d_inner + conv_dim].astype(BF16)
    wdt = jnp.pad(w_in[:, d_inner + conv_dim:], ((0, 0), (0, LANES - heads))).astype(BF16)
    z, xbc, dt = _ssd_in(x, wz, wx, wdt, tm)
    lane_pad = lambda v: jnp.pad(v, (0, LANES - heads))[None, :]
    expand = (jnp.arange(LANES)[:, None] == (jnp.arange(d_inner) // HEAD)[None, :]).astype(F32)
    common = (conv_w, conv_b[None, :], lane_pad(dt_bias), lane_pad(a_log),
              jnp.repeat(d_skip, HEAD)[None, :], norm_w[None, :], expand)
    conv_tail = CONV_WIDTH - 1
    pad_conv = lambda cs: jnp.pad(cs, ((0, 0), (8 - conv_tail, 0), (0, 0)))
    y_p, conv_p, h_p = _ssd(xbc, dt, z, jnp.zeros((bp, 8, conv_dim), F32),
                            jnp.zeros((bp, d_inner, D_STATE), F32), *common,
                            batch=bp, seq=sp, row0=0, chunk=SSD_CHUNK if sp % SSD_CHUNK == 0 else sp)
    y_s, conv_s_out, h_s = _ssd(xbc, dt, z, pad_conv(conv_s), ssm_s.reshape(bs, d_inner, D_STATE), *common,
                                batch=bs, seq=ss, row0=tp, chunk=SSD_CHUNK if ss % SSD_CHUNK == 0 else ss)
    x1 = _proj_ln(y_p, y_s, None, (x, x, tp // tm), w_out.astype(BF16), lng, lnb, alpha=alpha, tm=tm)
    shape_h = lambda h, b: h.reshape(b, heads, HEAD, D_STATE)
    return (x1, conv_p[:, 8 - conv_tail:], conv_s_out[:, 8 - conv_tail:], shape_h(h_p, bp), shape_h(h_s, bs))


def kernel(x_prompt, x_sample, state_rwkv_wkv, state_rwkv_shift, state_ssd_ssm, state_ssd_conv,
           rwkv_mu, rwkv_w_rkv, rwkv_w0, rwkv_w1, rwkv_w2, rwkv_a0, rwkv_a1, rwkv_a2,
           rwkv_g1, rwkv_g2, rwkv_k_k, rwkv_k_a, rwkv_r_k, rwkv_gn_w, rwkv_gn_b, rwkv_w_o,
           ssd_w_in, ssd_conv_w, ssd_conv_b, ssd_dt_bias, ssd_a_log, ssd_d, ssd_norm_w, ssd_w_out,
           ln_gain, ln_bias, moe_w_rg, moe_b_rg, moe_w_re, moe_b_re, moe_w_gate, moe_w_up, moe_w_down):
    bp, sp, d = x_prompt.shape
    bs, ss, _ = x_sample.shape
    depth = ln_gain.shape[0]
    alpha = (2 * depth) ** 0.25
    tp, ts = bp * sp, bs * ss
    tm = 256
    tile_rows = 256
    x = None
    xp3, xs3 = x_prompt, x_sample
    wkv_p, wkv_s, sh_p, sh_s, ssm_p, ssm_s, cv_p, cv_s = [], [], [], [], [], [], [], []
    for i in range(depth):
        j = i // 2
        lng = lambda n: ln_gain[i, n][None, :]
        lnb = lambda n: ln_bias[i, n][None, :]
        if i % 2 == 0:
            if x is not None:
                xp3, xs3 = x[:tp].reshape(bp, sp, d), x[tp:].reshape(bs, ss, d)
            prm = (rwkv_mu[j], rwkv_w_rkv[j], rwkv_w0[j], rwkv_w1[j], rwkv_w2[j], rwkv_a0[j],
                   rwkv_a1[j], rwkv_a2[j], rwkv_g1[j], rwkv_g2[j], rwkv_k_k[j], rwkv_k_a[j],
                   rwkv_r_k[j].reshape(-1), rwkv_gn_w[j], rwkv_gn_b[j], rwkv_w_o[j])
            sh_p.append(xp3[:, -1])
            sh_s.append(xs3[:, -1])
            x, w_p, w_s = _rwkv_layer(xp3, xs3, state_rwkv_shift[j], state_rwkv_wkv[j], prm,
                                      lng(0), lnb(0), alpha=alpha, tm=tm)
            wkv_p.append(w_p)
            wkv_s.append(w_s)
        else:
            prm = (ssd_w_in[j], ssd_conv_w[j], ssd_conv_b[j], ssd_dt_bias[j], ssd_a_log[j],
                   ssd_d[j], ssd_norm_w[j], ssd_w_out[j])
            x, c_p, c_s, h_p, h_s = _ssd_layer(x, bp, sp, bs, ss, state_ssd_conv[j], state_ssd_ssm[j],
                                               prm, lng(0), lnb(0), alpha=alpha, tm=tm)
            cv_p.append(c_p)
            cv_s.append(c_s)
            ssm_p.append(h_p)
            ssm_s.append(h_s)
        x = _hier_moe_ln(x, moe_w_rg[i], moe_b_rg[i], moe_w_re[i], moe_b_re[i], moe_w_gate,
                         moe_w_up, moe_w_down, lng(1), lnb(1), alpha=alpha, tm=tm,
                         tile_rows=tile_rows, layer=i, split_rows=tp if i == depth - 1 else None)
    yp = x[0].reshape(bp, sp, d)
    ys = x[1].reshape(bs, ss, d)
    return (yp, ys, jnp.stack(wkv_p), jnp.stack(wkv_s), jnp.stack(sh_p), jnp.stack(sh_s),
            jnp.stack(ssm_p), jnp.stack(ssm_s), jnp.stack(cv_p), jnp.stack(cv_s))
```

```python
import functools
import math

import jax
import jax.numpy as jnp
from jax import lax
from jax.experimental import pallas as pl
from jax.experimental.pallas import tpu as pltpu

F32 = jnp.float32
BF16 = jnp.bfloat16
I32 = jnp.int32

LANES = 128
HEAD = 64
GN_EPS = HEAD * 1e-5
RMS_EPS = 1e-5
LN_EPS = 1e-5
WKV_CHUNK = 64
WKV_PAIRS_PER_STEP = 8
SSD_CHUNK = 128
SSD_GROUPS = 4
D_STATE = 128
CONV_WIDTH = 4
EXPERT_GROUPS = 4
EXPERTS_PER_GROUP = 8
VMEM_LIMIT = 56 * 1024 * 1024


def _params(sem):
    return pltpu.CompilerParams(dimension_semantics=sem, vmem_limit_bytes=VMEM_LIMIT)


def _mm(a, b):
    return jnp.dot(a.astype(BF16), b.astype(BF16), preferred_element_type=F32)


def _mm_nt(a, b):
    return lax.dot_general(a.astype(BF16), b.astype(BF16), (((1,), (1,)), ((), ())),
                           preferred_element_type=F32)


def _mm_tn(a, b):
    return lax.dot_general(a.astype(BF16), b.astype(BF16), (((0,), (0,)), ((), ())),
                           preferred_element_type=F32)


def _split3(x):
    hi = x.astype(BF16)
    r1 = x - hi.astype(F32)
    mid = r1.astype(BF16)
    lo = (r1 - mid.astype(F32)).astype(BF16)
    return hi, mid, lo


def _mm_sel_l(sel, x):
    k = x.shape[1]
    parts = jnp.concatenate([p.astype(F32) for p in _split3(x)], axis=1).astype(BF16)
    y = jnp.dot(sel.astype(BF16), parts, preferred_element_type=F32)
    return y[:, :k] + y[:, k:2 * k] + y[:, 2 * k:]


def _mm_sel_r(x, sel):
    n = x.shape[0]
    if n % 8:
        return sum(jnp.dot(p, sel.astype(BF16), preferred_element_type=F32) for p in _split3(x))
    parts = jnp.concatenate([p.astype(F32) for p in _split3(x)], axis=0).astype(BF16)
    y = jnp.dot(parts, sel.astype(BF16), preferred_element_type=F32)
    return y[:n] + y[n:2 * n] + y[2 * n:]


def _sigmoid(x):
    return 1.0 / (1.0 + jnp.exp(-x))


def _silu(x):
    return x * _sigmoid(x)


def _softplus(x):
    return jnp.maximum(x, 0.0) + jnp.log(1.0 + jnp.exp(-jnp.abs(x)))


def _layer_norm(x, g, b):
    mu = jnp.mean(x, axis=-1, keepdims=True)
    xc = x - mu
    var = jnp.mean(xc * xc, axis=-1, keepdims=True)
    return xc * lax.rsqrt(var + LN_EPS) * g + b


def _iota2(shape, dim):
    return lax.broadcasted_iota(I32, shape, dim)


def _part_specs(tm, width, prompt_tiles, sample_block0=0):
    return [pl.BlockSpec((tm, width), lambda i: (jnp.minimum(i, prompt_tiles - 1), 0)),
            pl.BlockSpec((tm, width), lambda i: (sample_block0 + jnp.maximum(i - prompt_tiles, 0), 0))]


def _rwkv_pre_kernel(xa_ref, xb_ref, tail_ref, xprev_s_ref, mu_ref, wrkv_ref, w0_ref, w1_ref, w2_ref,
                     a0_ref, a1_ref, a2_ref, g1_ref, g2_ref, r_ref, k_ref, v_ref, lw_ref, a_ref, g_ref,
                     *, prompt_tiles, seq):
    i = pl.program_id(0)
    tm = xa_ref.shape[0]
    is_prompt = i < prompt_tiles
    xa = xa_ref[...]
    x = jnp.where(is_prompt, xa, xb_ref[...])
    first = jnp.where((i * tm) % seq == 0, 0.0, tail_ref[7:8, :])
    shifted = jnp.where(_iota2(xa.shape, 0) == 0, first, pltpu.roll(xa, 1, axis=0))
    xx = jnp.where(is_prompt, shifted, xprev_s_ref[...]) - x

    def mix(j):
        return x + xx * mu_ref[j:j + 1, :]

    r_ref[...] = _mm(mix(0), wrkv_ref[0])
    k_ref[...] = _mm(mix(1), wrkv_ref[1])
    v_ref[...] = _mm(mix(2), wrkv_ref[2])
    ww = w0_ref[...] + _mm(jnp.tanh(_mm(mix(3), w1_ref[...])), w2_ref[...])
    w_raw = -_softplus(-ww) - 0.5
    lw_ref[...] = -jnp.exp(w_raw)
    a_ref[...] = _sigmoid(a0_ref[...] + _mm(_mm(mix(4), a1_ref[...]), a2_ref[...]))
    g_ref[...] = _mm(_sigmoid(_mm(mix(5), g1_ref[...])), g2_ref[...])


def _rwkv_pre(x_p, x_s, xprev_s, seq, mu, wrkv, w0, w1, w2, a0, a1, a2, g1, g2, tm):
    (tp, d), ts = x_p.shape, x_s.shape[0]
    assert seq % tm == 0 and ts % tm == 0 and tm % 8 == 0
    npt = tp // tm
    row = pl.BlockSpec((tm, d), lambda i: (i, 0))
    tail = pl.BlockSpec((8, d), lambda i: (jnp.maximum(jnp.minimum(i, npt - 1) * (tm // 8) - 1, 0), 0))

    def full(arr):
        nd = arr.ndim
        return pl.BlockSpec(arr.shape, lambda i, _n=nd: (0,) * _n)

    weights = (mu, wrkv, w0, w1, w2, a0, a1, a2, g1, g2)
    return pl.pallas_call(
        functools.partial(_rwkv_pre_kernel, prompt_tiles=npt, seq=seq),
        grid=((tp + ts) // tm,),
        in_specs=_part_specs(tm, d, npt) + [tail, _part_specs(tm, d, npt)[1]] + [full(w) for w in weights],
        out_specs=[row] * 6,
        out_shape=[jax.ShapeDtypeStruct((tp + ts, d), F32)] * 6,
        compiler_params=_params(("parallel",)),
        name="rwkv_pre",
    )(x_p, x_s, x_p, xprev_s, *weights)


def _wkv_kernel(r_ref, k_ref, v_ref, lw_ref, a_ref, kkw_ref, kaw_ref, rk_ref, gnw_ref, gnb_ref,
                *rest, chunk, pairs_per_step, has_init):
    if has_init:
        s0_ref, o_ref, sout_ref, s_sc = rest
    else:
        o_ref, sout_ref, s_sc = rest
    L = chunk
    c = pl.program_id(2)

    @pl.when(c == 0)
    def _():
        if has_init:
            zero = jnp.zeros((HEAD, HEAD), F32)
            for q in range(pairs_per_step):
                top = jnp.concatenate([s0_ref[2 * q], zero], axis=1)
                bot = jnp.concatenate([zero, s0_ref[2 * q + 1]], axis=1)
                s_sc[q] = jnp.concatenate([top, bot], axis=0)
        else:
            s_sc[...] = jnp.zeros_like(s_sc)

    lane = _iota2((1, LANES), 1)
    m0 = (lane < HEAD).astype(F32)
    m1 = 1.0 - m0
    same_head = (_iota2((LANES, LANES), 0) // HEAD == _iota2((LANES, LANES), 1) // HEAD).astype(F32)
    tril_incl = (_iota2((L, L), 1) <= _iota2((L, L), 0)).astype(F32)
    n2 = 2 * L
    row = _iota2((n2, n2), 0)
    col = _iota2((n2, n2), 1)
    strict = col < row
    incl = col <= row
    steps = int(math.log2(L))

    def headsum(t):
        return _mm_sel_r(t, same_head)

    def stack(t):
        return jnp.concatenate([t * m0, t * m1], axis=0)

    pairs = range(pairs_per_step)
    lanes = [slice(q * LANES, (q + 1) * LANES) for q in pairs]

    def prepare(ls):
        r = r_ref[:, ls]
        k = k_ref[:, ls]
        v = v_ref[:, ls]
        lw = lw_ref[:, ls]
        a = a_ref[:, ls]
        kk = k * kkw_ref[:, ls]
        k2 = k * (1.0 + (a - 1.0) * kaw_ref[:, ls])
        sums = headsum(jnp.concatenate([kk * kk, r * k2 * rk_ref[:, ls]], axis=0))
        kk = kk / jnp.maximum(jnp.sqrt(sums[:L]), 1e-12)
        bonus = sums[L:] * v
        bvec = kk * a
        cl = _mm_sel_l(tril_incl, lw)
        total = cl[L - 1:L, :]
        w_in = jnp.exp(cl)
        w_inv = jnp.exp(-cl)
        w_end = jnp.exp(total - cl)
        return dict(
            ar=jnp.concatenate([stack(-kk * jnp.exp(cl - lw)), stack(r * w_in)], axis=0).astype(BF16),
            bk=jnp.concatenate([stack(bvec * w_inv), stack(k2 * w_inv)], axis=0).astype(BF16),
            bk_end=jnp.concatenate([stack(bvec * w_end), stack(k2 * w_end)], axis=0),
            v_s=stack(v), total=total, bonus=bonus)

    ops = [prepare(ls) for ls in lanes]
    states = [s_sc[q] for q in pairs]
    cross = [_mm_nt(o["ar"], o["bk"]) for o in ops]
    from_s = [_mm_nt(o["ar"], s) for o, s in zip(ops, states)]
    n_ab = [jnp.where(strict, x[:n2, :n2], 0.0) for x in cross]
    a_rb = [jnp.where(incl, x[n2:, :n2], 0.0) for x in cross]
    to_v = [jnp.concatenate([jnp.where(strict, x[:n2, n2:], 0.0), jnp.where(incl, x[n2:, n2:], 0.0)], axis=0)
            for x in cross]
    from_v = [_mm(m, o["v_s"]) for m, o in zip(to_v, ops)]

    u = [fs[:n2] + fv[:n2] for fs, fv in zip(from_s, from_v)]
    p = n_ab
    for j in range(steps):
        if j + 1 < steps:
            both = [_mm(pi, jnp.concatenate([ui, pi], axis=1)) for ui, pi in zip(u, p)]
            u = [ui + b[:, :LANES] for ui, b in zip(u, both)]
            p = [b[:, LANES:] for b in both]
        else:
            u = [ui + _mm(pi, ui) for ui, pi in zip(u, p)]

    y_s = [fs[n2:] + fv[n2:] + _mm(mb, ui) for fs, fv, mb, ui in zip(from_s, from_v, a_rb, u)]
    s_new = [s * jnp.exp(o["total"]) + _mm_tn(jnp.concatenate([ui, o["v_s"]], axis=0), o["bk_end"])
             for o, s, ui in zip(ops, states, u)]
    for q in pairs:
        s_sc[q] = s_new[q]
        y = y_s[q][:L] + y_s[q][L:]
        ym = headsum(y) * (1.0 / HEAD)
        yc = y - ym
        yv = headsum(yc * yc) * (1.0 / HEAD)
        ls = lanes[q]
        o_ref[:, ls] = yc * lax.rsqrt(yv + GN_EPS) * gnw_ref[:, ls] + gnb_ref[:, ls] + ops[q]["bonus"]

    @pl.when(c == pl.num_programs(2) - 1)
    def _():
        for q in pairs:
            sout_ref[2 * q] = s_new[q][:HEAD, :HEAD]
            sout_ref[2 * q + 1] = s_new[q][HEAD:, HEAD:]


def _wkv(r, k, v, lw, a, kkw, kaw, rk, gnw, gnb, s0, *, batch, seq, row0, chunk, pairs_per_step):
    d = r.shape[1]
    pps = pairs_per_step
    width = pps * LANES
    nc = seq // chunk
    blk0 = row0 // chunk
    tok = pl.BlockSpec((chunk, width), lambda b, p, c: (blk0 + b * nc + c, p))
    out = pl.BlockSpec((chunk, width), lambda b, p, c: (b * nc + c, p))
    par = pl.BlockSpec((1, width), lambda b, p, c: (0, p))
    st = pl.BlockSpec((None, 2 * pps, HEAD, HEAD), lambda b, p, c: (b, p, 0, 0))
    has_init = s0 is not None
    return pl.pallas_call(
        functools.partial(_wkv_kernel, chunk=chunk, pairs_per_step=pps, has_init=has_init),
        grid=(batch, d // width, nc),
        in_specs=[tok] * 5 + [par] * 5 + ([st] if has_init else []),
        out_specs=[out, st],
        out_shape=[jax.ShapeDtypeStruct((batch * seq, d), F32),
                   jax.ShapeDtypeStruct((batch, d // HEAD, HEAD, HEAD), F32)],
        scratch_shapes=[pltpu.VMEM((pps, LANES, LANES), F32)],
        compiler_params=_params(("parallel", "parallel", "arbitrary")),
        name="wkv_l%d" % chunk,
    )(r, k, v, lw, a, kkw, kaw, rk, gnw, gnb, *([s0] if has_init else []))


def _proj_ln_kernel(op_ref, os_ref, g_ref, xa_ref, xb_ref, w_ref, lng_ref, lnb_ref, out_ref, *, alpha,
                    gated, prompt_tiles):
    is_prompt = pl.program_id(0) < prompt_tiles
    o = jnp.where(is_prompt, op_ref[...], os_ref[...])
    if gated:
        o = o * g_ref[...]
    y = _mm(o, w_ref[...])
    x = jnp.where(is_prompt, xa_ref[...], xb_ref[...])
    out_ref[...] = _layer_norm(alpha * x + y, lng_ref[...], lnb_ref[...])


def _proj_ln(o_p, o_s, g, x_parts, w, lng, lnb, *, alpha, tm):
    x_a, x_b, sample_block0 = x_parts
    d = x_a.shape[1]
    kin = o_p.shape[1]
    npt = o_p.shape[0] // tm
    t = o_p.shape[0] + o_s.shape[0]
    gated = g is not None
    g_spec = pl.BlockSpec((tm, kin), lambda i: (i, 0))
    if not gated:
        g, g_spec = lng, pl.BlockSpec((1, d), lambda i: (0, 0))
    return pl.pallas_call(
        functools.partial(_proj_ln_kernel, alpha=alpha, gated=gated, prompt_tiles=npt),
        grid=(t // tm,),
        in_specs=_part_specs(tm, kin, npt) + [g_spec] + _part_specs(tm, d, npt, sample_block0)
                 + [pl.BlockSpec(w.shape, lambda i: (0, 0)),
                    pl.BlockSpec((1, d), lambda i: (0, 0)),
                    pl.BlockSpec((1, d), lambda i: (0, 0))],
        out_specs=pl.BlockSpec((tm, d), lambda i: (i, 0)),
        out_shape=jax.ShapeDtypeStruct((t, d), F32),
        compiler_params=_params(("parallel",)),
        name="proj_ln",
    )(o_p, o_s, g, x_a, x_b, w, lng, lnb)


def _router_kernel(x_ref, w_ref, b_ref, route_ref, cnt_ref):
    ng, epg = EXPERT_GROUPS, EXPERTS_PER_GROUP
    x = x_ref[...]
    x_hi = x.astype(BF16)
    x_lo = (x - x_hi.astype(F32)).astype(BF16)
    by_hi = jnp.dot(x_hi, w_ref[...], preferred_element_type=F32)
    logits = (by_hi[:, :LANES] + by_hi[:, LANES:]
              + jnp.dot(x_lo, w_ref[:, :LANES], preferred_element_type=F32)
              + b_ref[...])
    lane = _iota2(logits.shape, 1)
    neg = jnp.float32(-jnp.inf)
    big = jnp.int32(1 << 20)

    is_g = lane < ng
    lg = jnp.where(is_g, logits, neg)
    mg = jnp.max(lg, axis=-1, keepdims=True)
    g_sel = jnp.min(jnp.where(lg == mg, lane, big), axis=-1, keepdims=True)
    zg = jnp.sum(jnp.where(is_g, jnp.exp(lg - mg), 0.0), axis=-1, keepdims=True)
    p_group = 1.0 / zg

    e_lane = lane - ng
    in_sel = (e_lane >= g_sel * epg) & (e_lane < (g_sel + 1) * epg)
    le = jnp.where(in_sel, logits, neg)
    m1 = jnp.max(le, axis=-1, keepdims=True)
    i1 = jnp.min(jnp.where(le == m1, lane, big), axis=-1, keepdims=True)
    ze = jnp.sum(jnp.where(in_sel, jnp.exp(le - m1), 0.0), axis=-1, keepdims=True)
    le2 = jnp.where(lane == i1, neg, le)
    m2 = jnp.max(le2, axis=-1, keepdims=True)
    i2 = jnp.min(jnp.where(le2 == m2, lane, big), axis=-1, keepdims=True)
    pe1 = 1.0 / ze
    pe2 = jnp.exp(m2 - m1) / ze
    den = pe1 + pe2
    c1 = pe1 / den * p_group
    c2 = pe2 / den * p_group
    e1 = i1 - ng
    e2 = i2 - ng
    route_ref[...] = jnp.where(lane == 0, e1.astype(F32),
                     jnp.where(lane == 1, e2.astype(F32),
                     jnp.where(lane == 2, c1, jnp.where(lane == 3, c2, 0.0))))

    @pl.when(pl.program_id(0) == 0)
    def _():
        cnt_ref[...] = jnp.zeros_like(cnt_ref)

    chosen = ((lane == e1) | (lane == e2)).astype(F32)
    cnt_ref[...] += jnp.sum(chosen, axis=0, keepdims=True)


def _router(x, w_terms, b, tm):
    t, d = x.shape
    return pl.pallas_call(
        _router_kernel,
        grid=(t // tm,),
        in_specs=[pl.BlockSpec((tm, d), lambda i: (i, 0)),
                  pl.BlockSpec((d, 2 * LANES), lambda i: (0, 0)),
                  pl.BlockSpec((1, LANES), lambda i: (0, 0))],
        out_specs=[pl.BlockSpec((tm, LANES), lambda i: (i, 0)),
                   pl.BlockSpec((1, LANES), lambda i: (0, 0))],
        out_shape=[jax.ShapeDtypeStruct((t, LANES), F32),
                   jax.ShapeDtypeStruct((1, LANES), F32)],
        compiler_params=_params(("arbitrary",)),
        name="moe_router",
    )(x, w_terms, b)


def _slots_kernel(route_ref, cnt_ref, pos_ref, ends_ref, off_sc, *, tile_rows):
    i = pl.program_id(0)
    route = route_ref[...]
    tm = route.shape[0]
    lane = _iota2((tm, LANES), 1)
    oh1 = lane == route[:, 0:1].astype(I32)
    oh2 = lane == route[:, 1:2].astype(I32)
    oh = (oh1 | oh2).astype(F32)

    @pl.when(i == 0)
    def _():
        padded = jnp.ceil(cnt_ref[...] * (1.0 / tile_rows)) * tile_rows
        before = (_iota2((LANES, LANES), 0) < _iota2((LANES, LANES), 1)).astype(F32)
        starts = _mm_sel_r(padded, before)
        off_sc[...] = starts
        ends_ref[...] = (starts + padded).astype(I32)

    earlier = (_iota2((tm, tm), 1) < _iota2((tm, tm), 0)).astype(F32)
    base = off_sc[...] + _mm(earlier, oh)
    p1 = jnp.sum(jnp.where(oh1, base, 0.0), axis=-1, keepdims=True)
    p2 = jnp.sum(jnp.where(oh2, base, 0.0), axis=-1, keepdims=True)
    pos_ref[...] = jnp.where(lane == 0, p1, jnp.where(lane == 1, p2, 0.0)).astype(I32)
    off_sc[...] += jnp.sum(oh, axis=0, keepdims=True)


def _slots(route, counts, tm, tile_rows):
    t = route.shape[0]
    return pl.pallas_call(
        functools.partial(_slots_kernel, tile_rows=tile_rows),
        grid=(t // tm,),
        in_specs=[pl.BlockSpec((tm, LANES), lambda i: (i, 0)),
                  pl.BlockSpec((1, LANES), lambda i: (0, 0))],
        out_specs=[pl.BlockSpec((tm, LANES), lambda i: (i, 0)),
                   pl.BlockSpec((1, LANES), lambda i: (0, 0))],
        out_shape=[jax.ShapeDtypeStruct((t, LANES), I32),
                   jax.ShapeDtypeStruct((1, LANES), I32)],
        scratch_shapes=[pltpu.VMEM((1, LANES), F32)],
        compiler_params=_params(("arbitrary",)),
        name="moe_slots",
    )(route, counts)


DMA_UNROLL = 8


def _scatter_kernel(pos_ref, te_ref, nu_ref, x_ref, xs_ref, zero_sc, sem, zsem, *, tile_rows):
    i = pl.program_id(0)
    tm = x_ref.shape[0]
    n_tiles = xs_ref.shape[0] // tile_rows

    @pl.when(i == 0)
    def _():
        zero_sc[...] = jnp.zeros_like(zero_sc)

        def zcopy(j):
            return pltpu.make_async_copy(zero_sc, xs_ref.at[pl.ds(j * tile_rows, tile_rows)], zsem)

        def fill(j, n):
            partial = (j >= nu_ref[0] - 1) | (te_ref[j] != te_ref[jnp.minimum(j + 1, n_tiles - 1)])

            @pl.when(partial)
            def _():
                zcopy(j).start()
            return n + partial.astype(I32)

        n_fill = lax.fori_loop(0, n_tiles, fill, 0)

        def drain(_, carry):
            zcopy(0).wait()
            return carry
        lax.fori_loop(0, n_fill, drain, 0)

    def copy(t, c):
        p = pos_ref[2 * (i * tm + t) + c]
        return pltpu.make_async_copy(x_ref.at[pl.ds(t, 1)], xs_ref.at[pl.ds(p, 1)], sem)

    def start(t, carry):
        copy(t, 0).start(priority=0)
        copy(t, 1).start(priority=1)
        return carry

    def wait(t, carry):
        copy(t, 0).wait()
        copy(t, 1).wait()
        return carry

    lax.fori_loop(0, tm, start, 0, unroll=DMA_UNROLL)
    lax.fori_loop(0, tm, wait, 0, unroll=DMA_UNROLL)


def _scatter(pos_flat, tile_expert, n_used, x, n_slots, tm, tile_rows):
    t, d = x.shape
    grid_spec = pltpu.PrefetchScalarGridSpec(
        num_scalar_prefetch=3,
        grid=(t // tm,),
        in_specs=[pl.BlockSpec((tm, d), lambda i, pos, te, nu: (i, 0))],
        out_specs=pl.BlockSpec(memory_space=pl.ANY),
        scratch_shapes=[pltpu.VMEM((tile_rows, d), F32), pltpu.SemaphoreType.DMA(()),
                        pltpu.SemaphoreType.DMA(())],
    )
    return pl.pallas_call(
        functools.partial(_scatter_kernel, tile_rows=tile_rows),
        grid_spec=grid_spec,
        out_shape=jax.ShapeDtypeStruct((n_slots, d), F32),
        compiler_params=_params(("arbitrary",)),
        name="moe_scatter",
    )(pos_flat, tile_expert, n_used, x)


def _experts_kernel(te_ref, nu_ref, xs_ref, wg_ref, wu_ref, wd_ref, out_ref, wg_sc, wu_sc, wd_sc):
    j = pl.program_id(0)
    used = j < nu_ref[0]
    prev = te_ref[jnp.maximum(j - 1, 0)]

    @pl.when(used & ((j == 0) | (te_ref[j] != prev)))
    def _():
        wg_sc[...] = wg_ref[...].astype(BF16)
        wu_sc[...] = wu_ref[...].astype(BF16)
        wd_sc[...] = wd_ref[...].astype(BF16)

    @pl.when(used)
    def _():
        xb = xs_ref[...].astype(BF16)
        h = _silu(jnp.dot(xb, wg_sc[...], preferred_element_type=F32))
        h = h * jnp.dot(xb, wu_sc[...], preferred_element_type=F32)
        out_ref[...] = jnp.dot(h.astype(BF16), wd_sc[...], preferred_element_type=F32)

    @pl.when(jnp.logical_not(used))
    def _():
        out_ref[...] = jnp.zeros_like(out_ref)


def _experts(tile_expert, n_used, xs, wg, wu, wd, tile_rows, expert0):
    n_slots, d = xs.shape
    f = wg.shape[-1]

    def last_used(j, nu):
        return jnp.maximum(jnp.minimum(j, nu[0] - 1), 0)

    def rows(j, te, nu):
        return (last_used(j, nu), 0)

    def wsel(j, te, nu):
        return (expert0 + te[last_used(j, nu)], 0, 0)

    grid_spec = pltpu.PrefetchScalarGridSpec(
        num_scalar_prefetch=2,
        grid=(n_slots // tile_rows,),
        in_specs=[pl.BlockSpec((tile_rows, d), rows),
                  pl.BlockSpec((None, d, f), wsel),
                  pl.BlockSpec((None, d, f), wsel),
                  pl.BlockSpec((None, f, d), wsel)],
        out_specs=pl.BlockSpec((tile_rows, d), lambda j, te, nu: (j, 0)),
        scratch_shapes=[pltpu.VMEM((d, f), BF16), pltpu.VMEM((d, f), BF16), pltpu.VMEM((f, d), BF16)],
    )
    return pl.pallas_call(
        _experts_kernel,
        grid_spec=grid_spec,
        out_shape=jax.ShapeDtypeStruct((n_slots, d), F32),
        compiler_params=_params(("arbitrary",)),
        name="moe_experts",
    )(tile_expert, n_used, xs, wg, wu, wd)


def _combine_kernel(pos_ref, x_ref, route_ref, lng_ref, lnb_ref, ys_ref, *rest, alpha, split_tiles):
    if split_tiles is None:
        out_ref, o_sc, sem = rest
    else:
        out_ref, out_s_ref, o_sc, sem = rest
    i = pl.program_id(0)
    n = pl.num_programs(0)
    tm = x_ref.shape[0]
    slot = i % 2

    def copy(tile, sl, t, c):
        p = pos_ref[2 * (tile * tm + t) + c]
        return pltpu.make_async_copy(ys_ref.at[pl.ds(p, 1)], o_sc.at[sl, c, pl.ds(t, 1)], sem.at[sl])

    def start_tile(tile, sl):
        def body(t, carry):
            copy(tile, sl, t, 0).start(priority=0)
            copy(tile, sl, t, 1).start(priority=1)
            return carry
        lax.fori_loop(0, tm, body, 0, unroll=DMA_UNROLL)

    @pl.when(i == 0)
    def _():
        start_tile(0, 0)

    @pl.when(i + 1 < n)
    def _():
        start_tile(i + 1, 1 - slot)

    def wait(t, carry):
        copy(i, slot, t, 0).wait()
        copy(i, slot, t, 1).wait()
        return carry

    lax.fori_loop(0, tm, wait, 0, unroll=DMA_UNROLL)
    route = route_ref[...]
    y = route[:, 2:3] * o_sc[slot, 0] + route[:, 3:4] * o_sc[slot, 1]
    res = _layer_norm(alpha * x_ref[...] + y, lng_ref[...], lnb_ref[...])
    if split_tiles is None:
        out_ref[...] = res
    else:
        @pl.when(i < split_tiles)
        def _():
            out_ref[...] = res

        @pl.when(i >= split_tiles)
        def _():
            out_s_ref[...] = res


def _combine(pos_flat, x, route, lng, lnb, ys, *, alpha, tm, split_rows=None):
    t, d = x.shape
    out_specs = pl.BlockSpec((tm, d), lambda i, pos: (i, 0))
    out_shape = jax.ShapeDtypeStruct((t, d), F32)
    npt = None
    if split_rows is not None:
        npt = split_rows // tm
        out_specs = [pl.BlockSpec((tm, d), lambda i, pos: (jnp.minimum(i, npt - 1), 0)),
                     pl.BlockSpec((tm, d), lambda i, pos: (jnp.maximum(i - npt, 0), 0))]
        out_shape = [jax.ShapeDtypeStruct((split_rows, d), F32),
                     jax.ShapeDtypeStruct((t - split_rows, d), F32)]
    grid_spec = pltpu.PrefetchScalarGridSpec(
        num_scalar_prefetch=1,
        grid=(t // tm,),
        in_specs=[pl.BlockSpec((tm, d), lambda i, pos: (i, 0)),
                  pl.BlockSpec((tm, LANES), lambda i, pos: (i, 0)),
                  pl.BlockSpec((1, d), lambda i, pos: (0, 0)),
                  pl.BlockSpec((1, d), lambda i, pos: (0, 0)),
                  pl.BlockSpec(memory_space=pl.ANY)],
        out_specs=out_specs,
        scratch_shapes=[pltpu.VMEM((2, 2, tm, d), F32), pltpu.SemaphoreType.DMA((2,))],
    )
    return pl.pallas_call(
        functools.partial(_combine_kernel, alpha=alpha, split_tiles=npt),
        grid_spec=grid_spec,
        out_shape=out_shape,
        compiler_params=_params(("arbitrary",)),
        name="moe_combine",
    )(pos_flat, x, route, lng, lnb, ys)


def _hier_moe_ln(x, w_rg, b_rg, w_re, b_re, w_gate, w_up, w_down, lng, lnb, *, alpha, tm, tile_rows,
                 layer=0, split_rows=None):
    t, d = x.shape
    ng, epg = EXPERT_GROUPS, EXPERTS_PER_GROUP
    n_exp = ng * epg
    pad = LANES - ng - n_exp
    w_router = jnp.pad(jnp.concatenate([w_rg, w_re], axis=1), ((0, 0), (0, pad)))
    b_router = jnp.pad(jnp.concatenate([b_rg, b_re]), (0, pad))[None, :]
    w_hi = w_router.astype(BF16)
    w_lo = (w_router - w_hi.astype(F32)).astype(BF16)
    route, counts = _router(x, jnp.concatenate([w_hi, w_lo], axis=1), b_router, tm)
    pos, ends = _slots(route, counts, tm, tile_rows)
    pos_flat = pos[:, :2].reshape(-1)
    n_slots = 2 * t + n_exp * tile_rows
    n_tiles = n_slots // tile_rows
    seg_end = ends[0, :n_exp]
    tile_start = jnp.arange(n_tiles, dtype=I32) * tile_rows
    tile_expert = jnp.minimum(jnp.sum(tile_start[:, None] >= seg_end[None, :], axis=1), n_exp - 1).astype(I32)
    n_used = (seg_end[n_exp - 1:] // tile_rows).astype(I32)
    xs = _scatter(pos_flat, tile_expert, n_used, x, n_slots, tm, tile_rows)
    f = w_gate.shape[-1]
    ys = _experts(tile_expert, n_used, xs, w_gate.reshape(-1, d, f), w_up.reshape(-1, d, f),
                  w_down.reshape(-1, f, d), tile_rows, layer * n_exp)
    return _combine(pos_flat, x, route, lng, lnb, ys, alpha=alpha, tm=tm, split_rows=split_rows)


def _ssd_in_kernel(x_ref, wz_ref, wx_ref, wdt_ref, z_ref, xbc_ref, dt_ref):
    xb = x_ref[...].astype(BF16)
    z_ref[...] = jnp.dot(xb, wz_ref[...], preferred_element_type=F32)
    xbc_ref[...] = jnp.dot(xb, wx_ref[...], preferred_element_type=F32)
    dt_ref[...] = jnp.dot(xb, wdt_ref[...], preferred_element_type=F32)


def _ssd_in(x, wz, wx, wdt, tm):
    t, d = x.shape
    outs = [wz.shape[1], wx.shape[1], wdt.shape[1]]
    return pl.pallas_call(
        _ssd_in_kernel,
        grid=(t // tm,),
        in_specs=[pl.BlockSpec((tm, d), lambda i: (i, 0))]
                 + [pl.BlockSpec(w.shape, lambda i: (0, 0)) for w in (wz, wx, wdt)],
        out_specs=[pl.BlockSpec((tm, n), lambda i: (i, 0)) for n in outs],
        out_shape=[jax.ShapeDtypeStruct((t, n), F32) for n in outs],
        compiler_params=_params(("parallel",)),
        name="ssd_in",
    )(x, wz, wx, wdt)


def _ssd_kernel(xbc_ref, dt_ref, z_ref, conv0_ref, h0_ref, cw_ref, cb_ref, dtb_ref, alog_ref,
                dexp_ref, nw_ref, expand_ref, y_ref, convo_ref, hout_ref, pad_sc, ht_sc, *, chunk):
    c = chunk
    ci = pl.program_id(1)
    d_inner = z_ref.shape[1]
    gn = SSD_GROUPS * D_STATE
    per_group = d_inner // SSD_GROUPS
    tail = 8

    @pl.when(ci == 0)
    def _():
        pad_sc[0:tail, :] = conv0_ref[...]
        ht_sc[...] = h0_ref[...].T

    pad_sc[tail:tail + c, :] = xbc_ref[...]
    acc = cb_ref[...]
    for i in range(CONV_WIDTH):
        off = tail - (CONV_WIDTH - 1) + i
        acc = acc + pad_sc[off:off + c, :] * cw_ref[i:i + 1, :]
    xbc = _silu(acc)
    last_rows = pad_sc[c:c + tail, :]
    pad_sc[0:tail, :] = last_rows
    xs = xbc[:, :d_inner]

    dt = _softplus(dt_ref[...] + dtb_ref[...])
    da = dt * (-jnp.exp(alog_ref[...]))
    tril_incl = (_iota2((c, c), 1) <= _iota2((c, c), 0)).astype(F32)
    acum = _mm_sel_l(tril_incl, da)
    a_last = acum[c - 1:c, :]

    def to_cols(t):
        if c < LANES:
            t = jnp.concatenate([t, jnp.zeros((LANES - c, LANES), F32)], axis=0)
        return t.T[:, :c]

    acum_t = to_cols(acum)
    dt_t = to_cols(dt)
    expand = expand_ref[...]
    ea = _mm_sel_r(jnp.exp(acum), expand)
    dd = _mm_sel_r(jnp.exp(a_last - acum) * dt, expand)
    ea_last = _mm_sel_r(jnp.exp(a_last), expand)
    causal = _iota2((c, c), 1) <= _iota2((c, c), 0)
    lane = _iota2((1, LANES), 1)
    m0 = (lane < HEAD).astype(F32)
    m1 = 1.0 - m0

    ys = []
    for g in range(SSD_GROUPS):
        b_g = xbc[:, d_inner + g * D_STATE:d_inner + (g + 1) * D_STATE]
        c_g = xbc[:, d_inner + gn + g * D_STATE:d_inner + gn + (g + 1) * D_STATE]
        cb = _mm_nt(c_g, b_g)
        gsl = slice(g * per_group, (g + 1) * per_group)
        ht_g = ht_sc[:, gsl]
        y_off = _mm(c_g, ht_g) * ea[:, gsl]
        ht_sc[:, gsl] = ht_g * ea_last[:, gsl] + _mm_tn(b_g, xs[:, gsl] * dd[:, gsl])
        pieces = []
        for q in range(per_group // LANES):
            j0 = (g * per_group) // HEAD + 2 * q
            ms = []
            for j in (j0, j0 + 1):
                seg = acum[:, j:j + 1] - acum_t[j:j + 1, :]
                ms.append(jnp.where(causal, cb * jnp.exp(jnp.where(causal, seg, 0.0)) * dt_t[j:j + 1, :], 0.0))
            lo = g * per_group + q * LANES
            x_pair = xs[:, lo:lo + LANES]
            x_stack = jnp.concatenate([x_pair * m0, x_pair * m1], axis=0)
            pieces.append(_mm(jnp.concatenate(ms, axis=1), x_stack))
        y_g = jnp.concatenate(pieces, axis=1) + y_off + xs[:, gsl] * dexp_ref[:, gsl]
        y_g = y_g * _silu(z_ref[:, gsl])
        y_g = y_g * lax.rsqrt(jnp.mean(y_g * y_g, axis=-1, keepdims=True) + RMS_EPS)
        ys.append(y_g * nw_ref[:, gsl])
    y_ref[...] = jnp.concatenate(ys, axis=1)

    @pl.when(ci == pl.num_programs(1) - 1)
    def _():
        convo_ref[...] = last_rows
        hout_ref[...] = ht_sc[...].T


def _ssd(xbc, dt, z, conv0, h0, cw, cb, dtb, alog, dexp, nw, expand, *, batch, seq, row0, chunk):
    d_inner = z.shape[1]
    conv_dim = xbc.shape[1]
    nc = seq // chunk
    blk0 = row0 // chunk

    def tok(n):
        return pl.BlockSpec((chunk, n), lambda b, c: (blk0 + b * nc + c, 0))

    def par(arr):
        return pl.BlockSpec(arr.shape, lambda b, c: (0, 0))

    return pl.pallas_call(
        functools.partial(_ssd_kernel, chunk=chunk),
        grid=(batch, nc),
        in_specs=[tok(conv_dim), tok(LANES), tok(d_inner),
                  pl.BlockSpec((None, 8, conv_dim), lambda b, c: (b, 0, 0)),
                  pl.BlockSpec((None, d_inner, D_STATE), lambda b, c: (b, 0, 0)),
                  par(cw), par(cb), par(dtb), par(alog), par(dexp), par(nw), par(expand)],
        out_specs=[pl.BlockSpec((chunk, d_inner), lambda b, c: (b * nc + c, 0)),
                   pl.BlockSpec((None, 8, conv_dim), lambda b, c: (b, 0, 0)),
                   pl.BlockSpec((None, d_inner, D_STATE), lambda b, c: (b, 0, 0))],
        out_shape=[jax.ShapeDtypeStruct((batch * seq, d_inner), F32),
                   jax.ShapeDtypeStruct((batch, 8, conv_dim), F32),
                   jax.ShapeDtypeStruct((batch, d_inner, D_STATE), F32)],
        scratch_shapes=[pltpu.VMEM((chunk + 8, conv_dim), F32), pltpu.VMEM((D_STATE, d_inner), F32)],
        compiler_params=_params(("parallel", "arbitrary")),
        name="ssd_l%d" % chunk,
    )(xbc, dt, z, conv0, h0, cw, cb, dtb, alog, dexp, nw, expand)


def _rwkv_layer(xp3, xs3, shift_s, wkv_s, prm, lng, lnb, *, alpha, tm):
    (mu, w_rkv, w0, w1, w2, a0, a1, a2, g1, g2, k_k, k_a, r_k, gn_w, gn_b, w_o) = prm
    bp, sp, d = xp3.shape
    bs, ss, _ = xs3.shape
    tp = bp * sp
    heads = d // HEAD
    x_p = xp3.reshape(tp, d)
    x_s = xs3.reshape(bs * ss, d)
    prev_s = jnp.concatenate([shift_s[:, None, :], xs3[:, :-1]], axis=1).reshape(bs * ss, d)
    row = lambda v: v.reshape(1, d)
    r, k, v, lw, a, g = _rwkv_pre(x_p, x_s, prev_s, sp, mu, w_rkv.astype(BF16), row(w0), w1.astype(BF16),
                                  w2.astype(BF16), row(a0), a1.astype(BF16), a2.astype(BF16),
                                  g1.astype(BF16), g2.astype(BF16), tm)
    pv = (row(k_k), row(k_a), row(r_k), row(gn_w), row(gn_b))
    o_p, sp_out = _wkv(r, k, v, lw, a, *pv, None, batch=bp, seq=sp, row0=0,
                       chunk=min(WKV_CHUNK, sp), pairs_per_step=min(WKV_PAIRS_PER_STEP, heads // 2))
    o_s, ss_out = _wkv(r, k, v, lw, a, *pv, wkv_s, batch=bs, seq=ss, row0=tp,
                       chunk=min(WKV_CHUNK, ss), pairs_per_step=heads // 2)
    x1 = _proj_ln(o_p, o_s, g, (x_p, x_s, 0), w_o.astype(BF16), lng, lnb, alpha=alpha, tm=tm)
    return x1, sp_out, ss_out


def _ssd_layer(x, bp, sp, bs, ss, conv_s, ssm_s, prm, lng, lnb, *, alpha, tm):
    (w_in, conv_w, conv_b, dt_bias, a_log, d_skip, norm_w, w_out) = prm
    t, d = x.shape
    tp = bp * sp
    heads = a_log.shape[0]
    d_inner = heads * HEAD
    conv_dim = conv_w.shape[1]
    wz = w_in[:, :d_inner].astype(BF16)
    wx = w_in[:, d_inner:d_inner + conv_dim].astype(BF16)
    wdt = jnp.pad(w_in[:, d_inner + conv_dim:], ((0, 0), (0, LANES - heads))).astype(BF16)
    z, xbc, dt = _ssd_in(x, wz, wx, wdt, tm)
    lane_pad = lambda v: jnp.pad(v, (0, LANES - heads))[None, :]
    expand = (jnp.arange(LANES)[:, None] == (jnp.arange(d_inner) // HEAD)[None, :]).astype(F32)
    common = (conv_w, conv_b[None, :], lane_pad(dt_bias), lane_pad(a_log),
              jnp.repeat(d_skip, HEAD)[None, :], norm_w[None, :], expand)
    conv_tail = CONV_WIDTH - 1
    pad_conv = lambda cs: jnp.pad(cs, ((0, 0), (8 - conv_tail, 0), (0, 0)))
    y_p, conv_p, h_p = _ssd(xbc, dt, z, jnp.zeros((bp, 8, conv_dim), F32),
                            jnp.zeros((bp, d_inner, D_STATE), F32), *common,
                            batch=bp, seq=sp, row0=0, chunk=SSD_CHUNK if sp % SSD_CHUNK == 0 else sp)
    y_s, conv_s_out, h_s = _ssd(xbc, dt, z, pad_conv(conv_s), ssm_s.reshape(bs, d_inner, D_STATE), *common,
                                batch=bs, seq=ss, row0=tp, chunk=SSD_CHUNK if ss % SSD_CHUNK == 0 else ss)
    x1 = _proj_ln(y_p, y_s, None, (x, x, tp // tm), w_out.astype(BF16), lng, lnb, alpha=alpha, tm=tm)
    shape_h = lambda h, b: h.reshape(b, heads, HEAD, D_STATE)
    return (x1, conv_p[:, 8 - conv_tail:], conv_s_out[:, 8 - conv_tail:], shape_h(h_p, bp), shape_h(h_s, bs))


def kernel(x_prompt, x_sample, state_rwkv_wkv, state_rwkv_shift, state_ssd_ssm, state_ssd_conv,
           rwkv_mu, rwkv_w_rkv, rwkv_w0, rwkv_w1, rwkv_w2, rwkv_a0, rwkv_a1, rwkv_a2,
           rwkv_g1, rwkv_g2, rwkv_k_k, rwkv_k_a, rwkv_r_k, rwkv_gn_w, rwkv_gn_b, rwkv_w_o,
           ssd_w_in, ssd_conv_w, ssd_conv_b, ssd_dt_bias, ssd_a_log, ssd_d, ssd_norm_w, ssd_w_out,
           ln_gain, ln_bias, moe_w_rg, moe_b_rg, moe_w_re, moe_b_re, moe_w_gate, moe_w_up, moe_w_down):
    bp, sp, d = x_prompt.shape
    bs, ss, _ = x_sample.shape
    depth = ln_gain.shape[0]
    alpha = (2 * depth) ** 0.25
    tp, ts = bp * sp, bs * ss
    tm = 256
    tile_rows = 256
    x = None
    xp3, xs3 = x_prompt, x_sample
    wkv_p, wkv_s, sh_p, sh_s, ssm_p, ssm_s, cv_p, cv_s = [], [], [], [], [], [], [], []
    for i in range(depth):
        j = i // 2
        lng = lambda n: ln_gain[i, n][None, :]
        lnb = lambda n: ln_bias[i, n][None, :]
        if i % 2 == 0:
            if x is not None:
                xp3, xs3 = x[:tp].reshape(bp, sp, d), x[tp:].reshape(bs, ss, d)
            prm = (rwkv_mu[j], rwkv_w_rkv[j], rwkv_w0[j], rwkv_w1[j], rwkv_w2[j], rwkv_a0[j],
                   rwkv_a1[j], rwkv_a2[j], rwkv_g1[j], rwkv_g2[j], rwkv_k_k[j], rwkv_k_a[j],
                   rwkv_r_k[j].reshape(-1), rwkv_gn_w[j], rwkv_gn_b[j], rwkv_w_o[j])
            sh_p.append(xp3[:, -1])
            sh_s.append(xs3[:, -1])
            x, w_p, w_s = _rwkv_layer(xp3, xs3, state_rwkv_shift[j], state_rwkv_wkv[j], prm,
                                      lng(0), lnb(0), alpha=alpha, tm=tm)
            wkv_p.append(w_p)
            wkv_s.append(w_s)
        else:
            prm = (ssd_w_in[j], ssd_conv_w[j], ssd_conv_b[j], ssd_dt_bias[j], ssd_a_log[j],
                   ssd_d[j], ssd_norm_w[j], ssd_w_out[j])
            x, c_p, c_s, h_p, h_s = _ssd_layer(x, bp, sp, bs, ss, state_ssd_conv[j], state_ssd_ssm[j],
                                               prm, lng(0), lnb(0), alpha=alpha, tm=tm)
            cv_p.append(c_p)
            cv_s.append(c_s)
            ssm_p.append(h_p)
            ssm_s.append(h_s)
        x = _hier_moe_ln(x, moe_w_rg[i], moe_b_rg[i], moe_w_re[i], moe_b_re[i], moe_w_gate,
                         moe_w_up, moe_w_down, lng(1), lnb(1), alpha=alpha, tm=tm,
                         tile_rows=tile_rows, layer=i, split_rows=tp if i == depth - 1 else None)
    yp = x[0].reshape(bp, sp, d)
    ys = x[1].reshape(bs, ss, d)
    return (yp, ys, jnp.stack(wkv_p), jnp.stack(wkv_s), jnp.stack(sh_p), jnp.stack(sh_s),
            jnp.stack(ssm_p), jnp.stack(ssm_s), jnp.stack(cv_p), jnp.stack(cv_s))
```

```python
import functools
import math

import jax
import jax.numpy as jnp
from jax import lax
from jax.experimental import pallas as pl
from jax.experimental.pallas import tpu as pltpu

F32 = jnp.float32
BF16 = jnp.bfloat16
I32 = jnp.int32

LANES = 128
HEAD = 64
GN_EPS = HEAD * 1e-5
RMS_EPS = 1e-5
LN_EPS = 1e-5
WKV_CHUNK = 64
WKV_PAIRS_PER_STEP = 8
SSD_CHUNK = 128
SSD_GROUPS = 4
D_STATE = 128
CONV_WIDTH = 4
EXPERT_GROUPS = 4
EXPERTS_PER_GROUP = 8
VMEM_LIMIT = 56 * 1024 * 1024


def _params(sem):
    return pltpu.CompilerParams(dimension_semantics=sem, vmem_limit_bytes=VMEM_LIMIT)


def _mm(a, b):
    return jnp.dot(a.astype(BF16), b.astype(BF16), preferred_element_type=F32)


def _mm_nt(a, b):
    return lax.dot_general(a.astype(BF16), b.astype(BF16), (((1,), (1,)), ((), ())),
                           preferred_element_type=F32)


def _mm_tn(a, b):
    return lax.dot_general(a.astype(BF16), b.astype(BF16), (((0,), (0,)), ((), ())),
                           preferred_element_type=F32)


def _split3(x):
    hi = x.astype(BF16)
    r1 = x - hi.astype(F32)
    mid = r1.astype(BF16)
    lo = (r1 - mid.astype(F32)).astype(BF16)
    return hi, mid, lo


def _mm_sel_l(sel, x):
    k = x.shape[1]
    parts = jnp.concatenate([p.astype(F32) for p in _split3(x)], axis=1).astype(BF16)
    y = jnp.dot(sel.astype(BF16), parts, preferred_element_type=F32)
    return y[:, :k] + y[:, k:2 * k] + y[:, 2 * k:]


def _mm_sel_r(x, sel):
    n = x.shape[0]
    if n % 8:
        return sum(jnp.dot(p, sel.astype(BF16), preferred_element_type=F32) for p in _split3(x))
    parts = jnp.concatenate([p.astype(F32) for p in _split3(x)], axis=0).astype(BF16)
    y = jnp.dot(parts, sel.astype(BF16), preferred_element_type=F32)
    return y[:n] + y[n:2 * n] + y[2 * n:]


def _sigmoid(x):
    return 1.0 / (1.0 + jnp.exp(-x))


def _silu(x):
    return x * _sigmoid(x)


def _softplus(x):
    return jnp.maximum(x, 0.0) + jnp.log(1.0 + jnp.exp(-jnp.abs(x)))


def _layer_norm(x, g, b):
    mu = jnp.mean(x, axis=-1, keepdims=True)
    xc = x - mu
    var = jnp.mean(xc * xc, axis=-1, keepdims=True)
    return xc * lax.rsqrt(var + LN_EPS) * g + b


def _iota2(shape, dim):
    return lax.broadcasted_iota(I32, shape, dim)


def _part_specs(tm, width, prompt_tiles, sample_block0=0):
    return [pl.BlockSpec((tm, width), lambda i: (jnp.minimum(i, prompt_tiles - 1), 0)),
            pl.BlockSpec((tm, width), lambda i: (sample_block0 + jnp.maximum(i - prompt_tiles, 0), 0))]


def _rwkv_pre_kernel(xa_ref, xb_ref, tail_ref, xprev_s_ref, mu_ref, wrkv_ref, w0_ref, w1_ref, w2_ref,
                     a0_ref, a1_ref, a2_ref, g1_ref, g2_ref, r_ref, k_ref, v_ref, lw_ref, a_ref, g_ref,
                     *, prompt_tiles, seq):
    i = pl.program_id(0)
    tm = xa_ref.shape[0]
    is_prompt = i < prompt_tiles
    xa = xa_ref[...]
    x = jnp.where(is_prompt, xa, xb_ref[...])
    first = jnp.where((i * tm) % seq == 0, 0.0, tail_ref[7:8, :])
    shifted = jnp.where(_iota2(xa.shape, 0) == 0, first, pltpu.roll(xa, 1, axis=0))
    xx = jnp.where(is_prompt, shifted, xprev_s_ref[...]) - x

    def mix(j):
        return x + xx * mu_ref[j:j + 1, :]

    r_ref[...] = _mm(mix(0), wrkv_ref[0])
    k_ref[...] = _mm(mix(1), wrkv_ref[1])
    v_ref[...] = _mm(mix(2), wrkv_ref[2])
    ww = w0_ref[...] + _mm(jnp.tanh(_mm(mix(3), w1_ref[...])), w2_ref[...])
    w_raw = -_softplus(-ww) - 0.5
    lw_ref[...] = -jnp.exp(w_raw)
    a_ref[...] = _sigmoid(a0_ref[...] + _mm(_mm(mix(4), a1_ref[...]), a2_ref[...]))
    g_ref[...] = _mm(_sigmoid(_mm(mix(5), g1_ref[...])), g2_ref[...])


def _rwkv_pre(x_p, x_s, xprev_s, seq, mu, wrkv, w0, w1, w2, a0, a1, a2, g1, g2, tm):
    (tp, d), ts = x_p.shape, x_s.shape[0]
    assert seq % tm == 0 and ts % tm == 0 and tm % 8 == 0
    npt = tp // tm
    row = pl.BlockSpec((tm, d), lambda i: (i, 0))
    tail = pl.BlockSpec((8, d), lambda i: (jnp.maximum(jnp.minimum(i, npt - 1) * (tm // 8) - 1, 0), 0))

    def full(arr):
        nd = arr.ndim
        return pl.BlockSpec(arr.shape, lambda i, _n=nd: (0,) * _n)

    weights = (mu, wrkv, w0, w1, w2, a0, a1, a2, g1, g2)
    return pl.pallas_call(
        functools.partial(_rwkv_pre_kernel, prompt_tiles=npt, seq=seq),
        grid=((tp + ts) // tm,),
        in_specs=_part_specs(tm, d, npt) + [tail, _part_specs(tm, d, npt)[1]] + [full(w) for w in weights],
        out_specs=[row] * 6,
        out_shape=[jax.ShapeDtypeStruct((tp + ts, d), F32)] * 6,
        compiler_params=_params(("parallel",)),
        name="rwkv_pre",
    )(x_p, x_s, x_p, xprev_s, *weights)


def _wkv_kernel(r_ref, k_ref, v_ref, lw_ref, a_ref, kkw_ref, kaw_ref, rk_ref, gnw_ref, gnb_ref,
                *rest, chunk, pairs_per_step, has_init):
    if has_init:
        s0_ref, o_ref, sout_ref, s_sc = rest
    else:
        o_ref, sout_ref, s_sc = rest
    L = chunk
    c = pl.program_id(2)

    @pl.when(c == 0)
    def _():
        if has_init:
            zero = jnp.zeros((HEAD, HEAD), F32)
            for q in range(pairs_per_step):
                top = jnp.concatenate([s0_ref[2 * q], zero], axis=1)
                bot = jnp.concatenate([zero, s0_ref[2 * q + 1]], axis=1)
                s_sc[q] = jnp.concatenate([top, bot], axis=0)
        else:
            s_sc[...] = jnp.zeros_like(s_sc)

    lane = _iota2((1, LANES), 1)
    m0 = (lane < HEAD).astype(F32)
    m1 = 1.0 - m0
    same_head = (_iota2((LANES, LANES), 0) // HEAD == _iota2((LANES, LANES), 1) // HEAD).astype(F32)
    tril_incl = (_iota2((L, L), 1) <= _iota2((L, L), 0)).astype(F32)
    n2 = 2 * L
    row = _iota2((n2, n2), 0)
    col = _iota2((n2, n2), 1)
    strict = col < row
    incl = col <= row
    steps = int(math.log2(L))

    def headsum(t):
        return _mm_sel_r(t, same_head)

    def stack(t):
        return jnp.concatenate([t * m0, t * m1], axis=0)

    pairs = range(pairs_per_step)
    lanes = [slice(q * LANES, (q + 1) * LANES) for q in pairs]

    def prepare(ls):
        r = r_ref[:, ls]
        k = k_ref[:, ls]
        v = v_ref[:, ls]
        lw = lw_ref[:, ls]
        a = a_ref[:, ls]
        kk = k * kkw_ref[:, ls]
        k2 = k * (1.0 + (a - 1.0) * kaw_ref[:, ls])
        sums = headsum(jnp.concatenate([kk * kk, r * k2 * rk_ref[:, ls]], axis=0))
        kk = kk / jnp.maximum(jnp.sqrt(sums[:L]), 1e-12)
        bonus = sums[L:] * v
        bvec = kk * a
        cl = _mm_sel_l(tril_incl, lw)
        total = cl[L - 1:L, :]
        w_in = jnp.exp(cl)
        w_inv = jnp.exp(-cl)
        w_end = jnp.exp(total - cl)
        return dict(
            ar=jnp.concatenate([stack(-kk * jnp.exp(cl - lw)), stack(r * w_in)], axis=0).astype(BF16),
            bk=jnp.concatenate([stack(bvec * w_inv), stack(k2 * w_inv)], axis=0).astype(BF16),
            bk_end=jnp.concatenate([stack(bvec * w_end), stack(k2 * w_end)], axis=0),
            v_s=stack(v), total=total, bonus=bonus)

    ops = [prepare(ls) for ls in lanes]
    states = [s_sc[q] for q in pairs]
    cross = [_mm_nt(o["ar"], o["bk"]) for o in ops]
    from_s = [_mm_nt(o["ar"], s) for o, s in zip(ops, states)]
    n_ab = [jnp.where(strict, x[:n2, :n2], 0.0) for x in cross]
    a_rb = [jnp.where(incl, x[n2:, :n2], 0.0) for x in cross]
    to_v = [jnp.concatenate([jnp.where(strict, x[:n2, n2:], 0.0), jnp.where(incl, x[n2:, n2:], 0.0)], axis=0)
            for x in cross]
    from_v = [_mm(m, o["v_s"]) for m, o in zip(to_v, ops)]

    u = [fs[:n2] + fv[:n2] for fs, fv in zip(from_s, from_v)]
    p = n_ab
    for j in range(steps):
        if j + 1 < steps:
            both = [_mm(pi, jnp.concatenate([ui, pi], axis=1)) for ui, pi in zip(u, p)]
            u = [ui + b[:, :LANES] for ui, b in zip(u, both)]
            p = [b[:, LANES:] for b in both]
        else:
            u = [ui + _mm(pi, ui) for ui, pi in zip(u, p)]

    y_s = [fs[n2:] + fv[n2:] + _mm(mb, ui) for fs, fv, mb, ui in zip(from_s, from_v, a_rb, u)]
    s_new = [s * jnp.exp(o["total"]) + _mm_tn(jnp.concatenate([ui, o["v_s"]], axis=0), o["bk_end"])
             for o, s, ui in zip(ops, states, u)]
    for q in pairs:
        s_sc[q] = s_new[q]
        y = y_s[q][:L] + y_s[q][L:]
        ym = headsum(y) * (1.0 / HEAD)
        yc = y - ym
        yv = headsum(yc * yc) * (1.0 / HEAD)
        ls = lanes[q]
        o_ref[:, ls] = yc * lax.rsqrt(yv + GN_EPS) * gnw_ref[:, ls] + gnb_ref[:, ls] + ops[q]["bonus"]

    @pl.when(c == pl.num_programs(2) - 1)
    def _():
        for q in pairs:
            sout_ref[2 * q] = s_new[q][:HEAD, :HEAD]
            sout_ref[2 * q + 1] = s_new[q][HEAD:, HEAD:]


def _wkv(r, k, v, lw, a, kkw, kaw, rk, gnw, gnb, s0, *, batch, seq, row0, chunk, pairs_per_step):
    d = r.shape[1]
    pps = pairs_per_step
    width = pps * LANES
    nc = seq // chunk
    blk0 = row0 // chunk
    tok = pl.BlockSpec((chunk, width), lambda b, p, c: (blk0 + b * nc + c, p))
    out = pl.BlockSpec((chunk, width), lambda b, p, c: (b * nc + c, p))
    par = pl.BlockSpec((1, width), lambda b, p, c: (0, p))
    st = pl.BlockSpec((None, 2 * pps, HEAD, HEAD), lambda b, p, c: (b, p, 0, 0))
    has_init = s0 is not None
    return pl.pallas_call(
        functools.partial(_wkv_kernel, chunk=chunk, pairs_per_step=pps, has_init=has_init),
        grid=(batch, d // width, nc),
        in_specs=[tok] * 5 + [par] * 5 + ([st] if has_init else []),
        out_specs=[out, st],
        out_shape=[jax.ShapeDtypeStruct((batch * seq, d), F32),
                   jax.ShapeDtypeStruct((batch, d // HEAD, HEAD, HEAD), F32)],
        scratch_shapes=[pltpu.VMEM((pps, LANES, LANES), F32)],
        compiler_params=_params(("parallel", "parallel", "arbitrary")),
        name="wkv_l%d" % chunk,
    )(r, k, v, lw, a, kkw, kaw, rk, gnw, gnb, *([s0] if has_init else []))


def _proj_ln_kernel(op_ref, os_ref, g_ref, xa_ref, xb_ref, w_ref, lng_ref, lnb_ref, out_ref, *, alpha,
                    gated, prompt_tiles):
    is_prompt = pl.program_id(0) < prompt_tiles
    o = jnp.where(is_prompt, op_ref[...], os_ref[...])
    if gated:
        o = o * g_ref[...]
    y = _mm(o, w_ref[...])
    x = jnp.where(is_prompt, xa_ref[...], xb_ref[...])
    out_ref[...] = _layer_norm(alpha * x + y, lng_ref[...], lnb_ref[...])


def _proj_ln(o_p, o_s, g, x_parts, w, lng, lnb, *, alpha, tm):
    x_a, x_b, sample_block0 = x_parts
    d = x_a.shape[1]
    kin = o_p.shape[1]
    npt = o_p.shape[0] // tm
    t = o_p.shape[0] + o_s.shape[0]
    gated = g is not None
    g_spec = pl.BlockSpec((tm, kin), lambda i: (i, 0))
    if not gated:
        g, g_spec = lng, pl.BlockSpec((1, d), lambda i: (0, 0))
    return pl.pallas_call(
        functools.partial(_proj_ln_kernel, alpha=alpha, gated=gated, prompt_tiles=npt),
        grid=(t // tm,),
        in_specs=_part_specs(tm, kin, npt) + [g_spec] + _part_specs(tm, d, npt, sample_block0)
                 + [pl.BlockSpec(w.shape, lambda i: (0, 0)),
                    pl.BlockSpec((1, d), lambda i: (0, 0)),
                    pl.BlockSpec((1, d), lambda i: (0, 0))],
        out_specs=pl.BlockSpec((tm, d), lambda i: (i, 0)),
        out_shape=jax.ShapeDtypeStruct((t, d), F32),
        compiler_params=_params(("parallel",)),
        name="proj_ln",
    )(o_p, o_s, g, x_a, x_b, w, lng, lnb)


def _router_kernel(x_ref, w_ref, b_ref, route_ref, cnt_ref):
    ng, epg = EXPERT_GROUPS, EXPERTS_PER_GROUP
    x = x_ref[...]
    x_hi = x.astype(BF16)
    x_lo = (x - x_hi.astype(F32)).astype(BF16)
    by_hi = jnp.dot(x_hi, w_ref[...], preferred_element_type=F32)
    logits = (by_hi[:, :LANES] + by_hi[:, LANES:]
              + jnp.dot(x_lo, w_ref[:, :LANES], preferred_element_type=F32)
              + b_ref[...])
    lane = _iota2(logits.shape, 1)
    neg = jnp.float32(-jnp.inf)
    big = jnp.int32(1 << 20)

    is_g = lane < ng
    lg = jnp.where(is_g, logits, neg)
    mg = jnp.max(lg, axis=-1, keepdims=True)
    g_sel = jnp.min(jnp.where(lg == mg, lane, big), axis=-1, keepdims=True)
    zg = jnp.sum(jnp.where(is_g, jnp.exp(lg - mg), 0.0), axis=-1, keepdims=True)
    p_group = 1.0 / zg

    e_lane = lane - ng
    in_sel = (e_lane >= g_sel * epg) & (e_lane < (g_sel + 1) * epg)
    le = jnp.where(in_sel, logits, neg)
    m1 = jnp.max(le, axis=-1, keepdims=True)
    i1 = jnp.min(jnp.where(le == m1, lane, big), axis=-1, keepdims=True)
    ze = jnp.sum(jnp.where(in_sel, jnp.exp(le - m1), 0.0), axis=-1, keepdims=True)
    le2 = jnp.where(lane == i1, neg, le)
    m2 = jnp.max(le2, axis=-1, keepdims=True)
    i2 = jnp.min(jnp.where(le2 == m2, lane, big), axis=-1, keepdims=True)
    pe1 = 1.0 / ze
    pe2 = jnp.exp(m2 - m1) / ze
    den = pe1 + pe2
    c1 = pe1 / den * p_group
    c2 = pe2 / den * p_group
    e1 = i1 - ng
    e2 = i2 - ng
    route_ref[...] = jnp.where(lane == 0, e1.astype(F32),
                     jnp.where(lane == 1, e2.astype(F32),
                     jnp.where(lane == 2, c1, jnp.where(lane == 3, c2, 0.0))))

    @pl.when(pl.program_id(0) == 0)
    def _():
        cnt_ref[...] = jnp.zeros_like(cnt_ref)

    chosen = ((lane == e1) | (lane == e2)).astype(F32)
    cnt_ref[...] += jnp.sum(chosen, axis=0, keepdims=True)


def _router(x, w_terms, b, tm):
    t, d = x.shape
    return pl.pallas_call(
        _router_kernel,
        grid=(t // tm,),
        in_specs=[pl.BlockSpec((tm, d), lambda i: (i, 0)),
                  pl.BlockSpec((d, 2 * LANES), lambda i: (0, 0)),
                  pl.BlockSpec((1, LANES), lambda i: (0, 0))],
        out_specs=[pl.BlockSpec((tm, LANES), lambda i: (i, 0)),
                   pl.BlockSpec((1, LANES), lambda i: (0, 0))],
        out_shape=[jax.ShapeDtypeStruct((t, LANES), F32),
                   jax.ShapeDtypeStruct((1, LANES), F32)],
        compiler_params=_params(("arbitrary",)),
        name="moe_router",
    )(x, w_terms, b)


def _slots_kernel(route_ref, cnt_ref, pos_ref, ends_ref, off_sc, *, tile_rows):
    i = pl.program_id(0)
    route = route_ref[...]
    tm = route.shape[0]
    lane = _iota2((tm, LANES), 1)
    oh1 = lane == route[:, 0:1].astype(I32)
    oh2 = lane == route[:, 1:2].astype(I32)
    oh = (oh1 | oh2).astype(F32)

    @pl.when(i == 0)
    def _():
        padded = jnp.ceil(cnt_ref[...] * (1.0 / tile_rows)) * tile_rows
        before = (_iota2((LANES, LANES), 0) < _iota2((LANES, LANES), 1)).astype(F32)
        starts = _mm_sel_r(padded, before)
        off_sc[...] = starts
        ends_ref[...] = (starts + padded).astype(I32)

    earlier = (_iota2((tm, tm), 1) < _iota2((tm, tm), 0)).astype(F32)
    base = off_sc[...] + _mm(earlier, oh)
    p1 = jnp.sum(jnp.where(oh1, base, 0.0), axis=-1, keepdims=True)
    p2 = jnp.sum(jnp.where(oh2, base, 0.0), axis=-1, keepdims=True)
    pos_ref[...] = jnp.where(lane == 0, p1, jnp.where(lane == 1, p2, 0.0)).astype(I32)
    off_sc[...] += jnp.sum(oh, axis=0, keepdims=True)


def _slots(route, counts, tm, tile_rows):
    t = route.shape[0]
    return pl.pallas_call(
        functools.partial(_slots_kernel, tile_rows=tile_rows),
        grid=(t // tm,),
        in_specs=[pl.BlockSpec((tm, LANES), lambda i: (i, 0)),
                  pl.BlockSpec((1, LANES), lambda i: (0, 0))],
        out_specs=[pl.BlockSpec((tm, LANES), lambda i: (i, 0)),
                   pl.BlockSpec((1, LANES), lambda i: (0, 0))],
        out_shape=[jax.ShapeDtypeStruct((t, LANES), I32),
                   jax.ShapeDtypeStruct((1, LANES), I32)],
        scratch_shapes=[pltpu.VMEM((1, LANES), F32)],
        compiler_params=_params(("arbitrary",)),
        name="moe_slots",
    )(route, counts)


DMA_UNROLL = 8
COMBINE_CHUNKS = 8


def _scatter_kernel(pos_ref, te_ref, nu_ref, x_ref, xs_ref, zero_sc, sem, zsem, *, tile_rows):
    i = pl.program_id(0)
    tm = x_ref.shape[0]
    n_tiles = xs_ref.shape[0] // tile_rows

    @pl.when(i == 0)
    def _():
        zero_sc[...] = jnp.zeros_like(zero_sc)

        def zcopy(j):
            return pltpu.make_async_copy(zero_sc, xs_ref.at[pl.ds(j * tile_rows, tile_rows)], zsem)

        def fill(j, n):
            partial = (j >= nu_ref[0] - 1) | (te_ref[j] != te_ref[jnp.minimum(j + 1, n_tiles - 1)])

            @pl.when(partial)
            def _():
                zcopy(j).start()
            return n + partial.astype(I32)

        n_fill = lax.fori_loop(0, n_tiles, fill, 0)

        def drain(_, carry):
            zcopy(0).wait()
            return carry
        lax.fori_loop(0, n_fill, drain, 0)

    def copy(t, c):
        p = pos_ref[2 * (i * tm + t) + c]
        return pltpu.make_async_copy(x_ref.at[pl.ds(t, 1)], xs_ref.at[pl.ds(p, 1)], sem)

    def start(t, carry):
        copy(t, 0).start(priority=0)
        copy(t, 1).start(priority=1)
        return carry

    def wait(t, carry):
        copy(t, 0).wait()
        copy(t, 1).wait()
        return carry

    lax.fori_loop(0, tm, start, 0, unroll=DMA_UNROLL)
    lax.fori_loop(0, tm, wait, 0, unroll=DMA_UNROLL)


def _scatter(pos_flat, tile_expert, n_used, x, n_slots, tm, tile_rows):
    t, d = x.shape
    grid_spec = pltpu.PrefetchScalarGridSpec(
        num_scalar_prefetch=3,
        grid=(t // tm,),
        in_specs=[pl.BlockSpec((tm, d), lambda i, pos, te, nu: (i, 0))],
        out_specs=pl.BlockSpec(memory_space=pl.ANY),
        scratch_shapes=[pltpu.VMEM((tile_rows, d), F32), pltpu.SemaphoreType.DMA(()),
                        pltpu.SemaphoreType.DMA(())],
    )
    return pl.pallas_call(
        functools.partial(_scatter_kernel, tile_rows=tile_rows),
        grid_spec=grid_spec,
        out_shape=jax.ShapeDtypeStruct((n_slots, d), F32),
        compiler_params=_params(("arbitrary",)),
        name="moe_scatter",
    )(pos_flat, tile_expert, n_used, x)


def _experts_kernel(te_ref, nx_ref, nu_ref, xs_ref, wg_ref, wu_ref, wd_ref, out_ref, wg_sc, wu_sc, wd_sc,
                    wg_in, wu_in, wd_in, sem, slot_ref, *, expert0):
    j = pl.program_id(0)
    used = j < nu_ref[0]
    prev = te_ref[jnp.maximum(j - 1, 0)]

    def fetches(e, s):
        return [pltpu.make_async_copy(hbm.at[expert0 + e], buf.at[s], sem.at[s])
                for hbm, buf in ((wg_ref, wg_in), (wu_ref, wu_in), (wd_ref, wd_in))]

    @pl.when(used & (j == 0))
    def _():
        slot_ref[0] = 0
        for cp in fetches(te_ref[0], 0):
            cp.start()

    @pl.when(used & ((j == 0) | (te_ref[j] != prev)))
    def _():
        s = slot_ref[0]
        for cp in fetches(te_ref[j], s):
            cp.wait()
        wg_sc[...] = wg_in[s].astype(BF16)
        wu_sc[...] = wu_in[s].astype(BF16)
        wd_sc[...] = wd_in[s].astype(BF16)
        nxt = nx_ref[j]

        @pl.when(nxt >= 0)
        def _():
            for cp in fetches(nxt, 1 - s):
                cp.start()
        slot_ref[0] = 1 - s

    @pl.when(used)
    def _():
        xb = xs_ref[...].astype(BF16)
        h = _silu(jnp.dot(xb, wg_sc[...], preferred_element_type=F32))
        h = h * jnp.dot(xb, wu_sc[...], preferred_element_type=F32)
        out_ref[...] = jnp.dot(h.astype(BF16), wd_sc[...], preferred_element_type=F32)

    @pl.when(jnp.logical_not(used))
    def _():
        out_ref[...] = jnp.zeros_like(out_ref)


def _experts(tile_expert, next_expert, n_used, xs, wg, wu, wd, tile_rows, expert0):
    n_slots, d = xs.shape
    f = wg.shape[-1]

    def rows(j, te, nx, nu):
        return (jnp.maximum(jnp.minimum(j, nu[0] - 1), 0), 0)

    hbm = pl.BlockSpec(memory_space=pl.ANY)
    grid_spec = pltpu.PrefetchScalarGridSpec(
        num_scalar_prefetch=3,
        grid=(n_slots // tile_rows,),
        in_specs=[pl.BlockSpec((tile_rows, d), rows), hbm, hbm, hbm],
        out_specs=pl.BlockSpec((tile_rows, d), lambda j, te, nx, nu: (j, 0)),
        scratch_shapes=[pltpu.VMEM((d, f), BF16), pltpu.VMEM((d, f), BF16), pltpu.VMEM((f, d), BF16),
                        pltpu.VMEM((2, d, f), F32), pltpu.VMEM((2, d, f), F32), pltpu.VMEM((2, f, d), F32),
                        pltpu.SemaphoreType.DMA((2,)), pltpu.SMEM((1,), I32)],
    )
    return pl.pallas_call(
        functools.partial(_experts_kernel, expert0=expert0),
        grid_spec=grid_spec,
        out_shape=jax.ShapeDtypeStruct((n_slots, d), F32),
        compiler_params=_params(("arbitrary",)),
        name="moe_experts",
    )(tile_expert, next_expert, n_used, xs, wg, wu, wd)


def _combine_kernel(pos_ref, x_ref, route_ref, lng_ref, lnb_ref, ys_ref, *rest, alpha, split_tiles):
    if split_tiles is None:
        out_ref, o_sc, sem = rest
    else:
        out_ref, out_s_ref, o_sc, sem = rest
    i = pl.program_id(0)
    n = pl.num_programs(0)
    tm = x_ref.shape[0]
    slot = i % 2
    spare = 1 - slot
    nxt = jnp.minimum(i + 1, n - 1)
    rows_per_chunk = tm // COMBINE_CHUNKS

    def copy(tile, sl, t, c):
        p = pos_ref[2 * (tile * tm + t) + c]
        return pltpu.make_async_copy(ys_ref.at[pl.ds(p, 1)], o_sc.at[sl, c, pl.ds(t, 1)], sem.at[sl])

    def wait_tile(tile, sl):
        def body(t, carry):
            copy(tile, sl, t, 0).wait()
            copy(tile, sl, t, 1).wait()
            return carry
        lax.fori_loop(0, tm, body, 0, unroll=DMA_UNROLL)

    @pl.when(i == 0)
    def _():
        def body(t, carry):
            copy(0, 0, t, 0).start(priority=0)
            copy(0, 0, t, 1).start(priority=1)
            return carry
        lax.fori_loop(0, tm, body, 0, unroll=DMA_UNROLL)

    wait_tile(i, slot)

    def mix_into(out):
        for k in range(COMBINE_CHUNKS):
            rows = slice(k * rows_per_chunk, (k + 1) * rows_per_chunk)
            for t in range(rows.start, rows.stop):
                copy(nxt, spare, t, 0).start(priority=0)
                copy(nxt, spare, t, 1).start(priority=1)
            y = route_ref[rows, 2:3] * o_sc[slot, 0, rows] + route_ref[rows, 3:4] * o_sc[slot, 1, rows]
            out[rows, :] = _layer_norm(alpha * x_ref[rows, :] + y, lng_ref[...], lnb_ref[...])

    if split_tiles is None:
        mix_into(out_ref)
    else:
        @pl.when(i < split_tiles)
        def _():
            mix_into(out_ref)

        @pl.when(i >= split_tiles)
        def _():
            mix_into(out_s_ref)

    @pl.when(i == n - 1)
    def _():
        wait_tile(nxt, spare)


def _combine(pos_flat, x, route, lng, lnb, ys, *, alpha, tm, split_rows=None):
    t, d = x.shape
    out_specs = pl.BlockSpec((tm, d), lambda i, pos: (i, 0))
    out_shape = jax.ShapeDtypeStruct((t, d), F32)
    npt = None
    if split_rows is not None:
        npt = split_rows // tm
        out_specs = [pl.BlockSpec((tm, d), lambda i, pos: (jnp.minimum(i, npt - 1), 0)),
                     pl.BlockSpec((tm, d), lambda i, pos: (jnp.maximum(i - npt, 0), 0))]
        out_shape = [jax.ShapeDtypeStruct((split_rows, d), F32),
                     jax.ShapeDtypeStruct((t - split_rows, d), F32)]
    grid_spec = pltpu.PrefetchScalarGridSpec(
        num_scalar_prefetch=1,
        grid=(t // tm,),
        in_specs=[pl.BlockSpec((tm, d), lambda i, pos: (i, 0)),
                  pl.BlockSpec((tm, LANES), lambda i, pos: (i, 0)),
                  pl.BlockSpec((1, d), lambda i, pos: (0, 0)),
                  pl.BlockSpec((1, d), lambda i, pos: (0, 0)),
                  pl.BlockSpec(memory_space=pl.ANY)],
        out_specs=out_specs,
        scratch_shapes=[pltpu.VMEM((2, 2, tm, d), F32), pltpu.SemaphoreType.DMA((2,))],
    )
    return pl.pallas_call(
        functools.partial(_combine_kernel, alpha=alpha, split_tiles=npt),
        grid_spec=grid_spec,
        out_shape=out_shape,
        compiler_params=_params(("arbitrary",)),
        name="moe_combine",
    )(pos_flat, x, route, lng, lnb, ys)


def _hier_moe_ln(x, w_rg, b_rg, w_re, b_re, w_gate, w_up, w_down, lng, lnb, *, alpha, tm, tile_rows,
                 layer=0, split_rows=None):
    t, d = x.shape
    ng, epg = EXPERT_GROUPS, EXPERTS_PER_GROUP
    n_exp = ng * epg
    pad = LANES - ng - n_exp
    w_router = jnp.pad(jnp.concatenate([w_rg, w_re], axis=1), ((0, 0), (0, pad)))
    b_router = jnp.pad(jnp.concatenate([b_rg, b_re]), (0, pad))[None, :]
    w_hi = w_router.astype(BF16)
    w_lo = (w_router - w_hi.astype(F32)).astype(BF16)
    route, counts = _router(x, jnp.concatenate([w_hi, w_lo], axis=1), b_router, tm)
    pos, ends = _slots(route, counts, tm, tile_rows)
    pos_flat = pos[:, :2].reshape(-1)
    n_slots = 2 * t + n_exp * tile_rows
    n_tiles = n_slots // tile_rows
    seg_end = ends[0, :n_exp]
    tile_start = jnp.arange(n_tiles, dtype=I32) * tile_rows
    tile_expert = jnp.minimum(jnp.sum(tile_start[:, None] >= seg_end[None, :], axis=1), n_exp - 1).astype(I32)
    n_used = (seg_end[n_exp - 1:] // tile_rows).astype(I32)
    after = seg_end[tile_expert] // tile_rows
    next_expert = jnp.where(after < n_used[0], tile_expert[jnp.minimum(after, n_tiles - 1)], -1).astype(I32)
    xs = _scatter(pos_flat, tile_expert, n_used, x, n_slots, tm, tile_rows)
    f = w_gate.shape[-1]
    ys = _experts(tile_expert, next_expert, n_used, xs, w_gate.reshape(-1, d, f), w_up.reshape(-1, d, f),
                  w_down.reshape(-1, f, d), tile_rows, layer * n_exp)
    return _combine(pos_flat, x, route, lng, lnb, ys, alpha=alpha, tm=tm, split_rows=split_rows)


def _ssd_in_kernel(x_ref, wz_ref, wx_ref, wdt_ref, z_ref, xbc_ref, dt_ref):
    xb = x_ref[...].astype(BF16)
    z_ref[...] = jnp.dot(xb, wz_ref[...], preferred_element_type=F32)
    xbc_ref[...] = jnp.dot(xb, wx_ref[...], preferred_element_type=F32)
    dt_ref[...] = jnp.dot(xb, wdt_ref[...], preferred_element_type=F32)


def _ssd_in(x, wz, wx, wdt, tm):
    t, d = x.shape
    outs = [wz.shape[1], wx.shape[1], wdt.shape[1]]
    return pl.pallas_call(
        _ssd_in_kernel,
        grid=(t // tm,),
        in_specs=[pl.BlockSpec((tm, d), lambda i: (i, 0))]
                 + [pl.BlockSpec(w.shape, lambda i: (0, 0)) for w in (wz, wx, wdt)],
        out_specs=[pl.BlockSpec((tm, n), lambda i: (i, 0)) for n in outs],
        out_shape=[jax.ShapeDtypeStruct((t, n), F32) for n in outs],
        compiler_params=_params(("parallel",)),
        name="ssd_in",
    )(x, wz, wx, wdt)


def _ssd_kernel(xbc_ref, dt_ref, z_ref, conv0_ref, h0_ref, cw_ref, cb_ref, dtb_ref, alog_ref,
                dexp_ref, nw_ref, expand_ref, y_ref, convo_ref, hout_ref, pad_sc, ht_sc, *, chunk):
    c = chunk
    ci = pl.program_id(1)
    d_inner = z_ref.shape[1]
    gn = SSD_GROUPS * D_STATE
    per_group = d_inner // SSD_GROUPS
    tail = 8

    @pl.when(ci == 0)
    def _():
        pad_sc[0:tail, :] = conv0_ref[...]
        ht_sc[...] = h0_ref[...].T

    pad_sc[tail:tail + c, :] = xbc_ref[...]
    acc = cb_ref[...]
    for i in range(CONV_WIDTH):
        off = tail - (CONV_WIDTH - 1) + i
        acc = acc + pad_sc[off:off + c, :] * cw_ref[i:i + 1, :]
    xbc = _silu(acc)
    last_rows = pad_sc[c:c + tail, :]
    pad_sc[0:tail, :] = last_rows
    xs = xbc[:, :d_inner]

    dt = _softplus(dt_ref[...] + dtb_ref[...])
    da = dt * (-jnp.exp(alog_ref[...]))
    tril_incl = (_iota2((c, c), 1) <= _iota2((c, c), 0)).astype(F32)
    acum = _mm_sel_l(tril_incl, da)
    a_last = acum[c - 1:c, :]

    def to_cols(t):
        if c < LANES:
            t = jnp.concatenate([t, jnp.zeros((LANES - c, LANES), F32)], axis=0)
        return t.T[:, :c]

    acum_t = to_cols(acum)
    expand = expand_ref[...]
    per_head = jnp.concatenate([jnp.exp(acum), jnp.exp(a_last - acum) * dt, dt], axis=0)
    per_chan = _mm_sel_r(per_head, expand)
    ea, dd = per_chan[:c], per_chan[c:2 * c]
    x_dt = xs * per_chan[2 * c:]
    ea_last = _mm_sel_r(jnp.exp(a_last), expand)
    causal = _iota2((c, c), 1) <= _iota2((c, c), 0)
    lane = _iota2((1, LANES), 1)
    m0 = (lane < HEAD).astype(F32)
    m1 = 1.0 - m0

    ys = []
    for g in range(SSD_GROUPS):
        b_g = xbc[:, d_inner + g * D_STATE:d_inner + (g + 1) * D_STATE]
        c_g = xbc[:, d_inner + gn + g * D_STATE:d_inner + gn + (g + 1) * D_STATE]
        cb = _mm_nt(c_g, b_g)
        gsl = slice(g * per_group, (g + 1) * per_group)
        ht_g = ht_sc[:, gsl]
        y_off = _mm(c_g, ht_g) * ea[:, gsl]
        ht_sc[:, gsl] = ht_g * ea_last[:, gsl] + _mm_tn(b_g, xs[:, gsl] * dd[:, gsl])
        pieces = []
        for q in range(per_group // LANES):
            j0 = (g * per_group) // HEAD + 2 * q
            ms = []
            for j in (j0, j0 + 1):
                seg = acum[:, j:j + 1] - acum_t[j:j + 1, :]
                ms.append(jnp.where(causal, cb * jnp.exp(seg), 0.0))
            lo = g * per_group + q * LANES
            x_pair = x_dt[:, lo:lo + LANES]
            x_stack = jnp.concatenate([x_pair * m0, x_pair * m1], axis=0)
            pieces.append(_mm(jnp.concatenate(ms, axis=1), x_stack))
        y_g = jnp.concatenate(pieces, axis=1) + y_off + xs[:, gsl] * dexp_ref[:, gsl]
        y_g = y_g * _silu(z_ref[:, gsl])
        y_g = y_g * lax.rsqrt(jnp.mean(y_g * y_g, axis=-1, keepdims=True) + RMS_EPS)
        ys.append(y_g * nw_ref[:, gsl])
    y_ref[...] = jnp.concatenate(ys, axis=1)

    @pl.when(ci == pl.num_programs(1) - 1)
    def _():
        convo_ref[...] = last_rows
        hout_ref[...] = ht_sc[...].T


def _ssd(xbc, dt, z, conv0, h0, cw, cb, dtb, alog, dexp, nw, expand, *, batch, seq, row0, chunk):
    d_inner = z.shape[1]
    conv_dim = xbc.shape[1]
    nc = seq // chunk
    blk0 = row0 // chunk

    def tok(n):
        return pl.BlockSpec((chunk, n), lambda b, c: (blk0 + b * nc + c, 0))

    def par(arr):
        return pl.BlockSpec(arr.shape, lambda b, c: (0, 0))

    return pl.pallas_call(
        functools.partial(_ssd_kernel, chunk=chunk),
        grid=(batch, nc),
        in_specs=[tok(conv_dim), tok(LANES), tok(d_inner),
                  pl.BlockSpec((None, 8, conv_dim), lambda b, c: (b, 0, 0)),
                  pl.BlockSpec((None, d_inner, D_STATE), lambda b, c: (b, 0, 0)),
                  par(cw), par(cb), par(dtb), par(alog), par(dexp), par(nw), par(expand)],
        out_specs=[pl.BlockSpec((chunk, d_inner), lambda b, c: (b * nc + c, 0)),
                   pl.BlockSpec((None, 8, conv_dim), lambda b, c: (b, 0, 0)),
                   pl.BlockSpec((None, d_inner, D_STATE), lambda b, c: (b, 0, 0))],
        out_shape=[jax.ShapeDtypeStruct((batch * seq, d_inner), F32),
                   jax.ShapeDtypeStruct((batch, 8, conv_dim), F32),
                   jax.ShapeDtypeStruct((batch, d_inner, D_STATE), F32)],
        scratch_shapes=[pltpu.VMEM((chunk + 8, conv_dim), F32), pltpu.VMEM((D_STATE, d_inner), F32)],
        compiler_params=_params(("parallel", "arbitrary")),
        name="ssd_l%d" % chunk,
    )(xbc, dt, z, conv0, h0, cw, cb, dtb, alog, dexp, nw, expand)


def _rwkv_layer(xp3, xs3, shift_s, wkv_s, prm, lng, lnb, *, alpha, tm):
    (mu, w_rkv, w0, w1, w2, a0, a1, a2, g1, g2, k_k, k_a, r_k, gn_w, gn_b, w_o) = prm
    bp, sp, d = xp3.shape
    bs, ss, _ = xs3.shape
    tp = bp * sp
    heads = d // HEAD
    x_p = xp3.reshape(tp, d)
    x_s = xs3.reshape(bs * ss, d)
    prev_s = jnp.concatenate([shift_s[:, None, :], xs3[:, :-1]], axis=1).reshape(bs * ss, d)
    row = lambda v: v.reshape(1, d)
    r, k, v, lw, a, g = _rwkv_pre(x_p, x_s, prev_s, sp, mu, w_rkv.astype(BF16), row(w0), w1.astype(BF16),
                                  w2.astype(BF16), row(a0), a1.astype(BF16), a2.astype(BF16),
                                  g1.astype(BF16), g2.astype(BF16), tm)
    pv = (row(k_k), row(k_a), row(r_k), row(gn_w), row(gn_b))
    o_p, sp_out = _wkv(r, k, v, lw, a, *pv, None, batch=bp, seq=sp, row0=0,
                       chunk=min(WKV_CHUNK, sp), pairs_per_step=min(WKV_PAIRS_PER_STEP, heads // 2))
    o_s, ss_out = _wkv(r, k, v, lw, a, *pv, wkv_s, batch=bs, seq=ss, row0=tp,
                       chunk=min(WKV_CHUNK, ss), pairs_per_step=heads // 2)
    x1 = _proj_ln(o_p, o_s, g, (x_p, x_s, 0), w_o.astype(BF16), lng, lnb, alpha=alpha, tm=tm)
    return x1, sp_out, ss_out


def _ssd_layer(x, bp, sp, bs, ss, conv_s, ssm_s, prm, lng, lnb, *, alpha, tm):
    (w_in, conv_w, conv_b, dt_bias, a_log, d_skip, norm_w, w_out) = prm
    t, d = x.shape
    tp = bp * sp
    heads = a_log.shape[0]
    d_inner = heads * HEAD
    conv_dim = conv_w.shape[1]
    wz = w_in[:, :d_inner].astype(BF16)
    wx = w_in[:, d_inner:d_inner + conv_dim].astype(BF16)
    wdt = jnp.pad(w_in[:, d_inner + conv_dim:], ((0, 0), (0, LANES - heads))).astype(BF16)
    z, xbc, dt = _ssd_in(x, wz, wx, wdt, tm)
    lane_pad = lambda v: jnp.pad(v, (0, LANES - heads))[None, :]
    expand = (jnp.arange(LANES)[:, None] == (jnp.arange(d_inner) // HEAD)[None, :]).astype(F32)
    common = (conv_w, conv_b[None, :], lane_pad(dt_bias), lane_pad(a_log),
              jnp.repeat(d_skip, HEAD)[None, :], norm_w[None, :], expand)
    conv_tail = CONV_WIDTH - 1
    pad_conv = lambda cs: jnp.pad(cs, ((0, 0), (8 - conv_tail, 0), (0, 0)))
    y_p, conv_p, h_p = _ssd(xbc, dt, z, jnp.zeros((bp, 8, conv_dim), F32),
                            jnp.zeros((bp, d_inner, D_STATE), F32), *common,
                            batch=bp, seq=sp, row0=0, chunk=SSD_CHUNK if sp % SSD_CHUNK == 0 else sp)
    y_s, conv_s_out, h_s = _ssd(xbc, dt, z, pad_conv(conv_s), ssm_s.reshape(bs, d_inner, D_STATE), *common,
                                batch=bs, seq=ss, row0=tp, chunk=SSD_CHUNK if ss % SSD_CHUNK == 0 else ss)
    x1 = _proj_ln(y_p, y_s, None, (x, x, tp // tm), w_out.astype(BF16), lng, lnb, alpha=alpha, tm=tm)
    shape_h = lambda h, b: h.reshape(b, heads, HEAD, D_STATE)
    return (x1, conv_p[:, 8 - conv_tail:], conv_s_out[:, 8 - conv_tail:], shape_h(h_p, bp), shape_h(h_s, bs))


def kernel(x_prompt, x_sample, state_rwkv_wkv, state_rwkv_shift, state_ssd_ssm, state_ssd_conv,
           rwkv_mu, rwkv_w_rkv, rwkv_w0, rwkv_w1, rwkv_w2, rwkv_a0, rwkv_a1, rwkv_a2,
           rwkv_g1, rwkv_g2, rwkv_k_k, rwkv_k_a, rwkv_r_k, rwkv_gn_w, rwkv_gn_b, rwkv_w_o,
           ssd_w_in, ssd_conv_w, ssd_conv_b, ssd_dt_bias, ssd_a_log, ssd_d, ssd_norm_w, ssd_w_out,
           ln_gain, ln_bias, moe_w_rg, moe_b_rg, moe_w_re, moe_b_re, moe_w_gate, moe_w_up, moe_w_down):
    bp, sp, d = x_prompt.shape
    bs, ss, _ = x_sample.shape
    depth = ln_gain.shape[0]
    alpha = (2 * depth) ** 0.25
    tp, ts = bp * sp, bs * ss
    tm = 256
    tile_rows = 256
    x = None
    xp3, xs3 = x_prompt, x_sample
    wkv_p, wkv_s, sh_p, sh_s, ssm_p, ssm_s, cv_p, cv_s = [], [], [], [], [], [], [], []
    for i in range(depth):
        j = i // 2
        lng = lambda n: ln_gain[i, n][None, :]
        lnb = lambda n: ln_bias[i, n][None, :]
        if i % 2 == 0:
            if x is not None:
                xp3, xs3 = x[:tp].reshape(bp, sp, d), x[tp:].reshape(bs, ss, d)
            prm = (rwkv_mu[j], rwkv_w_rkv[j], rwkv_w0[j], rwkv_w1[j], rwkv_w2[j], rwkv_a0[j],
                   rwkv_a1[j], rwkv_a2[j], rwkv_g1[j], rwkv_g2[j], rwkv_k_k[j], rwkv_k_a[j],
                   rwkv_r_k[j].reshape(-1), rwkv_gn_w[j], rwkv_gn_b[j], rwkv_w_o[j])
            sh_p.append(xp3[:, -1])
            sh_s.append(xs3[:, -1])
            x, w_p, w_s = _rwkv_layer(xp3, xs3, state_rwkv_shift[j], state_rwkv_wkv[j], prm,
                                      lng(0), lnb(0), alpha=alpha, tm=tm)
            wkv_p.append(w_p)
            wkv_s.append(w_s)
        else:
            prm = (ssd_w_in[j], ssd_conv_w[j], ssd_conv_b[j], ssd_dt_bias[j], ssd_a_log[j],
                   ssd_d[j], ssd_norm_w[j], ssd_w_out[j])
            x, c_p, c_s, h_p, h_s = _ssd_layer(x, bp, sp, bs, ss, state_ssd_conv[j], state_ssd_ssm[j],
                                               prm, lng(0), lnb(0), alpha=alpha, tm=tm)
            cv_p.append(c_p)
            cv_s.append(c_s)
            ssm_p.append(h_p)
            ssm_s.append(h_s)
        x = _hier_moe_ln(x, moe_w_rg[i], moe_b_rg[i], moe_w_re[i], moe_b_re[i], moe_w_gate,
                         moe_w_up, moe_w_down, lng(1), lnb(1), alpha=alpha, tm=tm,
                         tile_rows=tile_rows, layer=i, split_rows=tp if i == depth - 1 else None)
    yp = x[0].reshape(bp, sp, d)
    ys = x[1].reshape(bs, ss, d)
    return (yp, ys, jnp.stack(wkv_p), jnp.stack(wkv_s), jnp.stack(sh_p), jnp.stack(sh_s),
            jnp.stack(ssm_p), jnp.stack(ssm_s), jnp.stack(cv_p), jnp.stack(cv_s))
```

```python
import functools
import math

import jax
import jax.numpy as jnp
from jax import lax
from jax.experimental import pallas as pl
from jax.experimental.pallas import tpu as pltpu

F32 = jnp.float32
BF16 = jnp.bfloat16
I32 = jnp.int32

LANES = 128
HEAD = 64
GN_EPS = HEAD * 1e-5
RMS_EPS = 1e-5
LN_EPS = 1e-5
WKV_CHUNK = 64
WKV_PAIRS_PER_STEP = 8
SSD_CHUNK = 128
SSD_GROUPS = 4
D_STATE = 128
CONV_WIDTH = 4
EXPERT_GROUPS = 4
EXPERTS_PER_GROUP = 8
VMEM_LIMIT = 56 * 1024 * 1024


def _params(sem):
    return pltpu.CompilerParams(dimension_semantics=sem, vmem_limit_bytes=VMEM_LIMIT)


def _mm(a, b):
    return jnp.dot(a.astype(BF16), b.astype(BF16), preferred_element_type=F32)


def _mm_nt(a, b):
    return lax.dot_general(a.astype(BF16), b.astype(BF16), (((1,), (1,)), ((), ())),
                           preferred_element_type=F32)


def _mm_tn(a, b):
    return lax.dot_general(a.astype(BF16), b.astype(BF16), (((0,), (0,)), ((), ())),
                           preferred_element_type=F32)


def _split3(x):
    hi = x.astype(BF16)
    r1 = x - hi.astype(F32)
    mid = r1.astype(BF16)
    lo = (r1 - mid.astype(F32)).astype(BF16)
    return hi, mid, lo


def _mm_sel_l(sel, x):
    k = x.shape[1]
    parts = jnp.concatenate([p.astype(F32) for p in _split3(x)], axis=1).astype(BF16)
    y = jnp.dot(sel.astype(BF16), parts, preferred_element_type=F32)
    return y[:, :k] + y[:, k:2 * k] + y[:, 2 * k:]


def _mm_sel_r(x, sel):
    n = x.shape[0]
    if n % 8:
        return sum(jnp.dot(p, sel.astype(BF16), preferred_element_type=F32) for p in _split3(x))
    parts = jnp.concatenate([p.astype(F32) for p in _split3(x)], axis=0).astype(BF16)
    y = jnp.dot(parts, sel.astype(BF16), preferred_element_type=F32)
    return y[:n] + y[n:2 * n] + y[2 * n:]


def _sigmoid(x):
    return 1.0 / (1.0 + jnp.exp(-x))


def _silu(x):
    return x * _sigmoid(x)


def _softplus(x):
    return jnp.maximum(x, 0.0) + jnp.log(1.0 + jnp.exp(-jnp.abs(x)))


def _layer_norm(x, g, b):
    mu = jnp.mean(x, axis=-1, keepdims=True)
    xc = x - mu
    var = jnp.mean(xc * xc, axis=-1, keepdims=True)
    return xc * lax.rsqrt(var + LN_EPS) * g + b


def _iota2(shape, dim):
    return lax.broadcasted_iota(I32, shape, dim)


def _part_specs(tm, width, prompt_tiles, sample_block0=0):
    return [pl.BlockSpec((tm, width), lambda i: (jnp.minimum(i, prompt_tiles - 1), 0)),
            pl.BlockSpec((tm, width), lambda i: (sample_block0 + jnp.maximum(i - prompt_tiles, 0), 0))]


def _rwkv_pre_kernel(xa_ref, xb_ref, tail_ref, xprev_s_ref, mu_ref, wrkv_ref, w0_ref, w1_ref, w2_ref,
                     a0_ref, a1_ref, a2_ref, g1_ref, g2_ref, r_ref, k_ref, v_ref, lw_ref, a_ref, g_ref,
                     *, prompt_tiles, seq):
    i = pl.program_id(0)
    tm = xa_ref.shape[0]
    is_prompt = i < prompt_tiles
    xa = xa_ref[...]
    x = jnp.where(is_prompt, xa, xb_ref[...])
    first = jnp.where((i * tm) % seq == 0, 0.0, tail_ref[7:8, :])
    shifted = jnp.where(_iota2(xa.shape, 0) == 0, first, pltpu.roll(xa, 1, axis=0))
    xx = jnp.where(is_prompt, shifted, xprev_s_ref[...]) - x

    def mix(j):
        return x + xx * mu_ref[j:j + 1, :]

    r_ref[...] = _mm(mix(0), wrkv_ref[0])
    k_ref[...] = _mm(mix(1), wrkv_ref[1])
    v_ref[...] = _mm(mix(2), wrkv_ref[2])
    ww = w0_ref[...] + _mm(jnp.tanh(_mm(mix(3), w1_ref[...])), w2_ref[...])
    w_raw = -_softplus(-ww) - 0.5
    lw_ref[...] = -jnp.exp(w_raw)
    a_ref[...] = _sigmoid(a0_ref[...] + _mm(_mm(mix(4), a1_ref[...]), a2_ref[...]))
    g_ref[...] = _mm(_sigmoid(_mm(mix(5), g1_ref[...])), g2_ref[...])


def _rwkv_pre(x_p, x_s, xprev_s, seq, mu, wrkv, w0, w1, w2, a0, a1, a2, g1, g2, tm):
    (tp, d), ts = x_p.shape, x_s.shape[0]
    assert seq % tm == 0 and ts % tm == 0 and tm % 8 == 0
    npt = tp // tm
    row = pl.BlockSpec((tm, d), lambda i: (i, 0))
    tail = pl.BlockSpec((8, d), lambda i: (jnp.maximum(jnp.minimum(i, npt - 1) * (tm // 8) - 1, 0), 0))

    def full(arr):
        nd = arr.ndim
        return pl.BlockSpec(arr.shape, lambda i, _n=nd: (0,) * _n)

    weights = (mu, wrkv, w0, w1, w2, a0, a1, a2, g1, g2)
    return pl.pallas_call(
        functools.partial(_rwkv_pre_kernel, prompt_tiles=npt, seq=seq),
        grid=((tp + ts) // tm,),
        in_specs=_part_specs(tm, d, npt) + [tail, _part_specs(tm, d, npt)[1]] + [full(w) for w in weights],
        out_specs=[row] * 6,
        out_shape=[jax.ShapeDtypeStruct((tp + ts, d), F32)] * 6,
        compiler_params=_params(("parallel",)),
        name="rwkv_pre",
    )(x_p, x_s, x_p, xprev_s, *weights)


def _wkv_kernel(r_ref, k_ref, v_ref, lw_ref, a_ref, kkw_ref, kaw_ref, rk_ref, gnw_ref, gnb_ref,
                *rest, chunk, pairs_per_step, has_init):
    if has_init:
        s0_ref, o_ref, sout_ref, s_sc = rest
    else:
        o_ref, sout_ref, s_sc = rest
    L = chunk
    c = pl.program_id(2)

    @pl.when(c == 0)
    def _():
        if has_init:
            zero = jnp.zeros((HEAD, HEAD), F32)
            for q in range(pairs_per_step):
                top = jnp.concatenate([s0_ref[2 * q], zero], axis=1)
                bot = jnp.concatenate([zero, s0_ref[2 * q + 1]], axis=1)
                s_sc[q] = jnp.concatenate([top, bot], axis=0)
        else:
            s_sc[...] = jnp.zeros_like(s_sc)

    lane = _iota2((1, LANES), 1)
    m0 = (lane < HEAD).astype(F32)
    m1 = 1.0 - m0
    same_head = (_iota2((LANES, LANES), 0) // HEAD == _iota2((LANES, LANES), 1) // HEAD).astype(F32)
    tril_incl = (_iota2((L, L), 1) <= _iota2((L, L), 0)).astype(F32)
    n2 = 2 * L
    row = _iota2((n2, n2), 0)
    col = _iota2((n2, n2), 1)
    strict = col < row
    incl = col <= row
    steps = int(math.log2(L))

    def headsum(t):
        return _mm_sel_r(t, same_head)

    def stack(t):
        return jnp.concatenate([t * m0, t * m1], axis=0)

    pairs = range(pairs_per_step)
    lanes = [slice(q * LANES, (q + 1) * LANES) for q in pairs]

    def prepare(ls):
        r = r_ref[:, ls]
        k = k_ref[:, ls]
        v = v_ref[:, ls]
        lw = lw_ref[:, ls]
        a = a_ref[:, ls]
        kk = k * kkw_ref[:, ls]
        k2 = k * (1.0 + (a - 1.0) * kaw_ref[:, ls])
        sums = headsum(jnp.concatenate([kk * kk, r * k2 * rk_ref[:, ls]], axis=0))
        kk = kk / jnp.maximum(jnp.sqrt(sums[:L]), 1e-12)
        bonus = sums[L:] * v
        bvec = kk * a
        cl = _mm_sel_l(tril_incl, lw)
        total = cl[L - 1:L, :]
        w_in = jnp.exp(cl)
        w_inv = jnp.exp(-cl)
        w_end = jnp.exp(total - cl)
        return dict(
            ar=jnp.concatenate([stack(-kk * jnp.exp(cl - lw)), stack(r * w_in)], axis=0).astype(BF16),
            bk=jnp.concatenate([stack(bvec * w_inv), stack(k2 * w_inv)], axis=0).astype(BF16),
            bk_end=jnp.concatenate([stack(bvec * w_end), stack(k2 * w_end)], axis=0),
            v_s=stack(v), total=total, bonus=bonus)

    ops = [prepare(ls) for ls in lanes]
    states = [s_sc[q] for q in pairs]
    cross = [_mm_nt(o["ar"], o["bk"]) for o in ops]
    from_s = [_mm_nt(o["ar"], s) for o, s in zip(ops, states)]
    n_ab = [jnp.where(strict, x[:n2, :n2], 0.0) for x in cross]
    a_rb = [jnp.where(incl, x[n2:, :n2], 0.0) for x in cross]
    to_v = [jnp.concatenate([jnp.where(strict, x[:n2, n2:], 0.0), jnp.where(incl, x[n2:, n2:], 0.0)], axis=0)
            for x in cross]
    from_v = [_mm(m, o["v_s"]) for m, o in zip(to_v, ops)]

    u = [fs[:n2] + fv[:n2] for fs, fv in zip(from_s, from_v)]
    p = n_ab
    for j in range(steps):
        if j + 1 < steps:
            both = [_mm(pi, jnp.concatenate([ui, pi], axis=1)) for ui, pi in zip(u, p)]
            u = [ui + b[:, :LANES] for ui, b in zip(u, both)]
            p = [b[:, LANES:] for b in both]
        else:
            u = [ui + _mm(pi, ui) for ui, pi in zip(u, p)]

    y_s = [fs[n2:] + fv[n2:] + _mm(mb, ui) for fs, fv, mb, ui in zip(from_s, from_v, a_rb, u)]
    s_new = [s * jnp.exp(o["total"]) + _mm_tn(jnp.concatenate([ui, o["v_s"]], axis=0), o["bk_end"])
             for o, s, ui in zip(ops, states, u)]
    for q in pairs:
        s_sc[q] = s_new[q]
        y = y_s[q][:L] + y_s[q][L:]
        ym = headsum(y) * (1.0 / HEAD)
        yc = y - ym
        yv = headsum(yc * yc) * (1.0 / HEAD)
        ls = lanes[q]
        o_ref[:, ls] = yc * lax.rsqrt(yv + GN_EPS) * gnw_ref[:, ls] + gnb_ref[:, ls] + ops[q]["bonus"]

    @pl.when(c == pl.num_programs(2) - 1)
    def _():
        for q in pairs:
            sout_ref[2 * q] = s_new[q][:HEAD, :HEAD]
            sout_ref[2 * q + 1] = s_new[q][HEAD:, HEAD:]


def _wkv(r, k, v, lw, a, kkw, kaw, rk, gnw, gnb, s0, *, batch, seq, row0, chunk, pairs_per_step):
    d = r.shape[1]
    pps = pairs_per_step
    width = pps * LANES
    nc = seq // chunk
    blk0 = row0 // chunk
    tok = pl.BlockSpec((chunk, width), lambda b, p, c: (blk0 + b * nc + c, p))
    out = pl.BlockSpec((chunk, width), lambda b, p, c: (b * nc + c, p))
    par = pl.BlockSpec((1, width), lambda b, p, c: (0, p))
    st = pl.BlockSpec((None, 2 * pps, HEAD, HEAD), lambda b, p, c: (b, p, 0, 0))
    has_init = s0 is not None
    return pl.pallas_call(
        functools.partial(_wkv_kernel, chunk=chunk, pairs_per_step=pps, has_init=has_init),
        grid=(batch, d // width, nc),
        in_specs=[tok] * 5 + [par] * 5 + ([st] if has_init else []),
        out_specs=[out, st],
        out_shape=[jax.ShapeDtypeStruct((batch * seq, d), F32),
                   jax.ShapeDtypeStruct((batch, d // HEAD, HEAD, HEAD), F32)],
        scratch_shapes=[pltpu.VMEM((pps, LANES, LANES), F32)],
        compiler_params=_params(("parallel", "parallel", "arbitrary")),
        name="wkv_l%d" % chunk,
    )(r, k, v, lw, a, kkw, kaw, rk, gnw, gnb, *([s0] if has_init else []))


def _proj_ln_kernel(op_ref, os_ref, g_ref, xa_ref, xb_ref, w_ref, lng_ref, lnb_ref, out_ref, *, alpha,
                    gated, prompt_tiles):
    is_prompt = pl.program_id(0) < prompt_tiles
    o = jnp.where(is_prompt, op_ref[...], os_ref[...])
    if gated:
        o = o * g_ref[...]
    y = _mm(o, w_ref[...])
    x = jnp.where(is_prompt, xa_ref[...], xb_ref[...])
    out_ref[...] = _layer_norm(alpha * x + y, lng_ref[...], lnb_ref[...])


def _proj_ln(o_p, o_s, g, x_parts, w, lng, lnb, *, alpha, tm):
    x_a, x_b, sample_block0 = x_parts
    d = x_a.shape[1]
    kin = o_p.shape[1]
    npt = o_p.shape[0] // tm
    t = o_p.shape[0] + o_s.shape[0]
    gated = g is not None
    g_spec = pl.BlockSpec((tm, kin), lambda i: (i, 0))
    if not gated:
        g, g_spec = lng, pl.BlockSpec((1, d), lambda i: (0, 0))
    return pl.pallas_call(
        functools.partial(_proj_ln_kernel, alpha=alpha, gated=gated, prompt_tiles=npt),
        grid=(t // tm,),
        in_specs=_part_specs(tm, kin, npt) + [g_spec] + _part_specs(tm, d, npt, sample_block0)
                 + [pl.BlockSpec(w.shape, lambda i: (0, 0)),
                    pl.BlockSpec((1, d), lambda i: (0, 0)),
                    pl.BlockSpec((1, d), lambda i: (0, 0))],
        out_specs=pl.BlockSpec((tm, d), lambda i: (i, 0)),
        out_shape=jax.ShapeDtypeStruct((t, d), F32),
        compiler_params=_params(("parallel",)),
        name="proj_ln",
    )(o_p, o_s, g, x_a, x_b, w, lng, lnb)


def _router_kernel(x_ref, w_ref, b_ref, route_ref, cnt_ref):
    ng, epg = EXPERT_GROUPS, EXPERTS_PER_GROUP
    x = x_ref[...]
    x_hi = x.astype(BF16)
    x_lo = (x - x_hi.astype(F32)).astype(BF16)
    by_hi = jnp.dot(x_hi, w_ref[...], preferred_element_type=F32)
    logits = (by_hi[:, :LANES] + by_hi[:, LANES:]
              + jnp.dot(x_lo, w_ref[:, :LANES], preferred_element_type=F32)
              + b_ref[...])
    lane = _iota2(logits.shape, 1)
    neg = jnp.float32(-jnp.inf)
    big = jnp.int32(1 << 20)

    is_g = lane < ng
    lg = jnp.where(is_g, logits, neg)
    mg = jnp.max(lg, axis=-1, keepdims=True)
    g_sel = jnp.min(jnp.where(lg == mg, lane, big), axis=-1, keepdims=True)
    zg = jnp.sum(jnp.where(is_g, jnp.exp(lg - mg), 0.0), axis=-1, keepdims=True)
    p_group = 1.0 / zg

    e_lane = lane - ng
    in_sel = (e_lane >= g_sel * epg) & (e_lane < (g_sel + 1) * epg)
    le = jnp.where(in_sel, logits, neg)
    m1 = jnp.max(le, axis=-1, keepdims=True)
    i1 = jnp.min(jnp.where(le == m1, lane, big), axis=-1, keepdims=True)
    ze = jnp.sum(jnp.where(in_sel, jnp.exp(le - m1), 0.0), axis=-1, keepdims=True)
    le2 = jnp.where(lane == i1, neg, le)
    m2 = jnp.max(le2, axis=-1, keepdims=True)
    i2 = jnp.min(jnp.where(le2 == m2, lane, big), axis=-1, keepdims=True)
    pe1 = 1.0 / ze
    pe2 = jnp.exp(m2 - m1) / ze
    den = pe1 + pe2
    c1 = pe1 / den * p_group
    c2 = pe2 / den * p_group
    e1 = i1 - ng
    e2 = i2 - ng
    route_ref[...] = jnp.where(lane == 0, e1.astype(F32),
                     jnp.where(lane == 1, e2.astype(F32),
                     jnp.where(lane == 2, c1, jnp.where(lane == 3, c2, 0.0))))

    @pl.when(pl.program_id(0) == 0)
    def _():
        cnt_ref[...] = jnp.zeros_like(cnt_ref)

    chosen = ((lane == e1) | (lane == e2)).astype(F32)
    cnt_ref[...] += jnp.sum(chosen, axis=0, keepdims=True)


def _router(x, w_terms, b, tm):
    t, d = x.shape
    return pl.pallas_call(
        _router_kernel,
        grid=(t // tm,),
        in_specs=[pl.BlockSpec((tm, d), lambda i: (i, 0)),
                  pl.BlockSpec((d, 2 * LANES), lambda i: (0, 0)),
                  pl.BlockSpec((1, LANES), lambda i: (0, 0))],
        out_specs=[pl.BlockSpec((tm, LANES), lambda i: (i, 0)),
                   pl.BlockSpec((1, LANES), lambda i: (0, 0))],
        out_shape=[jax.ShapeDtypeStruct((t, LANES), F32),
                   jax.ShapeDtypeStruct((1, LANES), F32)],
        compiler_params=_params(("arbitrary",)),
        name="moe_router",
    )(x, w_terms, b)


def _slots_kernel(route_ref, cnt_ref, pos_ref, ends_ref, off_sc, *, tile_rows):
    i = pl.program_id(0)
    route = route_ref[...]
    tm = route.shape[0]
    lane = _iota2((tm, LANES), 1)
    oh1 = lane == route[:, 0:1].astype(I32)
    oh2 = lane == route[:, 1:2].astype(I32)
    oh = (oh1 | oh2).astype(F32)

    @pl.when(i == 0)
    def _():
        padded = jnp.ceil(cnt_ref[...] * (1.0 / tile_rows)) * tile_rows
        before = (_iota2((LANES, LANES), 0) < _iota2((LANES, LANES), 1)).astype(F32)
        starts = _mm_sel_r(padded, before)
        off_sc[...] = starts
        ends_ref[...] = (starts + padded).astype(I32)

    earlier = (_iota2((tm, tm), 1) < _iota2((tm, tm), 0)).astype(F32)
    base = off_sc[...] + _mm(earlier, oh)
    p1 = jnp.sum(jnp.where(oh1, base, 0.0), axis=-1, keepdims=True)
    p2 = jnp.sum(jnp.where(oh2, base, 0.0), axis=-1, keepdims=True)
    pos_ref[...] = jnp.where(lane == 0, p1, jnp.where(lane == 1, p2, 0.0)).astype(I32)
    off_sc[...] += jnp.sum(oh, axis=0, keepdims=True)


def _slots(route, counts, tm, tile_rows):
    t = route.shape[0]
    return pl.pallas_call(
        functools.partial(_slots_kernel, tile_rows=tile_rows),
        grid=(t // tm,),
        in_specs=[pl.BlockSpec((tm, LANES), lambda i: (i, 0)),
                  pl.BlockSpec((1, LANES), lambda i: (0, 0))],
        out_specs=[pl.BlockSpec((tm, LANES), lambda i: (i, 0)),
                   pl.BlockSpec((1, LANES), lambda i: (0, 0))],
        out_shape=[jax.ShapeDtypeStruct((t, LANES), I32),
                   jax.ShapeDtypeStruct((1, LANES), I32)],
        scratch_shapes=[pltpu.VMEM((1, LANES), F32)],
        compiler_params=_params(("arbitrary",)),
        name="moe_slots",
    )(route, counts)


DMA_UNROLL = 8
COMBINE_CHUNKS = 8


def _slot_tokens_kernel(pos_ref, tok_ref):
    n_slots = tok_ref.shape[0]
    n_tokens = pos_ref.shape[0] // 2

    def clear(s, carry):
        tok_ref[s] = 0
        return carry

    def assign(t, carry):
        tok_ref[pos_ref[2 * t]] = t
        tok_ref[pos_ref[2 * t + 1]] = t
        return carry

    lax.fori_loop(0, n_slots, clear, 0, unroll=DMA_UNROLL)
    lax.fori_loop(0, n_tokens, assign, 0, unroll=DMA_UNROLL)


def _slot_tokens(pos_flat, n_slots):
    return pl.pallas_call(
        _slot_tokens_kernel,
        in_specs=[pl.BlockSpec(memory_space=pltpu.SMEM)],
        out_specs=pl.BlockSpec(memory_space=pltpu.SMEM),
        out_shape=jax.ShapeDtypeStruct((n_slots,), I32),
        name="moe_slot_tokens",
    )(pos_flat)


GATHER_GROUPS = 3


def _experts_kernel(te_ref, nx_ref, nu_ref, tok_ref, x_ref, wg_ref, wu_ref, wd_ref, out_ref, wg_sc, wu_sc,
                    wd_sc, wg_in, wu_in, wd_in, xg, sem, gsem, slot_ref, *, expert0):
    j = pl.program_id(0)
    n_used = nu_ref[0]
    used = j < n_used
    prev = te_ref[jnp.maximum(j - 1, 0)]
    tile_rows = xg.shape[1]
    buf = j % 2
    nxt_tile = jnp.minimum(j + 1, n_used - 1)

    def gather(tile, b, r):
        tok = tok_ref[tile * tile_rows + r]
        return pltpu.make_async_copy(x_ref.at[pl.ds(tok, 1)], xg.at[b, pl.ds(r, 1)], gsem.at[b])

    def wait_tile(tile, b):
        for _ in range(tile_rows):
            gather(tile, b, 0).wait()

    @pl.when(used & (j == 0))
    def _():
        def body(r, carry):
            gather(0, 0, r).start()
            return carry
        lax.fori_loop(0, tile_rows, body, 0, unroll=DMA_UNROLL)

    def fetches(e, s):
        return [pltpu.make_async_copy(hbm.at[expert0 + e], dst.at[s], sem.at[s])
                for hbm, dst in ((wg_ref, wg_in), (wu_ref, wu_in), (wd_ref, wd_in))]

    @pl.when(used & (j == 0))
    def _():
        slot_ref[0] = 0
        for cp in fetches(te_ref[0], 0):
            cp.start()

    @pl.when(used & ((j == 0) | (te_ref[j] != prev)))
    def _():
        s = slot_ref[0]
        for cp in fetches(te_ref[j], s):
            cp.wait()
        wg_sc[...] = wg_in[s].astype(BF16)
        wu_sc[...] = wu_in[s].astype(BF16)
        wd_sc[...] = wd_in[s].astype(BF16)
        nxt = nx_ref[j]

        @pl.when(nxt >= 0)
        def _():
            for cp in fetches(nxt, 1 - s):
                cp.start()
        slot_ref[0] = 1 - s

    @pl.when(used)
    def _():
        wait_tile(j, buf)
        bounds = [tile_rows * g // GATHER_GROUPS for g in range(GATHER_GROUPS + 1)]

        def start_group(g):
            for r in range(bounds[g], bounds[g + 1]):
                gather(nxt_tile, 1 - buf, r).start(priority=r % 2)

        start_group(0)
        xb = xg[buf].astype(BF16)
        hg = jnp.dot(xb, wg_sc[...], preferred_element_type=F32)
        start_group(1)
        h = _silu(hg) * jnp.dot(xb, wu_sc[...], preferred_element_type=F32)
        start_group(2)
        out_ref[...] = jnp.dot(h.astype(BF16), wd_sc[...], preferred_element_type=F32)

    @pl.when(j == n_used - 1)
    def _():
        wait_tile(nxt_tile, 1 - buf)

    @pl.when(jnp.logical_not(used))
    def _():
        out_ref[...] = jnp.zeros_like(out_ref)


def _experts(tile_expert, next_expert, n_used, slot_tok, x, wg, wu, wd, tile_rows, expert0):
    d = x.shape[1]
    n_slots = slot_tok.shape[0]
    f = wg.shape[-1]
    assert GATHER_GROUPS == 3
    hbm = pl.BlockSpec(memory_space=pl.ANY)
    grid_spec = pltpu.PrefetchScalarGridSpec(
        num_scalar_prefetch=4,
        grid=(n_slots // tile_rows,),
        in_specs=[hbm, hbm, hbm, hbm],
        out_specs=pl.BlockSpec((tile_rows, d), lambda j, te, nx, nu, tok: (j, 0)),
        scratch_shapes=[pltpu.VMEM((d, f), BF16), pltpu.VMEM((d, f), BF16), pltpu.VMEM((f, d), BF16),
                        pltpu.VMEM((2, d, f), F32), pltpu.VMEM((2, d, f), F32), pltpu.VMEM((2, f, d), F32),
                        pltpu.VMEM((2, tile_rows, d), F32),
                        pltpu.SemaphoreType.DMA((2,)), pltpu.SemaphoreType.DMA((2,)),
                        pltpu.SMEM((1,), I32)],
    )
    return pl.pallas_call(
        functools.partial(_experts_kernel, expert0=expert0),
        grid_spec=grid_spec,
        out_shape=jax.ShapeDtypeStruct((n_slots, d), F32),
        compiler_params=_params(("arbitrary",)),
        name="moe_experts",
    )(tile_expert, next_expert, n_used, slot_tok, x, wg, wu, wd)


def _combine_kernel(pos_ref, x_ref, route_ref, lng_ref, lnb_ref, ys_ref, *rest, alpha, split_tiles):
    if split_tiles is None:
        out_ref, o_sc, sem = rest
    else:
        out_ref, out_s_ref, o_sc, sem = rest
    i = pl.program_id(0)
    n = pl.num_programs(0)
    tm = x_ref.shape[0]
    slot = i % 2
    spare = 1 - slot
    nxt = jnp.minimum(i + 1, n - 1)
    rows_per_chunk = tm // COMBINE_CHUNKS

    def copy(tile, sl, t, c):
        p = pos_ref[2 * (tile * tm + t) + c]
        return pltpu.make_async_copy(ys_ref.at[pl.ds(p, 1)], o_sc.at[sl, c, pl.ds(t, 1)], sem.at[sl])

    def wait_tile(tile, sl):
        for _ in range(2 * tm):
            copy(tile, sl, 0, 0).wait()

    @pl.when(i == 0)
    def _():
        def body(t, carry):
            copy(0, 0, t, 0).start(priority=0)
            copy(0, 0, t, 1).start(priority=1)
            return carry
        lax.fori_loop(0, tm, body, 0, unroll=DMA_UNROLL)

    wait_tile(i, slot)

    def mix_into(out):
        for k in range(COMBINE_CHUNKS):
            rows = slice(k * rows_per_chunk, (k + 1) * rows_per_chunk)
            for t in range(rows.start, rows.stop):
                copy(nxt, spare, t, 0).start(priority=0)
                copy(nxt, spare, t, 1).start(priority=1)
            y = route_ref[rows, 2:3] * o_sc[slot, 0, rows] + route_ref[rows, 3:4] * o_sc[slot, 1, rows]
            out[rows, :] = _layer_norm(alpha * x_ref[rows, :] + y, lng_ref[...], lnb_ref[...])

    if split_tiles is None:
        mix_into(out_ref)
    else:
        @pl.when(i < split_tiles)
        def _():
            mix_into(out_ref)

        @pl.when(i >= split_tiles)
        def _():
            mix_into(out_s_ref)

    @pl.when(i == n - 1)
    def _():
        wait_tile(nxt, spare)


def _combine(pos_flat, x, route, lng, lnb, ys, *, alpha, tm, split_rows=None):
    t, d = x.shape
    out_specs = pl.BlockSpec((tm, d), lambda i, pos: (i, 0))
    out_shape = jax.ShapeDtypeStruct((t, d), F32)
    npt = None
    if split_rows is not None:
        npt = split_rows // tm
        out_specs = [pl.BlockSpec((tm, d), lambda i, pos: (jnp.minimum(i, npt - 1), 0)),
                     pl.BlockSpec((tm, d), lambda i, pos: (jnp.maximum(i - npt, 0), 0))]
        out_shape = [jax.ShapeDtypeStruct((split_rows, d), F32),
                     jax.ShapeDtypeStruct((t - split_rows, d), F32)]
    grid_spec = pltpu.PrefetchScalarGridSpec(
        num_scalar_prefetch=1,
        grid=(t // tm,),
        in_specs=[pl.BlockSpec((tm, d), lambda i, pos: (i, 0)),
                  pl.BlockSpec((tm, LANES), lambda i, pos: (i, 0)),
                  pl.BlockSpec((1, d), lambda i, pos: (0, 0)),
                  pl.BlockSpec((1, d), lambda i, pos: (0, 0)),
                  pl.BlockSpec(memory_space=pl.ANY)],
        out_specs=out_specs,
        scratch_shapes=[pltpu.VMEM((2, 2, tm, d), F32), pltpu.SemaphoreType.DMA((2,))],
    )
    return pl.pallas_call(
        functools.partial(_combine_kernel, alpha=alpha, split_tiles=npt),
        grid_spec=grid_spec,
        out_shape=out_shape,
        compiler_params=_params(("arbitrary",)),
        name="moe_combine",
    )(pos_flat, x, route, lng, lnb, ys)


def _hier_moe_ln(x, w_rg, b_rg, w_re, b_re, w_gate, w_up, w_down, lng, lnb, *, alpha, tm, tile_rows,
                 layer=0, split_rows=None):
    t, d = x.shape
    ng, epg = EXPERT_GROUPS, EXPERTS_PER_GROUP
    n_exp = ng * epg
    pad = LANES - ng - n_exp
    w_router = jnp.pad(jnp.concatenate([w_rg, w_re], axis=1), ((0, 0), (0, pad)))
    b_router = jnp.pad(jnp.concatenate([b_rg, b_re]), (0, pad))[None, :]
    w_hi = w_router.astype(BF16)
    w_lo = (w_router - w_hi.astype(F32)).astype(BF16)
    route, counts = _router(x, jnp.concatenate([w_hi, w_lo], axis=1), b_router, tm)
    pos, ends = _slots(route, counts, tm, tile_rows)
    pos_flat = pos[:, :2].reshape(-1)
    n_slots = 2 * t + n_exp * tile_rows
    n_tiles = n_slots // tile_rows
    seg_end = ends[0, :n_exp]
    tile_start = jnp.arange(n_tiles, dtype=I32) * tile_rows
    tile_expert = jnp.minimum(jnp.sum(tile_start[:, None] >= seg_end[None, :], axis=1), n_exp - 1).astype(I32)
    n_used = (seg_end[n_exp - 1:] // tile_rows).astype(I32)
    after = seg_end[tile_expert] // tile_rows
    next_expert = jnp.where(after < n_used[0], tile_expert[jnp.minimum(after, n_tiles - 1)], -1).astype(I32)
    slot_tok = _slot_tokens(pos_flat, n_slots)
    f = w_gate.shape[-1]
    ys = _experts(tile_expert, next_expert, n_used, slot_tok, x, w_gate.reshape(-1, d, f), w_up.reshape(-1, d, f),
                  w_down.reshape(-1, f, d), tile_rows, layer * n_exp)
    return _combine(pos_flat, x, route, lng, lnb, ys, alpha=alpha, tm=tm, split_rows=split_rows)


def _ssd_in_kernel(x_ref, wz_ref, wx_ref, wdt_ref, z_ref, xbc_ref, dt_ref):
    xb = x_ref[...].astype(BF16)
    z_ref[...] = jnp.dot(xb, wz_ref[...], preferred_element_type=F32)
    xbc_ref[...] = jnp.dot(xb, wx_ref[...], preferred_element_type=F32)
    dt_ref[...] = jnp.dot(xb, wdt_ref[...], preferred_element_type=F32)


def _ssd_in(x, wz, wx, wdt, tm):
    t, d = x.shape
    outs = [wz.shape[1], wx.shape[1], wdt.shape[1]]
    return pl.pallas_call(
        _ssd_in_kernel,
        grid=(t // tm,),
        in_specs=[pl.BlockSpec((tm, d), lambda i: (i, 0))]
                 + [pl.BlockSpec(w.shape, lambda i: (0, 0)) for w in (wz, wx, wdt)],
        out_specs=[pl.BlockSpec((tm, n), lambda i: (i, 0)) for n in outs],
        out_shape=[jax.ShapeDtypeStruct((t, n), F32) for n in outs],
        compiler_params=_params(("parallel",)),
        name="ssd_in",
    )(x, wz, wx, wdt)


def _ssd_kernel(xbc_ref, dt_ref, z_ref, conv0_ref, h0_ref, cw_ref, cb_ref, dtb_ref, alog_ref,
                dexp_ref, nw_ref, expand_ref, y_ref, convo_ref, hout_ref, pad_sc, ht_sc, *, chunk):
    c = chunk
    ci = pl.program_id(1)
    d_inner = z_ref.shape[1]
    gn = SSD_GROUPS * D_STATE
    per_group = d_inner // SSD_GROUPS
    tail = 8

    @pl.when(ci == 0)
    def _():
        pad_sc[0:tail, :] = conv0_ref[...]
        ht_sc[...] = h0_ref[...].T

    pad_sc[tail:tail + c, :] = xbc_ref[...]
    acc = cb_ref[...]
    for i in range(CONV_WIDTH):
        off = tail - (CONV_WIDTH - 1) + i
        acc = acc + pad_sc[off:off + c, :] * cw_ref[i:i + 1, :]
    xbc = _silu(acc)
    last_rows = pad_sc[c:c + tail, :]
    pad_sc[0:tail, :] = last_rows
    xs = xbc[:, :d_inner]

    dt = _softplus(dt_ref[...] + dtb_ref[...])
    da = dt * (-jnp.exp(alog_ref[...]))
    tril_incl = (_iota2((c, c), 1) <= _iota2((c, c), 0)).astype(F32)
    acum = _mm_sel_l(tril_incl, da)
    a_last = acum[c - 1:c, :]

    def to_cols(t):
        if c < LANES:
            t = jnp.concatenate([t, jnp.zeros((LANES - c, LANES), F32)], axis=0)
        return t.T[:, :c]

    acum_t = to_cols(acum)
    expand = expand_ref[...]
    per_head = jnp.concatenate([jnp.exp(acum), jnp.exp(a_last - acum) * dt, dt], axis=0)
    per_chan = _mm_sel_r(per_head, expand)
    ea, dd = per_chan[:c], per_chan[c:2 * c]
    x_dt = xs * per_chan[2 * c:]
    ea_last = _mm_sel_r(jnp.exp(a_last), expand)
    causal = _iota2((c, c), 1) <= _iota2((c, c), 0)
    lane = _iota2((1, LANES), 1)
    m0 = (lane < HEAD).astype(F32)
    m1 = 1.0 - m0

    ys = []
    for g in range(SSD_GROUPS):
        b_g = xbc[:, d_inner + g * D_STATE:d_inner + (g + 1) * D_STATE]
        c_g = xbc[:, d_inner + gn + g * D_STATE:d_inner + gn + (g + 1) * D_STATE]
        cb = _mm_nt(c_g, b_g)
        gsl = slice(g * per_group, (g + 1) * per_group)
        ht_g = ht_sc[:, gsl]
        y_off = _mm(c_g, ht_g) * ea[:, gsl]
        ht_sc[:, gsl] = ht_g * ea_last[:, gsl] + _mm_tn(b_g, xs[:, gsl] * dd[:, gsl])
        pieces = []
        for q in range(per_group // LANES):
            j0 = (g * per_group) // HEAD + 2 * q
            ms = []
            for j in (j0, j0 + 1):
                seg = acum[:, j:j + 1] - acum_t[j:j + 1, :]
                ms.append(jnp.where(causal, cb * jnp.exp(seg), 0.0))
            lo = g * per_group + q * LANES
            x_pair = x_dt[:, lo:lo + LANES]
            x_stack = jnp.concatenate([x_pair * m0, x_pair * m1], axis=0)
            pieces.append(_mm(jnp.concatenate(ms, axis=1), x_stack))
        y_g = jnp.concatenate(pieces, axis=1) + y_off + xs[:, gsl] * dexp_ref[:, gsl]
        y_g = y_g * _silu(z_ref[:, gsl])
        y_g = y_g * lax.rsqrt(jnp.mean(y_g * y_g, axis=-1, keepdims=True) + RMS_EPS)
        ys.append(y_g * nw_ref[:, gsl])
    y_ref[...] = jnp.concatenate(ys, axis=1)

    @pl.when(ci == pl.num_programs(1) - 1)
    def _():
        convo_ref[...] = last_rows
        hout_ref[...] = ht_sc[...].T


def _ssd(xbc, dt, z, conv0, h0, cw, cb, dtb, alog, dexp, nw, expand, *, batch, seq, row0, chunk):
    d_inner = z.shape[1]
    conv_dim = xbc.shape[1]
    nc = seq // chunk
    blk0 = row0 // chunk

    def tok(n):
        return pl.BlockSpec((chunk, n), lambda b, c: (blk0 + b * nc + c, 0))

    def par(arr):
        return pl.BlockSpec(arr.shape, lambda b, c: (0, 0))

    return pl.pallas_call(
        functools.partial(_ssd_kernel, chunk=chunk),
        grid=(batch, nc),
        in_specs=[tok(conv_dim), tok(LANES), tok(d_inner),
                  pl.BlockSpec((None, 8, conv_dim), lambda b, c: (b, 0, 0)),
                  pl.BlockSpec((None, d_inner, D_STATE), lambda b, c: (b, 0, 0)),
                  par(cw), par(cb), par(dtb), par(alog), par(dexp), par(nw), par(expand)],
        out_specs=[pl.BlockSpec((chunk, d_inner), lambda b, c: (b * nc + c, 0)),
                   pl.BlockSpec((None, 8, conv_dim), lambda b, c: (b, 0, 0)),
                   pl.BlockSpec((None, d_inner, D_STATE), lambda b, c: (b, 0, 0))],
        out_shape=[jax.ShapeDtypeStruct((batch * seq, d_inner), F32),
                   jax.ShapeDtypeStruct((batch, 8, conv_dim), F32),
                   jax.ShapeDtypeStruct((batch, d_inner, D_STATE), F32)],
        scratch_shapes=[pltpu.VMEM((chunk + 8, conv_dim), F32), pltpu.VMEM((D_STATE, d_inner), F32)],
        compiler_params=_params(("parallel", "arbitrary")),
        name="ssd_l%d" % chunk,
    )(xbc, dt, z, conv0, h0, cw, cb, dtb, alog, dexp, nw, expand)


def _rwkv_layer(xp3, xs3, shift_s, wkv_s, prm, lng, lnb, *, alpha, tm):
    (mu, w_rkv, w0, w1, w2, a0, a1, a2, g1, g2, k_k, k_a, r_k, gn_w, gn_b, w_o) = prm
    bp, sp, d = xp3.shape
    bs, ss, _ = xs3.shape
    tp = bp * sp
    heads = d // HEAD
    x_p = xp3.reshape(tp, d)
    x_s = xs3.reshape(bs * ss, d)
    prev_s = jnp.concatenate([shift_s[:, None, :], xs3[:, :-1]], axis=1).reshape(bs * ss, d)
    row = lambda v: v.reshape(1, d)
    r, k, v, lw, a, g = _rwkv_pre(x_p, x_s, prev_s, sp, mu, w_rkv.astype(BF16), row(w0), w1.astype(BF16),
                                  w2.astype(BF16), row(a0), a1.astype(BF16), a2.astype(BF16),
                                  g1.astype(BF16), g2.astype(BF16), tm)
    pv = (row(k_k), row(k_a), row(r_k), row(gn_w), row(gn_b))
    o_p, sp_out = _wkv(r, k, v, lw, a, *pv, None, batch=bp, seq=sp, row0=0,
                       chunk=min(WKV_CHUNK, sp), pairs_per_step=min(WKV_PAIRS_PER_STEP, heads // 2))
    o_s, ss_out = _wkv(r, k, v, lw, a, *pv, wkv_s, batch=bs, seq=ss, row0=tp,
                       chunk=min(WKV_CHUNK, ss), pairs_per_step=heads // 2)
    x1 = _proj_ln(o_p, o_s, g, (x_p, x_s, 0), w_o.astype(BF16), lng, lnb, alpha=alpha, tm=tm)
    return x1, sp_out, ss_out


def _ssd_layer(x, bp, sp, bs, ss, conv_s, ssm_s, prm, lng, lnb, *, alpha, tm):
    (w_in, conv_w, conv_b, dt_bias, a_log, d_skip, norm_w, w_out) = prm
    t, d = x.shape
    tp = bp * sp
    heads = a_log.shape[0]
    d_inner = heads * HEAD
    conv_dim = conv_w.shape[1]
    wz = w_in[:, :d_inner].astype(BF16)
    wx = w_in[:, d_inner:d_inner + conv_dim].astype(BF16)
    wdt = jnp.pad(w_in[:, d_inner + conv_dim:], ((0, 0), (0, LANES - heads))).astype(BF16)
    z, xbc, dt = _ssd_in(x, wz, wx, wdt, tm)
    lane_pad = lambda v: jnp.pad(v, (0, LANES - heads))[None, :]
    expand = (jnp.arange(LANES)[:, None] == (jnp.arange(d_inner) // HEAD)[None, :]).astype(F32)
    common = (conv_w, conv_b[None, :], lane_pad(dt_bias), lane_pad(a_log),
              jnp.repeat(d_skip, HEAD)[None, :], norm_w[None, :], expand)
    conv_tail = CONV_WIDTH - 1
    pad_conv = lambda cs: jnp.pad(cs, ((0, 0), (8 - conv_tail, 0), (0, 0)))
    y_p, conv_p, h_p = _ssd(xbc, dt, z, jnp.zeros((bp, 8, conv_dim), F32),
                            jnp.zeros((bp, d_inner, D_STATE), F32), *common,
                            batch=bp, seq=sp, row0=0, chunk=SSD_CHUNK if sp % SSD_CHUNK == 0 else sp)
    y_s, conv_s_out, h_s = _ssd(xbc, dt, z, pad_conv(conv_s), ssm_s.reshape(bs, d_inner, D_STATE), *common,
                                batch=bs, seq=ss, row0=tp, chunk=SSD_CHUNK if ss % SSD_CHUNK == 0 else ss)
    x1 = _proj_ln(y_p, y_s, None, (x, x, tp // tm), w_out.astype(BF16), lng, lnb, alpha=alpha, tm=tm)
    shape_h = lambda h, b: h.reshape(b, heads, HEAD, D_STATE)
    return (x1, conv_p[:, 8 - conv_tail:], conv_s_out[:, 8 - conv_tail:], shape_h(h_p, bp), shape_h(h_s, bs))


def kernel(x_prompt, x_sample, state_rwkv_wkv, state_rwkv_shift, state_ssd_ssm, state_ssd_conv,
           rwkv_mu, rwkv_w_rkv, rwkv_w0, rwkv_w1, rwkv_w2, rwkv_a0, rwkv_a1, rwkv_a2,
           rwkv_g1, rwkv_g2, rwkv_k_k, rwkv_k_a, rwkv_r_k, rwkv_gn_w, rwkv_gn_b, rwkv_w_o,
           ssd_w_in, ssd_conv_w, ssd_conv_b, ssd_dt_bias, ssd_a_log, ssd_d, ssd_norm_w, ssd_w_out,
           ln_gain, ln_bias, moe_w_rg, moe_b_rg, moe_w_re, moe_b_re, moe_w_gate, moe_w_up, moe_w_down):
    bp, sp, d = x_prompt.shape
    bs, ss, _ = x_sample.shape
    depth = ln_gain.shape[0]
    alpha = (2 * depth) ** 0.25
    tp, ts = bp * sp, bs * ss
    tm = 256
    tile_rows = 256
    x = None
    xp3, xs3 = x_prompt, x_sample
    wkv_p, wkv_s, sh_p, sh_s, ssm_p, ssm_s, cv_p, cv_s = [], [], [], [], [], [], [], []
    for i in range(depth):
        j = i // 2
        lng = lambda n: ln_gain[i, n][None, :]
        lnb = lambda n: ln_bias[i, n][None, :]
        if i % 2 == 0:
            if x is not None:
                xp3, xs3 = x[:tp].reshape(bp, sp, d), x[tp:].reshape(bs, ss, d)
            prm = (rwkv_mu[j], rwkv_w_rkv[j], rwkv_w0[j], rwkv_w1[j], rwkv_w2[j], rwkv_a0[j],
                   rwkv_a1[j], rwkv_a2[j], rwkv_g1[j], rwkv_g2[j], rwkv_k_k[j], rwkv_k_a[j],
                   rwkv_r_k[j].reshape(-1), rwkv_gn_w[j], rwkv_gn_b[j], rwkv_w_o[j])
            sh_p.append(xp3[:, -1])
            sh_s.append(xs3[:, -1])
            x, w_p, w_s = _rwkv_layer(xp3, xs3, state_rwkv_shift[j], state_rwkv_wkv[j], prm,
                                      lng(0), lnb(0), alpha=alpha, tm=tm)
            wkv_p.append(w_p)
            wkv_s.append(w_s)
        else:
            prm = (ssd_w_in[j], ssd_conv_w[j], ssd_conv_b[j], ssd_dt_bias[j], ssd_a_log[j],
                   ssd_d[j], ssd_norm_w[j], ssd_w_out[j])
            x, c_p, c_s, h_p, h_s = _ssd_layer(x, bp, sp, bs, ss, state_ssd_conv[j], state_ssd_ssm[j],
                                               prm, lng(0), lnb(0), alpha=alpha, tm=tm)
            cv_p.append(c_p)
            cv_s.append(c_s)
            ssm_p.append(h_p)
            ssm_s.append(h_s)
        x = _hier_moe_ln(x, moe_w_rg[i], moe_b_rg[i], moe_w_re[i], moe_b_re[i], moe_w_gate,
                         moe_w_up, moe_w_down, lng(1), lnb(1), alpha=alpha, tm=tm,
                         tile_rows=tile_rows, layer=i, split_rows=tp if i == depth - 1 else None)
    yp = x[0].reshape(bp, sp, d)
    ys = x[1].reshape(bs, ss, d)
    return (yp, ys, jnp.stack(wkv_p), jnp.stack(wkv_s), jnp.stack(sh_p), jnp.stack(sh_s),
            jnp.stack(ssm_p), jnp.stack(ssm_s), jnp.stack(cv_p), jnp.stack(cv_s))
```

```python
import functools
import math

import jax
import jax.numpy as jnp
from jax import lax
from jax.experimental import pallas as pl
from jax.experimental.pallas import tpu as pltpu

F32 = jnp.float32
BF16 = jnp.bfloat16
I32 = jnp.int32

LANES = 128
HEAD = 64
GN_EPS = HEAD * 1e-5
RMS_EPS = 1e-5
LN_EPS = 1e-5
WKV_CHUNK = 64
WKV_PAIRS_PER_STEP = 8
SSD_CHUNK = 128
SSD_GROUPS = 4
D_STATE = 128
CONV_WIDTH = 4
EXPERT_GROUPS = 4
EXPERTS_PER_GROUP = 8
VMEM_LIMIT = 56 * 1024 * 1024


def _params(sem):
    return pltpu.CompilerParams(dimension_semantics=sem, vmem_limit_bytes=VMEM_LIMIT)


def _mm(a, b):
    return jnp.dot(a.astype(BF16), b.astype(BF16), preferred_element_type=F32)


def _mm_nt(a, b):
    return lax.dot_general(a.astype(BF16), b.astype(BF16), (((1,), (1,)), ((), ())),
                           preferred_element_type=F32)


def _mm_tn(a, b):
    return lax.dot_general(a.astype(BF16), b.astype(BF16), (((0,), (0,)), ((), ())),
                           preferred_element_type=F32)


def _split3(x):
    hi = x.astype(BF16)
    r1 = x - hi.astype(F32)
    mid = r1.astype(BF16)
    lo = (r1 - mid.astype(F32)).astype(BF16)
    return hi, mid, lo


def _mm_sel_l(sel, x):
    k = x.shape[1]
    parts = jnp.concatenate([p.astype(F32) for p in _split3(x)], axis=1).astype(BF16)
    y = jnp.dot(sel.astype(BF16), parts, preferred_element_type=F32)
    return y[:, :k] + y[:, k:2 * k] + y[:, 2 * k:]


def _mm_sel_r(x, sel):
    n = x.shape[0]
    if n % 8:
        return sum(jnp.dot(p, sel.astype(BF16), preferred_element_type=F32) for p in _split3(x))
    parts = jnp.concatenate([p.astype(F32) for p in _split3(x)], axis=0).astype(BF16)
    y = jnp.dot(parts, sel.astype(BF16), preferred_element_type=F32)
    return y[:n] + y[n:2 * n] + y[2 * n:]


def _sigmoid(x):
    return 1.0 / (1.0 + jnp.exp(-x))


def _silu(x):
    return x * _sigmoid(x)


def _softplus(x):
    return jnp.maximum(x, 0.0) + jnp.log(1.0 + jnp.exp(-jnp.abs(x)))


def _layer_norm(x, g, b):
    mu = jnp.mean(x, axis=-1, keepdims=True)
    xc = x - mu
    var = jnp.mean(xc * xc, axis=-1, keepdims=True)
    return xc * lax.rsqrt(var + LN_EPS) * g + b


def _iota2(shape, dim):
    return lax.broadcasted_iota(I32, shape, dim)


def _part_specs(tm, width, prompt_tiles, sample_block0=0):
    return [pl.BlockSpec((tm, width), lambda i: (jnp.minimum(i, prompt_tiles - 1), 0)),
            pl.BlockSpec((tm, width), lambda i: (sample_block0 + jnp.maximum(i - prompt_tiles, 0), 0))]


def _rwkv_pre_kernel(xa_ref, xb_ref, tail_ref, xprev_s_ref, mu_ref, wrkv_ref, w0_ref, w1_ref, w2_ref,
                     a0_ref, a1_ref, a2_ref, g1_ref, g2_ref, r_ref, k_ref, v_ref, lw_ref, a_ref, g_ref,
                     *, prompt_tiles, seq):
    i = pl.program_id(0)
    tm = xa_ref.shape[0]
    is_prompt = i < prompt_tiles
    xa = xa_ref[...]
    x = jnp.where(is_prompt, xa, xb_ref[...])
    first = jnp.where((i * tm) % seq == 0, 0.0, tail_ref[7:8, :])
    shifted = jnp.where(_iota2(xa.shape, 0) == 0, first, pltpu.roll(xa, 1, axis=0))
    xx = jnp.where(is_prompt, shifted, xprev_s_ref[...]) - x

    def mix(j):
        return x + xx * mu_ref[j:j + 1, :]

    r_ref[...] = _mm(mix(0), wrkv_ref[0])
    k_ref[...] = _mm(mix(1), wrkv_ref[1])
    v_ref[...] = _mm(mix(2), wrkv_ref[2])
    ww = w0_ref[...] + _mm(jnp.tanh(_mm(mix(3), w1_ref[...])), w2_ref[...])
    w_raw = -_softplus(-ww) - 0.5
    lw_ref[...] = -jnp.exp(w_raw)
    a_ref[...] = _sigmoid(a0_ref[...] + _mm(_mm(mix(4), a1_ref[...]), a2_ref[...]))
    g_ref[...] = _mm(_sigmoid(_mm(mix(5), g1_ref[...])), g2_ref[...])


def _rwkv_pre(x_p, x_s, xprev_s, seq, mu, wrkv, w0, w1, w2, a0, a1, a2, g1, g2, tm):
    (tp, d), ts = x_p.shape, x_s.shape[0]
    assert seq % tm == 0 and ts % tm == 0 and tm % 8 == 0
    npt = tp // tm
    row = pl.BlockSpec((tm, d), lambda i: (i, 0))
    tail = pl.BlockSpec((8, d), lambda i: (jnp.maximum(jnp.minimum(i, npt - 1) * (tm // 8) - 1, 0), 0))

    def full(arr):
        nd = arr.ndim
        return pl.BlockSpec(arr.shape, lambda i, _n=nd: (0,) * _n)

    weights = (mu, wrkv, w0, w1, w2, a0, a1, a2, g1, g2)
    return pl.pallas_call(
        functools.partial(_rwkv_pre_kernel, prompt_tiles=npt, seq=seq),
        grid=((tp + ts) // tm,),
        in_specs=_part_specs(tm, d, npt) + [tail, _part_specs(tm, d, npt)[1]] + [full(w) for w in weights],
        out_specs=[row] * 6,
        out_shape=[jax.ShapeDtypeStruct((tp + ts, d), F32)] * 6,
        compiler_params=_params(("parallel",)),
        name="rwkv_pre",
    )(x_p, x_s, x_p, xprev_s, *weights)


def _wkv_kernel(r_ref, k_ref, v_ref, lw_ref, a_ref, kkw_ref, kaw_ref, rk_ref, gnw_ref, gnb_ref,
                *rest, chunk, pairs_per_step, has_init):
    if has_init:
        s0_ref, o_ref, sout_ref, s_sc = rest
    else:
        o_ref, sout_ref, s_sc = rest
    L = chunk
    c = pl.program_id(2)

    @pl.when(c == 0)
    def _():
        if has_init:
            zero = jnp.zeros((HEAD, HEAD), F32)
            for q in range(pairs_per_step):
                top = jnp.concatenate([s0_ref[2 * q], zero], axis=1)
                bot = jnp.concatenate([zero, s0_ref[2 * q + 1]], axis=1)
                s_sc[q] = jnp.concatenate([top, bot], axis=0)
        else:
            s_sc[...] = jnp.zeros_like(s_sc)

    lane = _iota2((1, LANES), 1)
    m0 = (lane < HEAD).astype(F32)
    m1 = 1.0 - m0
    same_head = (_iota2((LANES, LANES), 0) // HEAD == _iota2((LANES, LANES), 1) // HEAD).astype(F32)
    tril_incl = (_iota2((L, L), 1) <= _iota2((L, L), 0)).astype(F32)
    n2 = 2 * L
    row = _iota2((n2, n2), 0)
    col = _iota2((n2, n2), 1)
    strict = col < row
    incl = col <= row
    steps = int(math.log2(L))

    def headsum(t):
        return _mm_sel_r(t, same_head)

    def stack(t):
        return jnp.concatenate([t * m0, t * m1], axis=0)

    pairs = range(pairs_per_step)
    lanes = [slice(q * LANES, (q + 1) * LANES) for q in pairs]

    def prepare(ls):
        r = r_ref[:, ls]
        k = k_ref[:, ls]
        v = v_ref[:, ls]
        lw = lw_ref[:, ls]
        a = a_ref[:, ls]
        kk = k * kkw_ref[:, ls]
        k2 = k * (1.0 + (a - 1.0) * kaw_ref[:, ls])
        sums = headsum(jnp.concatenate([kk * kk, r * k2 * rk_ref[:, ls]], axis=0))
        kk = kk / jnp.maximum(jnp.sqrt(sums[:L]), 1e-12)
        bonus = sums[L:] * v
        bvec = kk * a
        cl = _mm_sel_l(tril_incl, lw)
        total = cl[L - 1:L, :]
        w_in = jnp.exp(cl)
        w_inv = jnp.exp(-cl)
        w_end = jnp.exp(total - cl)
        return dict(
            ar=jnp.concatenate([stack(-kk * jnp.exp(cl - lw)), stack(r * w_in)], axis=0).astype(BF16),
            bk=jnp.concatenate([stack(bvec * w_inv), stack(k2 * w_inv)], axis=0).astype(BF16),
            bk_end=jnp.concatenate([stack(bvec * w_end), stack(k2 * w_end)], axis=0),
            v_s=stack(v), total=total, bonus=bonus)

    ops = [prepare(ls) for ls in lanes]
    states = [s_sc[q] for q in pairs]
    cross = [_mm_nt(o["ar"], o["bk"]) for o in ops]
    from_s = [_mm_nt(o["ar"], s) for o, s in zip(ops, states)]
    n_ab = [jnp.where(strict, x[:n2, :n2], 0.0) for x in cross]
    a_rb = [jnp.where(incl, x[n2:, :n2], 0.0) for x in cross]
    to_v = [jnp.concatenate([jnp.where(strict, x[:n2, n2:], 0.0), jnp.where(incl, x[n2:, n2:], 0.0)], axis=0)
            for x in cross]
    from_v = [_mm(m, o["v_s"]) for m, o in zip(to_v, ops)]

    u = [fs[:n2] + fv[:n2] for fs, fv in zip(from_s, from_v)]
    p = n_ab
    for j in range(steps):
        if j + 1 < steps:
            both = [_mm(pi, jnp.concatenate([ui, pi], axis=1)) for ui, pi in zip(u, p)]
            u = [ui + b[:, :LANES] for ui, b in zip(u, both)]
            p = [b[:, LANES:] for b in both]
        else:
            u = [ui + _mm(pi, ui) for ui, pi in zip(u, p)]

    y_s = [fs[n2:] + fv[n2:] + _mm(mb, ui) for fs, fv, mb, ui in zip(from_s, from_v, a_rb, u)]
    s_new = [s * jnp.exp(o["total"]) + _mm_tn(jnp.concatenate([ui, o["v_s"]], axis=0), o["bk_end"])
             for o, s, ui in zip(ops, states, u)]
    for q in pairs:
        s_sc[q] = s_new[q]
        y = y_s[q][:L] + y_s[q][L:]
        ym = headsum(y) * (1.0 / HEAD)
        yc = y - ym
        yv = headsum(yc * yc) * (1.0 / HEAD)
        ls = lanes[q]
        o_ref[:, ls] = yc * lax.rsqrt(yv + GN_EPS) * gnw_ref[:, ls] + gnb_ref[:, ls] + ops[q]["bonus"]

    @pl.when(c == pl.num_programs(2) - 1)
    def _():
        for q in pairs:
            sout_ref[2 * q] = s_new[q][:HEAD, :HEAD]
            sout_ref[2 * q + 1] = s_new[q][HEAD:, HEAD:]


def _wkv(r, k, v, lw, a, kkw, kaw, rk, gnw, gnb, s0, *, batch, seq, row0, chunk, pairs_per_step):
    d = r.shape[1]
    pps = pairs_per_step
    width = pps * LANES
    nc = seq // chunk
    blk0 = row0 // chunk
    tok = pl.BlockSpec((chunk, width), lambda b, p, c: (blk0 + b * nc + c, p))
    out = pl.BlockSpec((chunk, width), lambda b, p, c: (b * nc + c, p))
    par = pl.BlockSpec((1, width), lambda b, p, c: (0, p))
    st = pl.BlockSpec((None, 2 * pps, HEAD, HEAD), lambda b, p, c: (b, p, 0, 0))
    has_init = s0 is not None
    return pl.pallas_call(
        functools.partial(_wkv_kernel, chunk=chunk, pairs_per_step=pps, has_init=has_init),
        grid=(batch, d // width, nc),
        in_specs=[tok] * 5 + [par] * 5 + ([st] if has_init else []),
        out_specs=[out, st],
        out_shape=[jax.ShapeDtypeStruct((batch * seq, d), F32),
                   jax.ShapeDtypeStruct((batch, d // HEAD, HEAD, HEAD), F32)],
        scratch_shapes=[pltpu.VMEM((pps, LANES, LANES), F32)],
        compiler_params=_params(("parallel", "parallel", "arbitrary")),
        name="wkv_l%d" % chunk,
    )(r, k, v, lw, a, kkw, kaw, rk, gnw, gnb, *([s0] if has_init else []))


def _proj_ln_kernel(op_ref, os_ref, g_ref, xa_ref, xb_ref, w_ref, lng_ref, lnb_ref, rw_ref, rb_ref, out_ref,
                    route_ref, cnt_ref, *, alpha, gated, prompt_tiles):
    is_prompt = pl.program_id(0) < prompt_tiles
    o = jnp.where(is_prompt, op_ref[...], os_ref[...])
    if gated:
        o = o * g_ref[...]
    y = _mm(o, w_ref[...])
    x = jnp.where(is_prompt, xa_ref[...], xb_ref[...])
    x1 = _layer_norm(alpha * x + y, lng_ref[...], lnb_ref[...])
    out_ref[...] = x1
    route, chosen = _route(x1, rw_ref, rb_ref)
    route_ref[...] = route

    @pl.when(pl.program_id(0) == 0)
    def _():
        cnt_ref[...] = jnp.zeros_like(cnt_ref)

    cnt_ref[...] += jnp.sum(chosen, axis=0, keepdims=True)


def _proj_ln(o_p, o_s, g, x_parts, w, lng, lnb, router_w, router_b, *, alpha, tm):
    x_a, x_b, sample_block0 = x_parts
    d = x_a.shape[1]
    kin = o_p.shape[1]
    npt = o_p.shape[0] // tm
    t = o_p.shape[0] + o_s.shape[0]
    gated = g is not None
    g_spec = pl.BlockSpec((tm, kin), lambda i: (i, 0))
    if not gated:
        g, g_spec = lng, pl.BlockSpec((1, d), lambda i: (0, 0))
    const = lambda arr: pl.BlockSpec(arr.shape, lambda i: (0, 0))
    return pl.pallas_call(
        functools.partial(_proj_ln_kernel, alpha=alpha, gated=gated, prompt_tiles=npt),
        grid=(t // tm,),
        in_specs=_part_specs(tm, kin, npt) + [g_spec] + _part_specs(tm, d, npt, sample_block0)
                 + [const(w), const(lng), const(lnb), const(router_w), const(router_b)],
        out_specs=[pl.BlockSpec((tm, d), lambda i: (i, 0)),
                   pl.BlockSpec((tm, LANES), lambda i: (i, 0)),
                   pl.BlockSpec((1, LANES), lambda i: (0, 0))],
        out_shape=[jax.ShapeDtypeStruct((t, d), F32),
                   jax.ShapeDtypeStruct((t, LANES), F32),
                   jax.ShapeDtypeStruct((1, LANES), F32)],
        compiler_params=_params(("arbitrary",)),
        name="proj_ln",
    )(o_p, o_s, g, x_a, x_b, w, lng, lnb, router_w, router_b)


def _route(x, w_ref, b_ref):
    ng, epg = EXPERT_GROUPS, EXPERTS_PER_GROUP
    x_hi = x.astype(BF16)
    x_lo = (x - x_hi.astype(F32)).astype(BF16)
    by_hi = jnp.dot(x_hi, w_ref[...], preferred_element_type=F32)
    logits = (by_hi[:, :LANES] + by_hi[:, LANES:]
              + jnp.dot(x_lo, w_ref[:, :LANES], preferred_element_type=F32)
              + b_ref[...])
    lane = _iota2(logits.shape, 1)
    neg = jnp.float32(-jnp.inf)
    big = jnp.int32(1 << 20)

    is_g = lane < ng
    lg = jnp.where(is_g, logits, neg)
    mg = jnp.max(lg, axis=-1, keepdims=True)
    g_sel = jnp.min(jnp.where(lg == mg, lane, big), axis=-1, keepdims=True)
    zg = jnp.sum(jnp.where(is_g, jnp.exp(lg - mg), 0.0), axis=-1, keepdims=True)
    p_group = 1.0 / zg

    e_lane = lane - ng
    in_sel = (e_lane >= g_sel * epg) & (e_lane < (g_sel + 1) * epg)
    le = jnp.where(in_sel, logits, neg)
    m1 = jnp.max(le, axis=-1, keepdims=True)
    i1 = jnp.min(jnp.where(le == m1, lane, big), axis=-1, keepdims=True)
    ze = jnp.sum(jnp.where(in_sel, jnp.exp(le - m1), 0.0), axis=-1, keepdims=True)
    le2 = jnp.where(lane == i1, neg, le)
    m2 = jnp.max(le2, axis=-1, keepdims=True)
    i2 = jnp.min(jnp.where(le2 == m2, lane, big), axis=-1, keepdims=True)
    pe1 = 1.0 / ze
    pe2 = jnp.exp(m2 - m1) / ze
    den = pe1 + pe2
    c1 = pe1 / den * p_group
    c2 = pe2 / den * p_group
    e1 = i1 - ng
    e2 = i2 - ng
    route = jnp.where(lane == 0, e1.astype(F32),
            jnp.where(lane == 1, e2.astype(F32),
            jnp.where(lane == 2, c1, jnp.where(lane == 3, c2, 0.0))))
    return route, ((lane == e1) | (lane == e2)).astype(F32)


def _router_terms(w_rg, b_rg, w_re, b_re):
    pad = LANES - w_rg.shape[1] - w_re.shape[1]
    w = jnp.pad(jnp.concatenate([w_rg, w_re], axis=1), ((0, 0), (0, pad)))
    b = jnp.pad(jnp.concatenate([b_rg, b_re]), (0, pad))[None, :]
    w_hi = w.astype(BF16)
    w_lo = (w - w_hi.astype(F32)).astype(BF16)
    return jnp.concatenate([w_hi, w_lo], axis=1), b


def _slots_kernel(route_ref, cnt_ref, pos_ref, ends_ref, off_sc, *, tile_rows):
    i = pl.program_id(0)
    route = route_ref[...]
    tm = route.shape[0]
    lane = _iota2((tm, LANES), 1)
    oh1 = lane == route[:, 0:1].astype(I32)
    oh2 = lane == route[:, 1:2].astype(I32)
    oh = (oh1 | oh2).astype(F32)

    @pl.when(i == 0)
    def _():
        padded = jnp.ceil(cnt_ref[...] * (1.0 / tile_rows)) * tile_rows
        before = (_iota2((LANES, LANES), 0) < _iota2((LANES, LANES), 1)).astype(F32)
        starts = _mm_sel_r(padded, before)
        off_sc[...] = starts
        ends_ref[...] = (starts + padded).astype(I32)

    earlier = (_iota2((tm, tm), 1) < _iota2((tm, tm), 0)).astype(F32)
    base = off_sc[...] + _mm(earlier, oh)
    p1 = jnp.sum(jnp.where(oh1, base, 0.0), axis=-1, keepdims=True)
    p2 = jnp.sum(jnp.where(oh2, base, 0.0), axis=-1, keepdims=True)
    pos_ref[...] = jnp.where(lane == 0, p1, jnp.where(lane == 1, p2, 0.0)).astype(I32)
    off_sc[...] += jnp.sum(oh, axis=0, keepdims=True)


def _slots(route, counts, tm, tile_rows):
    t = route.shape[0]
    return pl.pallas_call(
        functools.partial(_slots_kernel, tile_rows=tile_rows),
        grid=(t // tm,),
        in_specs=[pl.BlockSpec((tm, LANES), lambda i: (i, 0)),
                  pl.BlockSpec((1, LANES), lambda i: (0, 0))],
        out_specs=[pl.BlockSpec((tm, LANES), lambda i: (i, 0)),
                   pl.BlockSpec((1, LANES), lambda i: (0, 0))],
        out_shape=[jax.ShapeDtypeStruct((t, LANES), I32),
                   jax.ShapeDtypeStruct((1, LANES), I32)],
        scratch_shapes=[pltpu.VMEM((1, LANES), F32)],
        compiler_params=_params(("arbitrary",)),
        name="moe_slots",
    )(route, counts)


WIDE_TILE_FACTOR = 2
DMA_UNROLL = 8
COMBINE_CHUNKS = 8


def _scatter_kernel(pos_ref, te_ref, nu_ref, x_ref, xs_ref, zero_sc, sem, zsem, *, tile_rows):
    i = pl.program_id(0)
    tm = x_ref.shape[0]
    n_tiles = xs_ref.shape[0] // tile_rows

    @pl.when(i == 0)
    def _():
        zero_sc[...] = jnp.zeros_like(zero_sc)

        def zcopy(j):
            return pltpu.make_async_copy(zero_sc, xs_ref.at[pl.ds(j * tile_rows, tile_rows)], zsem)

        def fill(j, n):
            partial = (j >= nu_ref[0] - 1) | (te_ref[j] != te_ref[jnp.minimum(j + 1, n_tiles - 1)])

            @pl.when(partial)
            def _():
                zcopy(j).start()
            return n + partial.astype(I32)

        n_fill = lax.fori_loop(0, n_tiles, fill, 0)

        def drain(_, carry):
            zcopy(0).wait()
            return carry
        lax.fori_loop(0, n_fill, drain, 0)

    def copy(t, c):
        p = pos_ref[2 * (i * tm + t) + c]
        return pltpu.make_async_copy(x_ref.at[pl.ds(t, 1)], xs_ref.at[pl.ds(p, 1)], sem)

    def start(t, carry):
        copy(t, 0).start(priority=0)
        copy(t, 1).start(priority=1)
        return carry

    lax.fori_loop(0, tm, start, 0, unroll=DMA_UNROLL)
    for _ in range(2 * tm):
        copy(0, 0).wait()


def _scatter(pos_flat, tile_expert, n_used, x, n_slots, tm, tile_rows):
    t, d = x.shape
    grid_spec = pltpu.PrefetchScalarGridSpec(
        num_scalar_prefetch=3,
        grid=(t // tm,),
        in_specs=[pl.BlockSpec((tm, d), lambda i, pos, te, nu: (i, 0))],
        out_specs=pl.BlockSpec(memory_space=pl.ANY),
        scratch_shapes=[pltpu.VMEM((tile_rows, d), F32), pltpu.SemaphoreType.DMA(()),
                        pltpu.SemaphoreType.DMA(())],
    )
    return pl.pallas_call(
        functools.partial(_scatter_kernel, tile_rows=tile_rows),
        grid_spec=grid_spec,
        out_shape=jax.ShapeDtypeStruct((n_slots, d), F32),
        compiler_params=_params(("arbitrary",)),
        name="moe_scatter",
    )(pos_flat, tile_expert, n_used, x)


def _experts_kernel(te_ref, nx_ref, nu_ref, xs_ref, wg_ref, wu_ref, wd_ref, out_ref, wg_sc, wu_sc, wd_sc,
                    wg_in, wu_in, wd_in, sem, slot_ref, *, expert0):
    j = pl.program_id(0)
    used = j < nu_ref[0]
    prev = te_ref[jnp.maximum(j - 1, 0)]

    def fetches(e, s):
        return [pltpu.make_async_copy(hbm.at[expert0 + e], dst.at[s], sem.at[s])
                for hbm, dst in ((wg_ref, wg_in), (wu_ref, wu_in), (wd_ref, wd_in))]

    @pl.when(used & (j == 0))
    def _():
        slot_ref[0] = 0
        for cp in fetches(te_ref[0], 0):
            cp.start()

    @pl.when(used & ((j == 0) | (te_ref[j] != prev)))
    def _():
        s = slot_ref[0]
        for cp in fetches(te_ref[j], s):
            cp.wait()
        wg_sc[...] = wg_in[s].astype(BF16)
        wu_sc[...] = wu_in[s].astype(BF16)
        wd_sc[...] = wd_in[s].astype(BF16)
        nxt = nx_ref[j]

        @pl.when(nxt >= 0)
        def _():
            for cp in fetches(nxt, 1 - s):
                cp.start()
        slot_ref[0] = 1 - s

    @pl.when(used)
    def _():
        xb = xs_ref[...].astype(BF16)
        h = _silu(jnp.dot(xb, wg_sc[...], preferred_element_type=F32))
        h = h * jnp.dot(xb, wu_sc[...], preferred_element_type=F32)
        out_ref[...] = jnp.dot(h.astype(BF16), wd_sc[...], preferred_element_type=F32)

    @pl.when(jnp.logical_not(used))
    def _():
        out_ref[...] = jnp.zeros_like(out_ref)


def _experts(tile_expert, next_expert, n_used, xs, wg, wu, wd, tile_rows, expert0):
    n_slots, d = xs.shape
    f = wg.shape[-1]

    def rows(j, te, nx, nu):
        return (jnp.maximum(jnp.minimum(j, nu[0] - 1), 0), 0)

    hbm = pl.BlockSpec(memory_space=pl.ANY)
    grid_spec = pltpu.PrefetchScalarGridSpec(
        num_scalar_prefetch=3,
        grid=(n_slots // tile_rows,),
        in_specs=[pl.BlockSpec((tile_rows, d), rows), hbm, hbm, hbm],
        out_specs=pl.BlockSpec((tile_rows, d), lambda j, te, nx, nu: (j, 0)),
        scratch_shapes=[pltpu.VMEM((d, f), BF16), pltpu.VMEM((d, f), BF16), pltpu.VMEM((f, d), BF16),
                        pltpu.VMEM((2, d, f), F32), pltpu.VMEM((2, d, f), F32), pltpu.VMEM((2, f, d), F32),
                        pltpu.SemaphoreType.DMA((2,)), pltpu.SMEM((1,), I32)],
    )
    return pl.pallas_call(
        functools.partial(_experts_kernel, expert0=expert0),
        grid_spec=grid_spec,
        out_shape=jax.ShapeDtypeStruct((n_slots, d), F32),
        compiler_params=_params(("arbitrary",)),
        name="moe_experts",
    )(tile_expert, next_expert, n_used, xs, wg, wu, wd)


def _combine_kernel(pos_ref, x_ref, route_ref, lng_ref, lnb_ref, ys_ref, *rest, alpha, split_tiles):
    if split_tiles is None:
        out_ref, o_sc, sem = rest
    else:
        out_ref, out_s_ref, o_sc, sem = rest
    i = pl.program_id(0)
    n = pl.num_programs(0)
    tm = x_ref.shape[0]
    slot = i % 2
    spare = 1 - slot
    nxt = jnp.minimum(i + 1, n - 1)
    rows_per_chunk = tm // COMBINE_CHUNKS

    def copy(tile, sl, t, c):
        p = pos_ref[2 * (tile * tm + t) + c]
        return pltpu.make_async_copy(ys_ref.at[pl.ds(p, 1)], o_sc.at[sl, c, pl.ds(t, 1)], sem.at[sl])

    def wait_tile(tile, sl):
        for _ in range(2 * tm):
            copy(tile, sl, 0, 0).wait()

    @pl.when(i == 0)
    def _():
        def body(t, carry):
            copy(0, 0, t, 0).start(priority=0)
            copy(0, 0, t, 1).start(priority=1)
            return carry
        lax.fori_loop(0, tm, body, 0, unroll=DMA_UNROLL)

    wait_tile(i, slot)

    def mix_into(out):
        for k in range(COMBINE_CHUNKS):
            rows = slice(k * rows_per_chunk, (k + 1) * rows_per_chunk)
            for t in range(rows.start, rows.stop):
                copy(nxt, spare, t, 0).start(priority=0)
                copy(nxt, spare, t, 1).start(priority=1)
            y = route_ref[rows, 2:3] * o_sc[slot, 0, rows] + route_ref[rows, 3:4] * o_sc[slot, 1, rows]
            out[rows, :] = _layer_norm(alpha * x_ref[rows, :] + y, lng_ref[...], lnb_ref[...])

    if split_tiles is None:
        mix_into(out_ref)
    else:
        @pl.when(i < split_tiles)
        def _():
            mix_into(out_ref)

        @pl.when(i >= split_tiles)
        def _():
            mix_into(out_s_ref)

    @pl.when(i == n - 1)
    def _():
        wait_tile(nxt, spare)


def _combine(pos_flat, x, route, lng, lnb, ys, *, alpha, tm, split_rows=None):
    t, d = x.shape
    out_specs = pl.BlockSpec((tm, d), lambda i, pos: (i, 0))
    out_shape = jax.ShapeDtypeStruct((t, d), F32)
    npt = None
    if split_rows is not None:
        npt = split_rows // tm
        out_specs = [pl.BlockSpec((tm, d), lambda i, pos: (jnp.minimum(i, npt - 1), 0)),
                     pl.BlockSpec((tm, d), lambda i, pos: (jnp.maximum(i - npt, 0), 0))]
        out_shape = [jax.ShapeDtypeStruct((split_rows, d), F32),
                     jax.ShapeDtypeStruct((t - split_rows, d), F32)]
    grid_spec = pltpu.PrefetchScalarGridSpec(
        num_scalar_prefetch=1,
        grid=(t // tm,),
        in_specs=[pl.BlockSpec((tm, d), lambda i, pos: (i, 0)),
                  pl.BlockSpec((tm, LANES), lambda i, pos: (i, 0)),
                  pl.BlockSpec((1, d), lambda i, pos: (0, 0)),
                  pl.BlockSpec((1, d), lambda i, pos: (0, 0)),
                  pl.BlockSpec(memory_space=pl.ANY)],
        out_specs=out_specs,
        scratch_shapes=[pltpu.VMEM((2, 2, tm, d), F32), pltpu.SemaphoreType.DMA((2,))],
    )
    return pl.pallas_call(
        functools.partial(_combine_kernel, alpha=alpha, split_tiles=npt),
        grid_spec=grid_spec,
        out_shape=out_shape,
        compiler_params=_params(("arbitrary",)),
        name="moe_combine",
    )(pos_flat, x, route, lng, lnb, ys)


def _hier_moe_ln(x, route, counts, w_gate, w_up, w_down, lng, lnb, *, alpha, tm, tile_rows,
                 layer=0, split_rows=None):
    t, d = x.shape
    n_exp = EXPERT_GROUPS * EXPERTS_PER_GROUP
    pos, ends = _slots(route, counts, WIDE_TILE_FACTOR * tm, tile_rows)
    pos_flat = pos[:, :2].reshape(-1)
    n_slots = 2 * t + n_exp * tile_rows
    n_tiles = n_slots // tile_rows
    seg_end = ends[0, :n_exp]
    tile_start = jnp.arange(n_tiles, dtype=I32) * tile_rows
    tile_expert = jnp.minimum(jnp.sum(tile_start[:, None] >= seg_end[None, :], axis=1), n_exp - 1).astype(I32)
    n_used = (seg_end[n_exp - 1:] // tile_rows).astype(I32)
    after = seg_end[tile_expert] // tile_rows
    next_expert = jnp.where(after < n_used[0], tile_expert[jnp.minimum(after, n_tiles - 1)], -1).astype(I32)
    xs = _scatter(pos_flat, tile_expert, n_used, x, n_slots, tm, tile_rows)
    f = w_gate.shape[-1]
    ys = _experts(tile_expert, next_expert, n_used, xs, w_gate.reshape(-1, d, f), w_up.reshape(-1, d, f),
                  w_down.reshape(-1, f, d), tile_rows, layer * n_exp)
    return _combine(pos_flat, x, route, lng, lnb, ys, alpha=alpha, tm=tm, split_rows=split_rows)


def _ssd_in_kernel(x_ref, wz_ref, wx_ref, wdt_ref, z_ref, xbc_ref, dt_ref):
    xb = x_ref[...].astype(BF16)
    z_ref[...] = jnp.dot(xb, wz_ref[...], preferred_element_type=F32)
    xbc_ref[...] = jnp.dot(xb, wx_ref[...], preferred_element_type=F32)
    dt_ref[...] = jnp.dot(xb, wdt_ref[...], preferred_element_type=F32)


def _ssd_in(x, wz, wx, wdt, tm):
    t, d = x.shape
    outs = [wz.shape[1], wx.shape[1], wdt.shape[1]]
    return pl.pallas_call(
        _ssd_in_kernel,
        grid=(t // tm,),
        in_specs=[pl.BlockSpec((tm, d), lambda i: (i, 0))]
                 + [pl.BlockSpec(w.shape, lambda i: (0, 0)) for w in (wz, wx, wdt)],
        out_specs=[pl.BlockSpec((tm, n), lambda i: (i, 0)) for n in outs],
        out_shape=[jax.ShapeDtypeStruct((t, n), F32) for n in outs],
        compiler_params=_params(("parallel",)),
        name="ssd_in",
    )(x, wz, wx, wdt)


def _ssd_kernel(xbc_ref, dt_ref, z_ref, conv0_ref, h0_ref, cw_ref, cb_ref, dtb_ref, alog_ref,
                dexp_ref, nw_ref, expand_ref, y_ref, convo_ref, hout_ref, pad_sc, ht_sc, *, chunk):
    c = chunk
    ci = pl.program_id(1)
    d_inner = z_ref.shape[1]
    gn = SSD_GROUPS * D_STATE
    per_group = d_inner // SSD_GROUPS
    tail = 8

    @pl.when(ci == 0)
    def _():
        pad_sc[0:tail, :] = conv0_ref[...]
        ht_sc[...] = h0_ref[...].T

    pad_sc[tail:tail + c, :] = xbc_ref[...]
    acc = cb_ref[...]
    for i in range(CONV_WIDTH):
        off = tail - (CONV_WIDTH - 1) + i
        acc = acc + pad_sc[off:off + c, :] * cw_ref[i:i + 1, :]
    xbc = _silu(acc)
    last_rows = pad_sc[c:c + tail, :]
    pad_sc[0:tail, :] = last_rows
    xs = xbc[:, :d_inner]

    dt = _softplus(dt_ref[...] + dtb_ref[...])
    da = dt * (-jnp.exp(alog_ref[...]))
    tril_incl = (_iota2((c, c), 1) <= _iota2((c, c), 0)).astype(F32)
    acum = _mm_sel_l(tril_incl, da)
    a_last = acum[c - 1:c, :]

    def to_cols(t):
        if c < LANES:
            t = jnp.concatenate([t, jnp.zeros((LANES - c, LANES), F32)], axis=0)
        return t.T[:, :c]

    acum_t = to_cols(acum)
    expand = expand_ref[...]
    per_head = jnp.concatenate([jnp.exp(acum), jnp.exp(a_last - acum) * dt, dt], axis=0)
    per_chan = _mm_sel_r(per_head, expand)
    ea, dd = per_chan[:c], per_chan[c:2 * c]
    x_dt = xs * per_chan[2 * c:]
    ea_last = _mm_sel_r(jnp.exp(a_last), expand)
    causal = _iota2((c, c), 1) <= _iota2((c, c), 0)
    lane = _iota2((1, LANES), 1)
    m0 = (lane < HEAD).astype(F32)
    m1 = 1.0 - m0

    ys = []
    for g in range(SSD_GROUPS):
        b_g = xbc[:, d_inner + g * D_STATE:d_inner + (g + 1) * D_STATE]
        c_g = xbc[:, d_inner + gn + g * D_STATE:d_inner + gn + (g + 1) * D_STATE]
        cb = _mm_nt(c_g, b_g)
        gsl = slice(g * per_group, (g + 1) * per_group)
        ht_g = ht_sc[:, gsl]
        y_off = _mm(c_g, ht_g) * ea[:, gsl]
        ht_sc[:, gsl] = ht_g * ea_last[:, gsl] + _mm_tn(b_g, xs[:, gsl] * dd[:, gsl])
        pieces = []
        for q in range(per_group // LANES):
            j0 = (g * per_group) // HEAD + 2 * q
            ms = []
            for j in (j0, j0 + 1):
                seg = acum[:, j:j + 1] - acum_t[j:j + 1, :]
                ms.append(jnp.where(causal, cb * jnp.exp(seg), 0.0))
            lo = g * per_group + q * LANES
            x_pair = x_dt[:, lo:lo + LANES]
            x_stack = jnp.concatenate([x_pair * m0, x_pair * m1], axis=0)
            pieces.append(_mm(jnp.concatenate(ms, axis=1), x_stack))
        y_g = jnp.concatenate(pieces, axis=1) + y_off + xs[:, gsl] * dexp_ref[:, gsl]
        y_g = y_g * _silu(z_ref[:, gsl])
        y_g = y_g * lax.rsqrt(jnp.mean(y_g * y_g, axis=-1, keepdims=True) + RMS_EPS)
        ys.append(y_g * nw_ref[:, gsl])
    y_ref[...] = jnp.concatenate(ys, axis=1)

    @pl.when(ci == pl.num_programs(1) - 1)
    def _():
        convo_ref[...] = last_rows
        hout_ref[...] = ht_sc[...].T


def _ssd(xbc, dt, z, conv0, h0, cw, cb, dtb, alog, dexp, nw, expand, *, batch, seq, row0, chunk):
    d_inner = z.shape[1]
    conv_dim = xbc.shape[1]
    nc = seq // chunk
    blk0 = row0 // chunk

    def tok(n):
        return pl.BlockSpec((chunk, n), lambda b, c: (blk0 + b * nc + c, 0))

    def par(arr):
        return pl.BlockSpec(arr.shape, lambda b, c: (0, 0))

    return pl.pallas_call(
        functools.partial(_ssd_kernel, chunk=chunk),
        grid=(batch, nc),
        in_specs=[tok(conv_dim), tok(LANES), tok(d_inner),
                  pl.BlockSpec((None, 8, conv_dim), lambda b, c: (b, 0, 0)),
                  pl.BlockSpec((None, d_inner, D_STATE), lambda b, c: (b, 0, 0)),
                  par(cw), par(cb), par(dtb), par(alog), par(dexp), par(nw), par(expand)],
        out_specs=[pl.BlockSpec((chunk, d_inner), lambda b, c: (b * nc + c, 0)),
                   pl.BlockSpec((None, 8, conv_dim), lambda b, c: (b, 0, 0)),
                   pl.BlockSpec((None, d_inner, D_STATE), lambda b, c: (b, 0, 0))],
        out_shape=[jax.ShapeDtypeStruct((batch * seq, d_inner), F32),
                   jax.ShapeDtypeStruct((batch, 8, conv_dim), F32),
                   jax.ShapeDtypeStruct((batch, d_inner, D_STATE), F32)],
        scratch_shapes=[pltpu.VMEM((chunk + 8, conv_dim), F32), pltpu.VMEM((D_STATE, d_inner), F32)],
        compiler_params=_params(("parallel", "arbitrary")),
        name="ssd_l%d" % chunk,
    )(xbc, dt, z, conv0, h0, cw, cb, dtb, alog, dexp, nw, expand)


def _rwkv_layer(xp3, xs3, shift_s, wkv_s, prm, lng, lnb, router, *, alpha, tm):
    (mu, w_rkv, w0, w1, w2, a0, a1, a2, g1, g2, k_k, k_a, r_k, gn_w, gn_b, w_o) = prm
    bp, sp, d = xp3.shape
    bs, ss, _ = xs3.shape
    tp = bp * sp
    heads = d // HEAD
    x_p = xp3.reshape(tp, d)
    x_s = xs3.reshape(bs * ss, d)
    prev_s = jnp.concatenate([shift_s[:, None, :], xs3[:, :-1]], axis=1).reshape(bs * ss, d)
    row = lambda v: v.reshape(1, d)
    r, k, v, lw, a, g = _rwkv_pre(x_p, x_s, prev_s, sp, mu, w_rkv.astype(BF16), row(w0), w1.astype(BF16),
                                  w2.astype(BF16), row(a0), a1.astype(BF16), a2.astype(BF16),
                                  g1.astype(BF16), g2.astype(BF16), tm)
    pv = (row(k_k), row(k_a), row(r_k), row(gn_w), row(gn_b))
    o_p, sp_out = _wkv(r, k, v, lw, a, *pv, None, batch=bp, seq=sp, row0=0,
                       chunk=min(WKV_CHUNK, sp), pairs_per_step=min(WKV_PAIRS_PER_STEP, heads // 2))
    o_s, ss_out = _wkv(r, k, v, lw, a, *pv, wkv_s, batch=bs, seq=ss, row0=tp,
                       chunk=min(WKV_CHUNK, ss), pairs_per_step=heads // 2)
    routed = _proj_ln(o_p, o_s, g, (x_p, x_s, 0), w_o.astype(BF16), lng, lnb, *router, alpha=alpha,
                      tm=WIDE_TILE_FACTOR * tm)
    return routed, sp_out, ss_out


def _ssd_layer(x, bp, sp, bs, ss, conv_s, ssm_s, prm, lng, lnb, router, *, alpha, tm):
    (w_in, conv_w, conv_b, dt_bias, a_log, d_skip, norm_w, w_out) = prm
    t, d = x.shape
    tp = bp * sp
    heads = a_log.shape[0]
    d_inner = heads * HEAD
    conv_dim = conv_w.shape[1]
    wz = w_in[:, :d_inner].astype(BF16)
    wx = w_in[:, d_inner:d_inner + conv_dim].astype(BF16)
    wdt = jnp.pad(w_in[:, d_inner + conv_dim:], ((0, 0), (0, LANES - heads))).astype(BF16)
    wide = WIDE_TILE_FACTOR * tm
    z, xbc, dt = _ssd_in(x, wz, wx, wdt, wide)
    lane_pad = lambda v: jnp.pad(v, (0, LANES - heads))[None, :]
    expand = (jnp.arange(LANES)[:, None] == (jnp.arange(d_inner) // HEAD)[None, :]).astype(F32)
    common = (conv_w, conv_b[None, :], lane_pad(dt_bias), lane_pad(a_log),
              jnp.repeat(d_skip, HEAD)[None, :], norm_w[None, :], expand)
    conv_tail = CONV_WIDTH - 1
    pad_conv = lambda cs: jnp.pad(cs, ((0, 0), (8 - conv_tail, 0), (0, 0)))
    y_p, conv_p, h_p = _ssd(xbc, dt, z, jnp.zeros((bp, 8, conv_dim), F32),
                            jnp.zeros((bp, d_inner, D_STATE), F32), *common,
                            batch=bp, seq=sp, row0=0, chunk=SSD_CHUNK if sp % SSD_CHUNK == 0 else sp)
    y_s, conv_s_out, h_s = _ssd(xbc, dt, z, pad_conv(conv_s), ssm_s.reshape(bs, d_inner, D_STATE), *common,
                                batch=bs, seq=ss, row0=tp, chunk=SSD_CHUNK if ss % SSD_CHUNK == 0 else ss)
    routed = _proj_ln(y_p, y_s, None, (x, x, tp // wide), w_out.astype(BF16), lng, lnb, *router, alpha=alpha,
                      tm=wide)
    shape_h = lambda h, b: h.reshape(b, heads, HEAD, D_STATE)
    return (routed, conv_p[:, 8 - conv_tail:], conv_s_out[:, 8 - conv_tail:], shape_h(h_p, bp), shape_h(h_s, bs))


def kernel(x_prompt, x_sample, state_rwkv_wkv, state_rwkv_shift, state_ssd_ssm, state_ssd_conv,
           rwkv_mu, rwkv_w_rkv, rwkv_w0, rwkv_w1, rwkv_w2, rwkv_a0, rwkv_a1, rwkv_a2,
           rwkv_g1, rwkv_g2, rwkv_k_k, rwkv_k_a, rwkv_r_k, rwkv_gn_w, rwkv_gn_b, rwkv_w_o,
           ssd_w_in, ssd_conv_w, ssd_conv_b, ssd_dt_bias, ssd_a_log, ssd_d, ssd_norm_w, ssd_w_out,
           ln_gain, ln_bias, moe_w_rg, moe_b_rg, moe_w_re, moe_b_re, moe_w_gate, moe_w_up, moe_w_down):
    bp, sp, d = x_prompt.shape
    bs, ss, _ = x_sample.shape
    depth = ln_gain.shape[0]
    alpha = (2 * depth) ** 0.25
    tp, ts = bp * sp, bs * ss
    tm = 256
    tile_rows = 256
    x = None
    xp3, xs3 = x_prompt, x_sample
    wkv_p, wkv_s, sh_p, sh_s, ssm_p, ssm_s, cv_p, cv_s = [], [], [], [], [], [], [], []
    for i in range(depth):
        j = i // 2
        lng = lambda n: ln_gain[i, n][None, :]
        lnb = lambda n: ln_bias[i, n][None, :]
        router = _router_terms(moe_w_rg[i], moe_b_rg[i], moe_w_re[i], moe_b_re[i])
        if i % 2 == 0:
            if x is not None:
                xp3, xs3 = x[:tp].reshape(bp, sp, d), x[tp:].reshape(bs, ss, d)
            prm = (rwkv_mu[j], rwkv_w_rkv[j], rwkv_w0[j], rwkv_w1[j], rwkv_w2[j], rwkv_a0[j],
                   rwkv_a1[j], rwkv_a2[j], rwkv_g1[j], rwkv_g2[j], rwkv_k_k[j], rwkv_k_a[j],
                   rwkv_r_k[j].reshape(-1), rwkv_gn_w[j], rwkv_gn_b[j], rwkv_w_o[j])
            sh_p.append(xp3[:, -1])
            sh_s.append(xs3[:, -1])
            routed, w_p, w_s = _rwkv_layer(xp3, xs3, state_rwkv_shift[j], state_rwkv_wkv[j], prm,
                                           lng(0), lnb(0), router, alpha=alpha, tm=tm)
            wkv_p.append(w_p)
            wkv_s.append(w_s)
        else:
            prm = (ssd_w_in[j], ssd_conv_w[j], ssd_conv_b[j], ssd_dt_bias[j], ssd_a_log[j],
                   ssd_d[j], ssd_norm_w[j], ssd_w_out[j])
            routed, c_p, c_s, h_p, h_s = _ssd_layer(x, bp, sp, bs, ss, state_ssd_conv[j], state_ssd_ssm[j],
                                                    prm, lng(0), lnb(0), router, alpha=alpha, tm=tm)
            cv_p.append(c_p)
            cv_s.append(c_s)
            ssm_p.append(h_p)
            ssm_s.append(h_s)
        x = _hier_moe_ln(*routed, moe_w_gate, moe_w_up, moe_w_down, lng(1), lnb(1), alpha=alpha, tm=tm,
                         tile_rows=tile_rows, layer=i, split_rows=tp if i == depth - 1 else None)
    yp = x[0].reshape(bp, sp, d)
    ys = x[1].reshape(bs, ss, d)
    return (yp, ys, jnp.stack(wkv_p), jnp.stack(wkv_s), jnp.stack(sh_p), jnp.stack(sh_s),
            jnp.stack(ssm_p), jnp.stack(ssm_s), jnp.stack(cv_p), jnp.stack(cv_s))
```

```python
import functools
import math

import jax
import jax.numpy as jnp
from jax import lax
from jax.experimental import pallas as pl
from jax.experimental.pallas import tpu as pltpu

F32 = jnp.float32
BF16 = jnp.bfloat16
I32 = jnp.int32

LANES = 128
HEAD = 64
GN_EPS = HEAD * 1e-5
RMS_EPS = 1e-5
LN_EPS = 1e-5
WKV_CHUNK = 64
WKV_PAIRS_PER_STEP = 8
WKV_SHORT_SEQS_PER_STEP = 4
SSD_CHUNK = 128
SSD_GROUPS = 4
D_STATE = 128
CONV_WIDTH = 4
EXPERT_GROUPS = 4
EXPERTS_PER_GROUP = 8
VMEM_LIMIT = 56 * 1024 * 1024


def _params(sem):
    return pltpu.CompilerParams(dimension_semantics=sem, vmem_limit_bytes=VMEM_LIMIT)


def _mm(a, b):
    return jnp.dot(a.astype(BF16), b.astype(BF16), preferred_element_type=F32)


def _mm_nt(a, b):
    return lax.dot_general(a.astype(BF16), b.astype(BF16), (((1,), (1,)), ((), ())),
                           preferred_element_type=F32)


def _mm_tn(a, b):
    return lax.dot_general(a.astype(BF16), b.astype(BF16), (((0,), (0,)), ((), ())),
                           preferred_element_type=F32)


def _split3(x):
    hi = x.astype(BF16)
    r1 = x - hi.astype(F32)
    mid = r1.astype(BF16)
    lo = (r1 - mid.astype(F32)).astype(BF16)
    return hi, mid, lo


def _mm_sel_l(sel, x):
    k = x.shape[1]
    parts = jnp.concatenate([p.astype(F32) for p in _split3(x)], axis=1).astype(BF16)
    y = jnp.dot(sel.astype(BF16), parts, preferred_element_type=F32)
    return y[:, :k] + y[:, k:2 * k] + y[:, 2 * k:]


def _mm_sel_r(x, sel):
    n = x.shape[0]
    if n % 8:
        return sum(jnp.dot(p, sel.astype(BF16), preferred_element_type=F32) for p in _split3(x))
    parts = jnp.concatenate([p.astype(F32) for p in _split3(x)], axis=0).astype(BF16)
    y = jnp.dot(parts, sel.astype(BF16), preferred_element_type=F32)
    return y[:n] + y[n:2 * n] + y[2 * n:]


def _sigmoid(x):
    return 1.0 / (1.0 + jnp.exp(-x))


def _silu(x):
    return x * _sigmoid(x)


def _softplus(x):
    return jnp.maximum(x, 0.0) + jnp.log(1.0 + jnp.exp(-jnp.abs(x)))


def _layer_norm(x, g, b):
    mu = jnp.mean(x, axis=-1, keepdims=True)
    xc = x - mu
    var = jnp.mean(xc * xc, axis=-1, keepdims=True)
    return xc * lax.rsqrt(var + LN_EPS) * g + b


def _iota2(shape, dim):
    return lax.broadcasted_iota(I32, shape, dim)


def _part_specs(tm, width, prompt_tiles, sample_block0=0):
    return [pl.BlockSpec((tm, width), lambda i: (jnp.minimum(i, prompt_tiles - 1), 0)),
            pl.BlockSpec((tm, width), lambda i: (sample_block0 + jnp.maximum(i - prompt_tiles, 0), 0))]


def _rwkv_pre_kernel(xa_ref, xb_ref, tail_ref, xprev_s_ref, mu_ref, wrkv_ref, w0_ref, w1_ref, w2_ref,
                     a0_ref, a1_ref, a2_ref, g1_ref, g2_ref, r_ref, k_ref, v_ref, lw_ref, a_ref, g_ref,
                     *, prompt_tiles, seq):
    i = pl.program_id(0)
    tm = xa_ref.shape[0]
    is_prompt = i < prompt_tiles
    xa = xa_ref[...]
    x = jnp.where(is_prompt, xa, xb_ref[...])
    first = jnp.where((i * tm) % seq == 0, 0.0, tail_ref[7:8, :])
    shifted = jnp.where(_iota2(xa.shape, 0) == 0, first, pltpu.roll(xa, 1, axis=0))
    xx = jnp.where(is_prompt, shifted, xprev_s_ref[...]) - x

    def mix(j):
        return x + xx * mu_ref[j:j + 1, :]

    r_ref[...] = _mm(mix(0), wrkv_ref[0])
    k_ref[...] = _mm(mix(1), wrkv_ref[1])
    v_ref[...] = _mm(mix(2), wrkv_ref[2])
    ww = w0_ref[...] + _mm(jnp.tanh(_mm(mix(3), w1_ref[...])), w2_ref[...])
    w_raw = -_softplus(-ww) - 0.5
    lw_ref[...] = -jnp.exp(w_raw)
    a_ref[...] = _sigmoid(a0_ref[...] + _mm(_mm(mix(4), a1_ref[...]), a2_ref[...]))
    g_ref[...] = _mm(_sigmoid(_mm(mix(5), g1_ref[...])), g2_ref[...])


def _rwkv_pre(x_p, x_s, xprev_s, seq, mu, wrkv, w0, w1, w2, a0, a1, a2, g1, g2, tm):
    (tp, d), ts = x_p.shape, x_s.shape[0]
    assert seq % tm == 0 and ts % tm == 0 and tm % 8 == 0
    npt = tp // tm
    row = pl.BlockSpec((tm, d), lambda i: (i, 0))
    tail = pl.BlockSpec((8, d), lambda i: (jnp.maximum(jnp.minimum(i, npt - 1) * (tm // 8) - 1, 0), 0))

    def full(arr):
        nd = arr.ndim
        return pl.BlockSpec(arr.shape, lambda i, _n=nd: (0,) * _n)

    weights = (mu, wrkv, w0, w1, w2, a0, a1, a2, g1, g2)
    return pl.pallas_call(
        functools.partial(_rwkv_pre_kernel, prompt_tiles=npt, seq=seq),
        grid=((tp + ts) // tm,),
        in_specs=_part_specs(tm, d, npt) + [tail, _part_specs(tm, d, npt)[1]] + [full(w) for w in weights],
        out_specs=[row] * 6,
        out_shape=[jax.ShapeDtypeStruct((tp + ts, d), F32)] * 6,
        compiler_params=_params(("parallel",)),
        name="rwkv_pre",
    )(x_p, x_s, x_p, xprev_s, *weights)


def _wkv_kernel(r_ref, k_ref, v_ref, lw_ref, a_ref, kkw_ref, kaw_ref, rk_ref, gnw_ref, gnb_ref,
                *rest, chunk, pairs_per_step, seqs_per_step, has_init):
    if has_init:
        s0_ref, o_ref, sout_ref, s_sc = rest
    else:
        o_ref, sout_ref, s_sc = rest
    L = chunk
    c = pl.program_id(2)
    units = [(s, q) for s in range(seqs_per_step) for q in range(pairs_per_step)]

    @pl.when(c == 0)
    def _():
        if has_init:
            zero = jnp.zeros((HEAD, HEAD), F32)
            for u, (s, q) in enumerate(units):
                top = jnp.concatenate([s0_ref[s, 2 * q], zero], axis=1)
                bot = jnp.concatenate([zero, s0_ref[s, 2 * q + 1]], axis=1)
                s_sc[u] = jnp.concatenate([top, bot], axis=0)
        else:
            s_sc[...] = jnp.zeros_like(s_sc)

    lane = _iota2((1, LANES), 1)
    m0 = (lane < HEAD).astype(F32)
    m1 = 1.0 - m0
    same_head = (_iota2((LANES, LANES), 0) // HEAD == _iota2((LANES, LANES), 1) // HEAD).astype(F32)
    tril_incl = (_iota2((L, L), 1) <= _iota2((L, L), 0)).astype(F32)
    n2 = 2 * L
    row = _iota2((n2, n2), 0)
    col = _iota2((n2, n2), 1)
    strict = col < row
    incl = col <= row
    steps = int(math.log2(L))

    def headsum(t):
        return _mm_sel_r(t, same_head)

    def stack(t):
        return jnp.concatenate([t * m0, t * m1], axis=0)

    pairs = range(len(units))
    rows = [slice(s * L, (s + 1) * L) for s, _ in units]
    lanes = [slice(q * LANES, (q + 1) * LANES) for _, q in units]

    def prepare(rs, ls):
        r = r_ref[rs, ls]
        k = k_ref[rs, ls]
        v = v_ref[rs, ls]
        lw = lw_ref[rs, ls]
        a = a_ref[rs, ls]
        kk = k * kkw_ref[:, ls]
        k2 = k * (1.0 + (a - 1.0) * kaw_ref[:, ls])
        sums = headsum(jnp.concatenate([kk * kk, r * k2 * rk_ref[:, ls]], axis=0))
        kk = kk / jnp.maximum(jnp.sqrt(sums[:L]), 1e-12)
        bonus = sums[L:] * v
        bvec = kk * a
        cl = _mm_sel_l(tril_incl, lw)
        total = cl[L - 1:L, :]
        w_in = jnp.exp(cl)
        w_inv = jnp.exp(-cl)
        w_end = jnp.exp(total - cl)
        return dict(
            ar=jnp.concatenate([stack(-kk * jnp.exp(cl - lw)), stack(r * w_in)], axis=0).astype(BF16),
            bk=jnp.concatenate([stack(bvec * w_inv), stack(k2 * w_inv)], axis=0).astype(BF16),
            bk_end=jnp.concatenate([stack(bvec * w_end), stack(k2 * w_end)], axis=0),
            v_s=stack(v), total=total, bonus=bonus)

    ops = [prepare(rs, ls) for rs, ls in zip(rows, lanes)]
    states = [s_sc[q] for q in pairs]
    cross = [_mm_nt(o["ar"], o["bk"]) for o in ops]
    from_s = [_mm_nt(o["ar"], s) for o, s in zip(ops, states)]
    n_ab = [jnp.where(strict, x[:n2, :n2], 0.0) for x in cross]
    a_rb = [jnp.where(incl, x[n2:, :n2], 0.0) for x in cross]
    to_v = [jnp.concatenate([jnp.where(strict, x[:n2, n2:], 0.0), jnp.where(incl, x[n2:, n2:], 0.0)], axis=0)
            for x in cross]
    from_v = [_mm(m, o["v_s"]) for m, o in zip(to_v, ops)]

    u = [fs[:n2] + fv[:n2] for fs, fv in zip(from_s, from_v)]
    p = n_ab
    for j in range(steps):
        if j + 1 < steps:
            both = [_mm(pi, jnp.concatenate([ui, pi], axis=1)) for ui, pi in zip(u, p)]
            u = [ui + b[:, :LANES] for ui, b in zip(u, both)]
            p = [b[:, LANES:] for b in both]
        else:
            u = [ui + _mm(pi, ui) for ui, pi in zip(u, p)]

    y_s = [fs[n2:] + fv[n2:] + _mm(mb, ui) for fs, fv, mb, ui in zip(from_s, from_v, a_rb, u)]
    s_new = [s * jnp.exp(o["total"]) + _mm_tn(jnp.concatenate([ui, o["v_s"]], axis=0), o["bk_end"])
             for o, s, ui in zip(ops, states, u)]
    for q in pairs:
        s_sc[q] = s_new[q]
        y = y_s[q][:L] + y_s[q][L:]
        ym = headsum(y) * (1.0 / HEAD)
        yc = y - ym
        yv = headsum(yc * yc) * (1.0 / HEAD)
        rs, ls = rows[q], lanes[q]
        o_ref[rs, ls] = yc * lax.rsqrt(yv + GN_EPS) * gnw_ref[:, ls] + gnb_ref[:, ls] + ops[q]["bonus"]

    @pl.when(c == pl.num_programs(2) - 1)
    def _():
        for u, (s, q) in enumerate(units):
            sout_ref[s, 2 * q] = s_new[u][:HEAD, :HEAD]
            sout_ref[s, 2 * q + 1] = s_new[u][HEAD:, HEAD:]


def _wkv(r, k, v, lw, a, kkw, kaw, rk, gnw, gnb, s0, *, batch, seq, row0, chunk, pairs_per_step,
         seqs_per_step=1):
    d = r.shape[1]
    pps, sps = pairs_per_step, seqs_per_step
    width = pps * LANES
    nc = seq // chunk
    assert batch % sps == 0 and (sps == 1 or nc == 1) and row0 % (sps * chunk) == 0
    blk0 = row0 // (sps * chunk)
    tok = pl.BlockSpec((sps * chunk, width), lambda b, p, c: (blk0 + b * nc + c, p))
    out = pl.BlockSpec((sps * chunk, width), lambda b, p, c: (b * nc + c, p))
    par = pl.BlockSpec((1, width), lambda b, p, c: (0, p))
    st = pl.BlockSpec((sps, 2 * pps, HEAD, HEAD), lambda b, p, c: (b, p, 0, 0))
    has_init = s0 is not None
    return pl.pallas_call(
        functools.partial(_wkv_kernel, chunk=chunk, pairs_per_step=pps, seqs_per_step=sps,
                          has_init=has_init),
        grid=(batch // sps, d // width, nc),
        in_specs=[tok] * 5 + [par] * 5 + ([st] if has_init else []),
        out_specs=[out, st],
        out_shape=[jax.ShapeDtypeStruct((batch * seq, d), F32),
                   jax.ShapeDtypeStruct((batch, d // HEAD, HEAD, HEAD), F32)],
        scratch_shapes=[pltpu.VMEM((sps * pps, LANES, LANES), F32)],
        compiler_params=_params(("parallel", "parallel", "arbitrary")),
        name="wkv_l%d" % chunk,
    )(r, k, v, lw, a, kkw, kaw, rk, gnw, gnb, *([s0] if has_init else []))


def _proj_ln_kernel(op_ref, os_ref, g_ref, xa_ref, xb_ref, w_ref, lng_ref, lnb_ref, rw_ref, rb_ref, out_ref,
                    route_ref, cnt_ref, *, alpha, gated, prompt_tiles):
    is_prompt = pl.program_id(0) < prompt_tiles
    o = jnp.where(is_prompt, op_ref[...], os_ref[...])
    if gated:
        o = o * g_ref[...]
    y = _mm(o, w_ref[...])
    x = jnp.where(is_prompt, xa_ref[...], xb_ref[...])
    x1 = _layer_norm(alpha * x + y, lng_ref[...], lnb_ref[...])
    out_ref[...] = x1
    route, chosen = _route(x1, rw_ref, rb_ref)
    route_ref[...] = route

    @pl.when(pl.program_id(0) == 0)
    def _():
        cnt_ref[...] = jnp.zeros_like(cnt_ref)

    cnt_ref[...] += jnp.sum(chosen, axis=0, keepdims=True)


def _proj_ln(o_p, o_s, g, x_parts, w, lng, lnb, router_w, router_b, *, alpha, tm):
    x_a, x_b, sample_block0 = x_parts
    d = x_a.shape[1]
    kin = o_p.shape[1]
    npt = o_p.shape[0] // tm
    t = o_p.shape[0] + o_s.shape[0]
    gated = g is not None
    g_spec = pl.BlockSpec((tm, kin), lambda i: (i, 0))
    if not gated:
        g, g_spec = lng, pl.BlockSpec((1, d), lambda i: (0, 0))
    const = lambda arr: pl.BlockSpec(arr.shape, lambda i: (0, 0))
    return pl.pallas_call(
        functools.partial(_proj_ln_kernel, alpha=alpha, gated=gated, prompt_tiles=npt),
        grid=(t // tm,),
        in_specs=_part_specs(tm, kin, npt) + [g_spec] + _part_specs(tm, d, npt, sample_block0)
                 + [const(w), const(lng), const(lnb), const(router_w), const(router_b)],
        out_specs=[pl.BlockSpec((tm, d), lambda i: (i, 0)),
                   pl.BlockSpec((tm, LANES), lambda i: (i, 0)),
                   pl.BlockSpec((1, LANES), lambda i: (0, 0))],
        out_shape=[jax.ShapeDtypeStruct((t, d), F32),
                   jax.ShapeDtypeStruct((t, LANES), F32),
                   jax.ShapeDtypeStruct((1, LANES), F32)],
        compiler_params=_params(("arbitrary",)),
        name="proj_ln",
    )(o_p, o_s, g, x_a, x_b, w, lng, lnb, router_w, router_b)


def _route(x, w_ref, b_ref):
    ng, epg = EXPERT_GROUPS, EXPERTS_PER_GROUP
    x_hi = x.astype(BF16)
    x_lo = (x - x_hi.astype(F32)).astype(BF16)
    by_hi = jnp.dot(x_hi, w_ref[...], preferred_element_type=F32)
    logits = (by_hi[:, :LANES] + by_hi[:, LANES:]
              + jnp.dot(x_lo, w_ref[:, :LANES], preferred_element_type=F32)
              + b_ref[...])
    lane = _iota2(logits.shape, 1)
    neg = jnp.float32(-jnp.inf)
    big = jnp.int32(1 << 20)

    is_g = lane < ng
    lg = jnp.where(is_g, logits, neg)
    mg = jnp.max(lg, axis=-1, keepdims=True)
    g_sel = jnp.min(jnp.where(lg == mg, lane, big), axis=-1, keepdims=True)
    zg = jnp.sum(jnp.where(is_g, jnp.exp(lg - mg), 0.0), axis=-1, keepdims=True)
    p_group = 1.0 / zg

    e_lane = lane - ng
    in_sel = (e_lane >= g_sel * epg) & (e_lane < (g_sel + 1) * epg)
    le = jnp.where(in_sel, logits, neg)
    m1 = jnp.max(le, axis=-1, keepdims=True)
    i1 = jnp.min(jnp.where(le == m1, lane, big), axis=-1, keepdims=True)
    ze = jnp.sum(jnp.where(in_sel, jnp.exp(le - m1), 0.0), axis=-1, keepdims=True)
    le2 = jnp.where(lane == i1, neg, le)
    m2 = jnp.max(le2, axis=-1, keepdims=True)
    i2 = jnp.min(jnp.where(le2 == m2, lane, big), axis=-1, keepdims=True)
    pe1 = 1.0 / ze
    pe2 = jnp.exp(m2 - m1) / ze
    den = pe1 + pe2
    c1 = pe1 / den * p_group
    c2 = pe2 / den * p_group
    e1 = i1 - ng
    e2 = i2 - ng
    route = jnp.where(lane == 0, e1.astype(F32),
            jnp.where(lane == 1, e2.astype(F32),
            jnp.where(lane == 2, c1, jnp.where(lane == 3, c2, 0.0))))
    return route, ((lane == e1) | (lane == e2)).astype(F32)


def _router_terms(w_rg, b_rg, w_re, b_re):
    pad = LANES - w_rg.shape[1] - w_re.shape[1]
    w = jnp.pad(jnp.concatenate([w_rg, w_re], axis=1), ((0, 0), (0, pad)))
    b = jnp.pad(jnp.concatenate([b_rg, b_re]), (0, pad))[None, :]
    w_hi = w.astype(BF16)
    w_lo = (w - w_hi.astype(F32)).astype(BF16)
    return jnp.concatenate([w_hi, w_lo], axis=1), b


def _slots_kernel(route_ref, cnt_ref, pos_ref, ends_ref, off_sc, *, tile_rows):
    i = pl.program_id(0)
    route = route_ref[...]
    tm = route.shape[0]
    lane = _iota2((tm, LANES), 1)
    oh1 = lane == route[:, 0:1].astype(I32)
    oh2 = lane == route[:, 1:2].astype(I32)
    oh = (oh1 | oh2).astype(F32)

    @pl.when(i == 0)
    def _():
        padded = jnp.ceil(cnt_ref[...] * (1.0 / tile_rows)) * tile_rows
        before = (_iota2((LANES, LANES), 0) < _iota2((LANES, LANES), 1)).astype(F32)
        starts = _mm_sel_r(padded, before)
        off_sc[...] = starts
        ends_ref[...] = (starts + padded).astype(I32)

    earlier = (_iota2((tm, tm), 1) < _iota2((tm, tm), 0)).astype(F32)
    base = off_sc[...] + _mm(earlier, oh)
    p1 = jnp.sum(jnp.where(oh1, base, 0.0), axis=-1, keepdims=True)
    p2 = jnp.sum(jnp.where(oh2, base, 0.0), axis=-1, keepdims=True)
    pos_ref[...] = jnp.where(lane == 0, p1, jnp.where(lane == 1, p2, 0.0)).astype(I32)
    off_sc[...] += jnp.sum(oh, axis=0, keepdims=True)


def _slots(route, counts, tm, tile_rows):
    t = route.shape[0]
    return pl.pallas_call(
        functools.partial(_slots_kernel, tile_rows=tile_rows),
        grid=(t // tm,),
        in_specs=[pl.BlockSpec((tm, LANES), lambda i: (i, 0)),
                  pl.BlockSpec((1, LANES), lambda i: (0, 0))],
        out_specs=[pl.BlockSpec((tm, LANES), lambda i: (i, 0)),
                   pl.BlockSpec((1, LANES), lambda i: (0, 0))],
        out_shape=[jax.ShapeDtypeStruct((t, LANES), I32),
                   jax.ShapeDtypeStruct((1, LANES), I32)],
        scratch_shapes=[pltpu.VMEM((1, LANES), F32)],
        compiler_params=_params(("arbitrary",)),
        name="moe_slots",
    )(route, counts)


WIDE_TILE_FACTOR = 2
DMA_UNROLL = 8
COMBINE_CHUNKS = 8


def _scatter_kernel(pos_ref, te_ref, nu_ref, x_ref, xs_ref, zero_sc, sem, zsem, *, tile_rows):
    i = pl.program_id(0)
    tm = x_ref.shape[0]
    n_tiles = xs_ref.shape[0] // tile_rows

    @pl.when(i == 0)
    def _():
        zero_sc[...] = jnp.zeros_like(zero_sc)

        def zcopy(j):
            return pltpu.make_async_copy(zero_sc, xs_ref.at[pl.ds(j * tile_rows, tile_rows)], zsem)

        def fill(j, n):
            partial = (j >= nu_ref[0] - 1) | (te_ref[j] != te_ref[jnp.minimum(j + 1, n_tiles - 1)])

            @pl.when(partial)
            def _():
                zcopy(j).start()
            return n + partial.astype(I32)

        n_fill = lax.fori_loop(0, n_tiles, fill, 0)

        def drain(_, carry):
            zcopy(0).wait()
            return carry
        lax.fori_loop(0, n_fill, drain, 0)

    def copy(t, c):
        p = pos_ref[2 * (i * tm + t) + c]
        return pltpu.make_async_copy(x_ref.at[pl.ds(t, 1)], xs_ref.at[pl.ds(p, 1)], sem)

    def start(t, carry):
        copy(t, 0).start(priority=0)
        copy(t, 1).start(priority=1)
        return carry

    lax.fori_loop(0, tm, start, 0, unroll=DMA_UNROLL)
    for _ in range(2 * tm):
        copy(0, 0).wait()


def _scatter(pos_flat, tile_expert, n_used, x, n_slots, tm, tile_rows):
    t, d = x.shape
    grid_spec = pltpu.PrefetchScalarGridSpec(
        num_scalar_prefetch=3,
        grid=(t // tm,),
        in_specs=[pl.BlockSpec((tm, d), lambda i, pos, te, nu: (i, 0))],
        out_specs=pl.BlockSpec(memory_space=pl.ANY),
        scratch_shapes=[pltpu.VMEM((tile_rows, d), F32), pltpu.SemaphoreType.DMA(()),
                        pltpu.SemaphoreType.DMA(())],
    )
    return pl.pallas_call(
        functools.partial(_scatter_kernel, tile_rows=tile_rows),
        grid_spec=grid_spec,
        out_shape=jax.ShapeDtypeStruct((n_slots, d), F32),
        compiler_params=_params(("arbitrary",)),
        name="moe_scatter",
    )(pos_flat, tile_expert, n_used, x)


def _experts_kernel(te_ref, nx_ref, nu_ref, xs_ref, wg_ref, wu_ref, wd_ref, out_ref, wg_sc, wu_sc, wd_sc,
                    wg_in, wu_in, wd_in, sem, slot_ref, *, expert0):
    j = pl.program_id(0)
    used = j < nu_ref[0]
    prev = te_ref[jnp.maximum(j - 1, 0)]

    def fetches(e, s):
        return [pltpu.make_async_copy(hbm.at[expert0 + e], dst.at[s], sem.at[s])
                for hbm, dst in ((wg_ref, wg_in), (wu_ref, wu_in), (wd_ref, wd_in))]

    @pl.when(used & (j == 0))
    def _():
        slot_ref[0] = 0
        for cp in fetches(te_ref[0], 0):
            cp.start()

    @pl.when(used & ((j == 0) | (te_ref[j] != prev)))
    def _():
        s = slot_ref[0]
        for cp in fetches(te_ref[j], s):
            cp.wait()
        wg_sc[...] = wg_in[s].astype(BF16)
        wu_sc[...] = wu_in[s].astype(BF16)
        wd_sc[...] = wd_in[s].astype(BF16)
        nxt = nx_ref[j]

        @pl.when(nxt >= 0)
        def _():
            for cp in fetches(nxt, 1 - s):
                cp.start()
        slot_ref[0] = 1 - s

    @pl.when(used)
    def _():
        xb = xs_ref[...].astype(BF16)
        h = _silu(jnp.dot(xb, wg_sc[...], preferred_element_type=F32))
        h = h * jnp.dot(xb, wu_sc[...], preferred_element_type=F32)
        out_ref[...] = jnp.dot(h.astype(BF16), wd_sc[...], preferred_element_type=F32)

    @pl.when(jnp.logical_not(used))
    def _():
        out_ref[...] = jnp.zeros_like(out_ref)


def _experts(tile_expert, next_expert, n_used, xs, wg, wu, wd, tile_rows, expert0):
    n_slots, d = xs.shape
    f = wg.shape[-1]

    def rows(j, te, nx, nu):
        return (jnp.maximum(jnp.minimum(j, nu[0] - 1), 0), 0)

    hbm = pl.BlockSpec(memory_space=pl.ANY)
    grid_spec = pltpu.PrefetchScalarGridSpec(
        num_scalar_prefetch=3,
        grid=(n_slots // tile_rows,),
        in_specs=[pl.BlockSpec((tile_rows, d), rows), hbm, hbm, hbm],
        out_specs=pl.BlockSpec((tile_rows, d), lambda j, te, nx, nu: (j, 0)),
        scratch_shapes=[pltpu.VMEM((d, f), BF16), pltpu.VMEM((d, f), BF16), pltpu.VMEM((f, d), BF16),
                        pltpu.VMEM((2, d, f), F32), pltpu.VMEM((2, d, f), F32), pltpu.VMEM((2, f, d), F32),
                        pltpu.SemaphoreType.DMA((2,)), pltpu.SMEM((1,), I32)],
    )
    return pl.pallas_call(
        functools.partial(_experts_kernel, expert0=expert0),
        grid_spec=grid_spec,
        out_shape=jax.ShapeDtypeStruct((n_slots, d), F32),
        compiler_params=_params(("arbitrary",)),
        name="moe_experts",
    )(tile_expert, next_expert, n_used, xs, wg, wu, wd)


def _combine_kernel(pos_ref, x_ref, route_ref, lng_ref, lnb_ref, ys_ref, *rest, alpha, split_tiles):
    if split_tiles is None:
        out_ref, o_sc, sem = rest
    else:
        out_ref, out_s_ref, o_sc, sem = rest
    i = pl.program_id(0)
    n = pl.num_programs(0)
    tm = x_ref.shape[0]
    slot = i % 2
    spare = 1 - slot
    nxt = jnp.minimum(i + 1, n - 1)
    rows_per_chunk = tm // COMBINE_CHUNKS

    def copy(tile, sl, t, c):
        p = pos_ref[2 * (tile * tm + t) + c]
        return pltpu.make_async_copy(ys_ref.at[pl.ds(p, 1)], o_sc.at[sl, c, pl.ds(t, 1)], sem.at[sl])

    def wait_tile(tile, sl):
        for _ in range(2 * tm):
            copy(tile, sl, 0, 0).wait()

    @pl.when(i == 0)
    def _():
        def body(t, carry):
            copy(0, 0, t, 0).start(priority=0)
            copy(0, 0, t, 1).start(priority=1)
            return carry
        lax.fori_loop(0, tm, body, 0, unroll=DMA_UNROLL)

    wait_tile(i, slot)

    def mix_into(out):
        for k in range(COMBINE_CHUNKS):
            rows = slice(k * rows_per_chunk, (k + 1) * rows_per_chunk)
            for t in range(rows.start, rows.stop):
                copy(nxt, spare, t, 0).start(priority=0)
                copy(nxt, spare, t, 1).start(priority=1)
            y = route_ref[rows, 2:3] * o_sc[slot, 0, rows] + route_ref[rows, 3:4] * o_sc[slot, 1, rows]
            out[rows, :] = _layer_norm(alpha * x_ref[rows, :] + y, lng_ref[...], lnb_ref[...])

    if split_tiles is None:
        mix_into(out_ref)
    else:
        @pl.when(i < split_tiles)
        def _():
            mix_into(out_ref)

        @pl.when(i >= split_tiles)
        def _():
            mix_into(out_s_ref)

    @pl.when(i == n - 1)
    def _():
        wait_tile(nxt, spare)


def _combine(pos_flat, x, route, lng, lnb, ys, *, alpha, tm, split_rows=None):
    t, d = x.shape
    out_specs = pl.BlockSpec((tm, d), lambda i, pos: (i, 0))
    out_shape = jax.ShapeDtypeStruct((t, d), F32)
    npt = None
    if split_rows is not None:
        npt = split_rows // tm
        out_specs = [pl.BlockSpec((tm, d), lambda i, pos: (jnp.minimum(i, npt - 1), 0)),
                     pl.BlockSpec((tm, d), lambda i, pos: (jnp.maximum(i - npt, 0), 0))]
        out_shape = [jax.ShapeDtypeStruct((split_rows, d), F32),
                     jax.ShapeDtypeStruct((t - split_rows, d), F32)]
    grid_spec = pltpu.PrefetchScalarGridSpec(
        num_scalar_prefetch=1,
        grid=(t // tm,),
        in_specs=[pl.BlockSpec((tm, d), lambda i, pos: (i, 0)),
                  pl.BlockSpec((tm, LANES), lambda i, pos: (i, 0)),
                  pl.BlockSpec((1, d), lambda i, pos: (0, 0)),
                  pl.BlockSpec((1, d), lambda i, pos: (0, 0)),
                  pl.BlockSpec(memory_space=pl.ANY)],
        out_specs=out_specs,
        scratch_shapes=[pltpu.VMEM((2, 2, tm, d), F32), pltpu.SemaphoreType.DMA((2,))],
    )
    return pl.pallas_call(
        functools.partial(_combine_kernel, alpha=alpha, split_tiles=npt),
        grid_spec=grid_spec,
        out_shape=out_shape,
        compiler_params=_params(("arbitrary",)),
        name="moe_combine",
    )(pos_flat, x, route, lng, lnb, ys)


def _hier_moe_ln(x, route, counts, w_gate, w_up, w_down, lng, lnb, *, alpha, tm, tile_rows,
                 layer=0, split_rows=None):
    t, d = x.shape
    n_exp = EXPERT_GROUPS * EXPERTS_PER_GROUP
    pos, ends = _slots(route, counts, WIDE_TILE_FACTOR * tm, tile_rows)
    pos_flat = pos[:, :2].reshape(-1)
    n_slots = 2 * t + n_exp * tile_rows
    n_tiles = n_slots // tile_rows
    seg_end = ends[0, :n_exp]
    tile_start = jnp.arange(n_tiles, dtype=I32) * tile_rows
    tile_expert = jnp.minimum(jnp.sum(tile_start[:, None] >= seg_end[None, :], axis=1), n_exp - 1).astype(I32)
    n_used = (seg_end[n_exp - 1:] // tile_rows).astype(I32)
    ids = jnp.arange(n_exp, dtype=I32)
    nonempty = seg_end > jnp.concatenate([jnp.zeros((1,), I32), seg_end[:-1]])
    later = jnp.where((ids[None, :] > tile_expert[:, None]) & nonempty[None, :], ids[None, :], n_exp)
    next_expert = jnp.min(later, axis=1)
    next_expert = jnp.where(next_expert < n_exp, next_expert, -1).astype(I32)
    xs = _scatter(pos_flat, tile_expert, n_used, x, n_slots, tm, tile_rows)
    f = w_gate.shape[-1]
    ys = _experts(tile_expert, next_expert, n_used, xs, w_gate.reshape(-1, d, f), w_up.reshape(-1, d, f),
                  w_down.reshape(-1, f, d), tile_rows, layer * n_exp)
    return _combine(pos_flat, x, route, lng, lnb, ys, alpha=alpha, tm=tm, split_rows=split_rows)


def _ssd_in_kernel(x_ref, wz_ref, wx_ref, wdt_ref, z_ref, xbc_ref, dt_ref):
    xb = x_ref[...].astype(BF16)
    z_ref[...] = jnp.dot(xb, wz_ref[...], preferred_element_type=F32)
    xbc_ref[...] = jnp.dot(xb, wx_ref[...], preferred_element_type=F32)
    dt_ref[...] = jnp.dot(xb, wdt_ref[...], preferred_element_type=F32)


def _ssd_in(x, wz, wx, wdt, tm):
    t, d = x.shape
    outs = [wz.shape[1], wx.shape[1], wdt.shape[1]]
    return pl.pallas_call(
        _ssd_in_kernel,
        grid=(t // tm,),
        in_specs=[pl.BlockSpec((tm, d), lambda i: (i, 0))]
                 + [pl.BlockSpec(w.shape, lambda i: (0, 0)) for w in (wz, wx, wdt)],
        out_specs=[pl.BlockSpec((tm, n), lambda i: (i, 0)) for n in outs],
        out_shape=[jax.ShapeDtypeStruct((t, n), F32) for n in outs],
        compiler_params=_params(("parallel",)),
        name="ssd_in",
    )(x, wz, wx, wdt)


def _ssd_kernel(xbc_ref, dt_ref, z_ref, conv0_ref, h0_ref, cw_ref, cb_ref, dtb_ref, alog_ref,
                dexp_ref, nw_ref, expand_ref, y_ref, convo_ref, hout_ref, pad_sc, ht_sc, *, chunk):
    c = chunk
    ci = pl.program_id(1)
    d_inner = z_ref.shape[1]
    gn = SSD_GROUPS * D_STATE
    per_group = d_inner // SSD_GROUPS
    tail = 8

    @pl.when(ci == 0)
    def _():
        pad_sc[0:tail, :] = conv0_ref[...]
        ht_sc[...] = h0_ref[...].T

    pad_sc[tail:tail + c, :] = xbc_ref[...]
    acc = cb_ref[...]
    for i in range(CONV_WIDTH):
        off = tail - (CONV_WIDTH - 1) + i
        acc = acc + pad_sc[off:off + c, :] * cw_ref[i:i + 1, :]
    xbc = _silu(acc)
    last_rows = pad_sc[c:c + tail, :]
    pad_sc[0:tail, :] = last_rows
    xs = xbc[:, :d_inner]

    dt = _softplus(dt_ref[...] + dtb_ref[...])
    da = dt * (-jnp.exp(alog_ref[...]))
    tril_incl = (_iota2((c, c), 1) <= _iota2((c, c), 0)).astype(F32)
    acum = _mm_sel_l(tril_incl, da)
    a_last = acum[c - 1:c, :]

    def to_cols(t):
        if c < LANES:
            t = jnp.concatenate([t, jnp.zeros((LANES - c, LANES), F32)], axis=0)
        return t.T[:, :c]

    acum_t = to_cols(acum)
    expand = expand_ref[...]
    per_head = jnp.concatenate([jnp.exp(acum), jnp.exp(a_last - acum) * dt, dt], axis=0)
    per_chan = _mm_sel_r(per_head, expand)
    ea, dd = per_chan[:c], per_chan[c:2 * c]
    x_dt = xs * per_chan[2 * c:]
    ea_last = _mm_sel_r(jnp.exp(a_last), expand)
    causal = _iota2((c, c), 1) <= _iota2((c, c), 0)
    lane = _iota2((1, LANES), 1)
    m0 = (lane < HEAD).astype(F32)
    m1 = 1.0 - m0

    ys = []
    for g in range(SSD_GROUPS):
        b_g = xbc[:, d_inner + g * D_STATE:d_inner + (g + 1) * D_STATE]
        c_g = xbc[:, d_inner + gn + g * D_STATE:d_inner + gn + (g + 1) * D_STATE]
        cb = _mm_nt(c_g, b_g)
        gsl = slice(g * per_group, (g + 1) * per_group)
        ht_g = ht_sc[:, gsl]
        y_off = _mm(c_g, ht_g) * ea[:, gsl]
        ht_sc[:, gsl] = ht_g * ea_last[:, gsl] + _mm_tn(b_g, xs[:, gsl] * dd[:, gsl])
        pieces = []
        for q in range(per_group // LANES):
            j0 = (g * per_group) // HEAD + 2 * q
            ms = []
            for j in (j0, j0 + 1):
                seg = acum[:, j:j + 1] - acum_t[j:j + 1, :]
                ms.append(jnp.where(causal, cb * jnp.exp(seg), 0.0))
            lo = g * per_group + q * LANES
            x_pair = x_dt[:, lo:lo + LANES]
            x_stack = jnp.concatenate([x_pair * m0, x_pair * m1], axis=0)
            pieces.append(_mm(jnp.concatenate(ms, axis=1), x_stack))
        y_g = jnp.concatenate(pieces, axis=1) + y_off + xs[:, gsl] * dexp_ref[:, gsl]
        y_g = y_g * _silu(z_ref[:, gsl])
        y_g = y_g * lax.rsqrt(jnp.mean(y_g * y_g, axis=-1, keepdims=True) + RMS_EPS)
        ys.append(y_g * nw_ref[:, gsl])
    y_ref[...] = jnp.concatenate(ys, axis=1)

    @pl.when(ci == pl.num_programs(1) - 1)
    def _():
        convo_ref[...] = last_rows
        hout_ref[...] = ht_sc[...].T


def _ssd(xbc, dt, z, conv0, h0, cw, cb, dtb, alog, dexp, nw, expand, *, batch, seq, row0, chunk):
    d_inner = z.shape[1]
    conv_dim = xbc.shape[1]
    nc = seq // chunk
    blk0 = row0 // chunk

    def tok(n):
        return pl.BlockSpec((chunk, n), lambda b, c: (blk0 + b * nc + c, 0))

    def par(arr):
        return pl.BlockSpec(arr.shape, lambda b, c: (0, 0))

    return pl.pallas_call(
        functools.partial(_ssd_kernel, chunk=chunk),
        grid=(batch, nc),
        in_specs=[tok(conv_dim), tok(LANES), tok(d_inner),
                  pl.BlockSpec((None, 8, conv_dim), lambda b, c: (b, 0, 0)),
                  pl.BlockSpec((None, d_inner, D_STATE), lambda b, c: (b, 0, 0)),
                  par(cw), par(cb), par(dtb), par(alog), par(dexp), par(nw), par(expand)],
        out_specs=[pl.BlockSpec((chunk, d_inner), lambda b, c: (b * nc + c, 0)),
                   pl.BlockSpec((None, 8, conv_dim), lambda b, c: (b, 0, 0)),
                   pl.BlockSpec((None, d_inner, D_STATE), lambda b, c: (b, 0, 0))],
        out_shape=[jax.ShapeDtypeStruct((batch * seq, d_inner), F32),
                   jax.ShapeDtypeStruct((batch, 8, conv_dim), F32),
                   jax.ShapeDtypeStruct((batch, d_inner, D_STATE), F32)],
        scratch_shapes=[pltpu.VMEM((chunk + 8, conv_dim), F32), pltpu.VMEM((D_STATE, d_inner), F32)],
        compiler_params=_params(("parallel", "arbitrary")),
        name="ssd_l%d" % chunk,
    )(xbc, dt, z, conv0, h0, cw, cb, dtb, alog, dexp, nw, expand)


def _rwkv_layer(xp3, xs3, shift_s, wkv_s, prm, lng, lnb, router, *, alpha, tm):
    (mu, w_rkv, w0, w1, w2, a0, a1, a2, g1, g2, k_k, k_a, r_k, gn_w, gn_b, w_o) = prm
    bp, sp, d = xp3.shape
    bs, ss, _ = xs3.shape
    tp = bp * sp
    heads = d // HEAD
    x_p = xp3.reshape(tp, d)
    x_s = xs3.reshape(bs * ss, d)
    prev_s = jnp.concatenate([shift_s[:, None, :], xs3[:, :-1]], axis=1).reshape(bs * ss, d)
    row = lambda v: v.reshape(1, d)
    r, k, v, lw, a, g = _rwkv_pre(x_p, x_s, prev_s, sp, mu, w_rkv.astype(BF16), row(w0), w1.astype(BF16),
                                  w2.astype(BF16), row(a0), a1.astype(BF16), a2.astype(BF16),
                                  g1.astype(BF16), g2.astype(BF16), tm)
    pv = (row(k_k), row(k_a), row(r_k), row(gn_w), row(gn_b))
    o_p, sp_out = _wkv(r, k, v, lw, a, *pv, None, batch=bp, seq=sp, row0=0,
                       chunk=min(WKV_CHUNK, sp), pairs_per_step=min(WKV_PAIRS_PER_STEP, heads // 2))
    chunk_s = min(WKV_CHUNK, ss)
    sps = WKV_SHORT_SEQS_PER_STEP
    if ss != chunk_s or bs % sps or tp % (sps * chunk_s):
        sps = 1
    o_s, ss_out = _wkv(r, k, v, lw, a, *pv, wkv_s, batch=bs, seq=ss, row0=tp, chunk=chunk_s,
                       pairs_per_step=heads // 2, seqs_per_step=sps)
    routed = _proj_ln(o_p, o_s, g, (x_p, x_s, 0), w_o.astype(BF16), lng, lnb, *router, alpha=alpha,
                      tm=WIDE_TILE_FACTOR * tm)
    return routed, sp_out, ss_out


def _ssd_layer(x, bp, sp, bs, ss, conv_s, ssm_s, prm, lng, lnb, router, *, alpha, tm):
    (w_in, conv_w, conv_b, dt_bias, a_log, d_skip, norm_w, w_out) = prm
    t, d = x.shape
    tp = bp * sp
    heads = a_log.shape[0]
    d_inner = heads * HEAD
    conv_dim = conv_w.shape[1]
    wz = w_in[:, :d_inner].astype(BF16)
    wx = w_in[:, d_inner:d_inner + conv_dim].astype(BF16)
    wdt = jnp.pad(w_in[:, d_inner + conv_dim:], ((0, 0), (0, LANES - heads))).astype(BF16)
    wide = WIDE_TILE_FACTOR * tm
    z, xbc, dt = _ssd_in(x, wz, wx, wdt, wide)
    lane_pad = lambda v: jnp.pad(v, (0, LANES - heads))[None, :]
    expand = (jnp.arange(LANES)[:, None] == (jnp.arange(d_inner) // HEAD)[None, :]).astype(F32)
    common = (conv_w, conv_b[None, :], lane_pad(dt_bias), lane_pad(a_log),
              jnp.repeat(d_skip, HEAD)[None, :], norm_w[None, :], expand)
    conv_tail = CONV_WIDTH - 1
    pad_conv = lambda cs: jnp.pad(cs, ((0, 0), (8 - conv_tail, 0), (0, 0)))
    y_p, conv_p, h_p = _ssd(xbc, dt, z, jnp.zeros((bp, 8, conv_dim), F32),
                            jnp.zeros((bp, d_inner, D_STATE), F32), *common,
                            batch=bp, seq=sp, row0=0, chunk=SSD_CHUNK if sp % SSD_CHUNK == 0 else sp)
    y_s, conv_s_out, h_s = _ssd(xbc, dt, z, pad_conv(conv_s), ssm_s.reshape(bs, d_inner, D_STATE), *common,
                                batch=bs, seq=ss, row0=tp, chunk=SSD_CHUNK if ss % SSD_CHUNK == 0 else ss)
    routed = _proj_ln(y_p, y_s, None, (x, x, tp // wide), w_out.astype(BF16), lng, lnb, *router, alpha=alpha,
                      tm=wide)
    shape_h = lambda h, b: h.reshape(b, heads, HEAD, D_STATE)
    return (routed, conv_p[:, 8 - conv_tail:], conv_s_out[:, 8 - conv_tail:], shape_h(h_p, bp), shape_h(h_s, bs))


def kernel(x_prompt, x_sample, state_rwkv_wkv, state_rwkv_shift, state_ssd_ssm, state_ssd_conv,
           rwkv_mu, rwkv_w_rkv, rwkv_w0, rwkv_w1, rwkv_w2, rwkv_a0, rwkv_a1, rwkv_a2,
           rwkv_g1, rwkv_g2, rwkv_k_k, rwkv_k_a, rwkv_r_k, rwkv_gn_w, rwkv_gn_b, rwkv_w_o,
           ssd_w_in, ssd_conv_w, ssd_conv_b, ssd_dt_bias, ssd_a_log, ssd_d, ssd_norm_w, ssd_w_out,
           ln_gain, ln_bias, moe_w_rg, moe_b_rg, moe_w_re, moe_b_re, moe_w_gate, moe_w_up, moe_w_down):
    bp, sp, d = x_prompt.shape
    bs, ss, _ = x_sample.shape
    depth = ln_gain.shape[0]
    alpha = (2 * depth) ** 0.25
    tp, ts = bp * sp, bs * ss
    tm = 256
    tile_rows = 256
    x = None
    xp3, xs3 = x_prompt, x_sample
    wkv_p, wkv_s, sh_p, sh_s, ssm_p, ssm_s, cv_p, cv_s = [], [], [], [], [], [], [], []
    for i in range(depth):
        j = i // 2
        lng = lambda n: ln_gain[i, n][None, :]
        lnb = lambda n: ln_bias[i, n][None, :]
        router = _router_terms(moe_w_rg[i], moe_b_rg[i], moe_w_re[i], moe_b_re[i])
        if i % 2 == 0:
            if x is not None:
                xp3, xs3 = x[:tp].reshape(bp, sp, d), x[tp:].reshape(bs, ss, d)
            prm = (rwkv_mu[j], rwkv_w_rkv[j], rwkv_w0[j], rwkv_w1[j], rwkv_w2[j], rwkv_a0[j],
                   rwkv_a1[j], rwkv_a2[j], rwkv_g1[j], rwkv_g2[j], rwkv_k_k[j], rwkv_k_a[j],
                   rwkv_r_k[j].reshape(-1), rwkv_gn_w[j], rwkv_gn_b[j], rwkv_w_o[j])
            sh_p.append(xp3[:, -1])
            sh_s.append(xs3[:, -1])
            routed, w_p, w_s = _rwkv_layer(xp3, xs3, state_rwkv_shift[j], state_rwkv_wkv[j], prm,
                                           lng(0), lnb(0), router, alpha=alpha, tm=tm)
            wkv_p.append(w_p)
            wkv_s.append(w_s)
        else:
            prm = (ssd_w_in[j], ssd_conv_w[j], ssd_conv_b[j], ssd_dt_bias[j], ssd_a_log[j],
                   ssd_d[j], ssd_norm_w[j], ssd_w_out[j])
            routed, c_p, c_s, h_p, h_s = _ssd_layer(x, bp, sp, bs, ss, state_ssd_conv[j], state_ssd_ssm[j],
                                                    prm, lng(0), lnb(0), router, alpha=alpha, tm=tm)
            cv_p.append(c_p)
            cv_s.append(c_s)
            ssm_p.append(h_p)
            ssm_s.append(h_s)
        x = _hier_moe_ln(*routed, moe_w_gate, moe_w_up, moe_w_down, lng(1), lnb(1), alpha=alpha, tm=tm,
                         tile_rows=tile_rows, layer=i, split_rows=tp if i == depth - 1 else None)
    yp = x[0].reshape(bp, sp, d)
    ys = x[1].reshape(bs, ss, d)
    return (yp, ys, jnp.stack(wkv_p), jnp.stack(wkv_s), jnp.stack(sh_p), jnp.stack(sh_s),
            jnp.stack(ssm_p), jnp.stack(ssm_s), jnp.stack(cv_p), jnp.stack(cv_s))
```

```python
import functools
import math

import jax
import jax.numpy as jnp
from jax import lax
from jax.experimental import pallas as pl
from jax.experimental.pallas import tpu as pltpu

F32 = jnp.float32
BF16 = jnp.bfloat16
I32 = jnp.int32

LANES = 128
SUBLANES = 8
HEAD = 64
GN_EPS = HEAD * 1e-5
RMS_EPS = 1e-5
LN_EPS = 1e-5
WKV_CHUNK = 64
WKV_PAIRS_PER_STEP = 8
WKV_SHORT_SEQS_PER_STEP = 8
SSD_CHUNK = 128
SSD_GROUPS = 4
D_STATE = 128
CONV_WIDTH = 4
EXPERT_GROUPS = 4
EXPERTS_PER_GROUP = 8
VMEM_LIMIT = 56 * 1024 * 1024


def _params(sem):
    return pltpu.CompilerParams(dimension_semantics=sem, vmem_limit_bytes=VMEM_LIMIT)


def _mm(a, b):
    return jnp.dot(a.astype(BF16), b.astype(BF16), preferred_element_type=F32)


def _mm_nt(a, b):
    return lax.dot_general(a.astype(BF16), b.astype(BF16), (((1,), (1,)), ((), ())),
                           preferred_element_type=F32)


def _mm_tn(a, b):
    return lax.dot_general(a.astype(BF16), b.astype(BF16), (((0,), (0,)), ((), ())),
                           preferred_element_type=F32)


def _split3(x):
    hi = x.astype(BF16)
    r1 = x - hi.astype(F32)
    mid = r1.astype(BF16)
    lo = (r1 - mid.astype(F32)).astype(BF16)
    return hi, mid, lo


def _mm_sel_l(sel, x):
    k = x.shape[1]
    parts = jnp.concatenate([p.astype(F32) for p in _split3(x)], axis=1).astype(BF16)
    y = jnp.dot(sel.astype(BF16), parts, preferred_element_type=F32)
    return y[:, :k] + y[:, k:2 * k] + y[:, 2 * k:]


def _mm_sel_r(x, sel):
    n = x.shape[0]
    if n % SUBLANES:
        return sum(jnp.dot(p, sel.astype(BF16), preferred_element_type=F32) for p in _split3(x))
    parts = jnp.concatenate([p.astype(F32) for p in _split3(x)], axis=0).astype(BF16)
    y = jnp.dot(parts, sel.astype(BF16), preferred_element_type=F32)
    return y[:n] + y[n:2 * n] + y[2 * n:]


def _sigmoid(x):
    return 1.0 / (1.0 + jnp.exp(-x))


def _silu(x):
    return x * _sigmoid(x)


def _softplus(x):
    return jnp.maximum(x, 0.0) + jnp.log(1.0 + jnp.exp(-jnp.abs(x)))


def _layer_norm(x, g, b):
    mu = jnp.mean(x, axis=-1, keepdims=True)
    xc = x - mu
    var = jnp.mean(xc * xc, axis=-1, keepdims=True)
    return xc * lax.rsqrt(var + LN_EPS) * g + b


def _iota2(shape, dim):
    return lax.broadcasted_iota(I32, shape, dim)


def _part_specs(tm, width, prompt_tiles, sample_block0=0):
    return [pl.BlockSpec((tm, width), lambda i: (jnp.minimum(i, prompt_tiles - 1), 0)),
            pl.BlockSpec((tm, width), lambda i: (sample_block0 + jnp.maximum(i - prompt_tiles, 0), 0))]


def _rwkv_pre_kernel(xa_ref, xb_ref, tail_ref, xprev_s_ref, mu_ref, wrkv_ref, w0_ref, w1_ref, w2_ref,
                     a0_ref, a1_ref, a2_ref, g1_ref, g2_ref, r_ref, k_ref, v_ref, lw_ref, a_ref, g_ref,
                     *, prompt_tiles, seq):
    i = pl.program_id(0)
    tm = xa_ref.shape[0]
    is_prompt = i < prompt_tiles
    xa = xa_ref[...]
    x = jnp.where(is_prompt, xa, xb_ref[...])
    first = jnp.where((i * tm) % seq == 0, 0.0, tail_ref[SUBLANES - 1:SUBLANES, :])
    shifted = jnp.where(_iota2(xa.shape, 0) == 0, first, pltpu.roll(xa, 1, axis=0))
    xx = jnp.where(is_prompt, shifted, xprev_s_ref[...]) - x

    def mix(j):
        return x + xx * mu_ref[j:j + 1, :]

    r_ref[...] = _mm(mix(0), wrkv_ref[0])
    k_ref[...] = _mm(mix(1), wrkv_ref[1])
    v_ref[...] = _mm(mix(2), wrkv_ref[2])
    ww = w0_ref[...] + _mm(jnp.tanh(_mm(mix(3), w1_ref[...])), w2_ref[...])
    w_raw = -_softplus(-ww) - 0.5
    lw_ref[...] = -jnp.exp(w_raw)
    a_ref[...] = _sigmoid(a0_ref[...] + _mm(_mm(mix(4), a1_ref[...]), a2_ref[...]))
    g_ref[...] = _mm(_sigmoid(_mm(mix(5), g1_ref[...])), g2_ref[...])


def _rwkv_pre(x_p, x_s, xprev_s, seq, mu, wrkv, w0, w1, w2, a0, a1, a2, g1, g2, tm):
    (tp, d), ts = x_p.shape, x_s.shape[0]
    assert seq % tm == 0 and ts % tm == 0 and tm % SUBLANES == 0
    npt = tp // tm
    row = pl.BlockSpec((tm, d), lambda i: (i, 0))
    tail = pl.BlockSpec((SUBLANES, d),
                        lambda i: (jnp.maximum(jnp.minimum(i, npt - 1) * (tm // SUBLANES) - 1, 0), 0))

    def full(arr):
        nd = arr.ndim
        return pl.BlockSpec(arr.shape, lambda i, _n=nd: (0,) * _n)

    weights = (mu, wrkv, w0, w1, w2, a0, a1, a2, g1, g2)
    return pl.pallas_call(
        functools.partial(_rwkv_pre_kernel, prompt_tiles=npt, seq=seq),
        grid=((tp + ts) // tm,),
        in_specs=_part_specs(tm, d, npt) + [tail, _part_specs(tm, d, npt)[1]] + [full(w) for w in weights],
        out_specs=[row] * 6,
        out_shape=[jax.ShapeDtypeStruct((tp + ts, d), F32)] * 6,
        compiler_params=_params(("parallel",)),
        name="rwkv_pre",
    )(x_p, x_s, x_p, xprev_s, *weights)


def _wkv_kernel(r_ref, k_ref, v_ref, lw_ref, a_ref, kkw_ref, kaw_ref, rk_ref, gnw_ref, gnb_ref,
                *rest, chunk, pairs_per_step, seqs_per_step, has_init):
    if has_init:
        s0_ref, o_ref, sout_ref, s_sc = rest
    else:
        o_ref, sout_ref, s_sc = rest
    L = chunk
    c = pl.program_id(2)
    units = [(s, q) for s in range(seqs_per_step) for q in range(pairs_per_step)]

    @pl.when(c == 0)
    def _():
        if has_init:
            zero = jnp.zeros((HEAD, HEAD), F32)
            for u, (s, q) in enumerate(units):
                top = jnp.concatenate([s0_ref[s, 2 * q], zero], axis=1)
                bot = jnp.concatenate([zero, s0_ref[s, 2 * q + 1]], axis=1)
                s_sc[u] = jnp.concatenate([top, bot], axis=0)
        else:
            s_sc[...] = jnp.zeros_like(s_sc)

    lane = _iota2((1, LANES), 1)
    m0 = (lane < HEAD).astype(F32)
    m1 = 1.0 - m0
    same_head = (_iota2((LANES, LANES), 0) // HEAD == _iota2((LANES, LANES), 1) // HEAD).astype(F32)
    tril_incl = (_iota2((L, L), 1) <= _iota2((L, L), 0)).astype(F32)
    n2 = 2 * L
    row = _iota2((n2, n2), 0)
    col = _iota2((n2, n2), 1)
    strict = col < row
    incl = col <= row
    steps = int(math.log2(L))

    def headsum(t):
        return _mm_sel_r(t, same_head)

    def stack(t):
        return jnp.concatenate([t * m0, t * m1], axis=0)

    pairs = range(len(units))
    rows = [slice(s * L, (s + 1) * L) for s, _ in units]
    lanes = [slice(q * LANES, (q + 1) * LANES) for _, q in units]

    def prepare(rs, ls):
        r = r_ref[rs, ls]
        k = k_ref[rs, ls]
        v = v_ref[rs, ls]
        lw = lw_ref[rs, ls]
        a = a_ref[rs, ls]
        kk = k * kkw_ref[:, ls]
        k2 = k * (1.0 + (a - 1.0) * kaw_ref[:, ls])
        sums = headsum(jnp.concatenate([kk * kk, r * k2 * rk_ref[:, ls]], axis=0))
        kk = kk / jnp.maximum(jnp.sqrt(sums[:L]), 1e-12)
        bonus = sums[L:] * v
        bvec = kk * a
        cl = _mm_sel_l(tril_incl, lw)
        total = cl[L - 1:L, :]
        w_in = jnp.exp(cl)
        w_inv = jnp.exp(-cl)
        w_end = jnp.exp(total - cl)
        return dict(
            ar=jnp.concatenate([stack(-kk * jnp.exp(cl - lw)), stack(r * w_in)], axis=0).astype(BF16),
            bk=jnp.concatenate([stack(bvec * w_inv), stack(k2 * w_inv)], axis=0).astype(BF16),
            bk_end=jnp.concatenate([stack(bvec * w_end), stack(k2 * w_end)], axis=0),
            v_s=stack(v), total=total, bonus=bonus)

    ops = [prepare(rs, ls) for rs, ls in zip(rows, lanes)]
    states = [s_sc[q] for q in pairs]
    cross = [_mm_nt(o["ar"], o["bk"]) for o in ops]
    from_s = [_mm_nt(o["ar"], s) for o, s in zip(ops, states)]
    n_ab = [jnp.where(strict, x[:n2, :n2], 0.0) for x in cross]
    a_rb = [jnp.where(incl, x[n2:, :n2], 0.0) for x in cross]
    to_v = [jnp.concatenate([jnp.where(strict, x[:n2, n2:], 0.0), jnp.where(incl, x[n2:, n2:], 0.0)], axis=0)
            for x in cross]
    from_v = [_mm(m, o["v_s"]) for m, o in zip(to_v, ops)]

    u = [fs[:n2] + fv[:n2] for fs, fv in zip(from_s, from_v)]
    p = n_ab
    for j in range(steps):
        if j + 1 < steps:
            both = [_mm(pi, jnp.concatenate([ui, pi], axis=1)) for ui, pi in zip(u, p)]
            u = [ui + b[:, :LANES] for ui, b in zip(u, both)]
            p = [b[:, LANES:] for b in both]
        else:
            u = [ui + _mm(pi, ui) for ui, pi in zip(u, p)]

    y_s = [fs[n2:] + fv[n2:] + _mm(mb, ui) for fs, fv, mb, ui in zip(from_s, from_v, a_rb, u)]
    s_new = [s * jnp.exp(o["total"]) + _mm_tn(jnp.concatenate([ui, o["v_s"]], axis=0), o["bk_end"])
             for o, s, ui in zip(ops, states, u)]
    for q in pairs:
        s_sc[q] = s_new[q]
        y = y_s[q][:L] + y_s[q][L:]
        ym = headsum(y) * (1.0 / HEAD)
        yc = y - ym
        yv = headsum(yc * yc) * (1.0 / HEAD)
        rs, ls = rows[q], lanes[q]
        o_ref[rs, ls] = yc * lax.rsqrt(yv + GN_EPS) * gnw_ref[:, ls] + gnb_ref[:, ls] + ops[q]["bonus"]

    @pl.when(c == pl.num_programs(2) - 1)
    def _():
        for u, (s, q) in enumerate(units):
            sout_ref[s, 2 * q] = s_new[u][:HEAD, :HEAD]
            sout_ref[s, 2 * q + 1] = s_new[u][HEAD:, HEAD:]


def _wkv(r, k, v, lw, a, kkw, kaw, rk, gnw, gnb, s0, *, batch, seq, row0, chunk, pairs_per_step,
         seqs_per_step=1):
    d = r.shape[1]
    pps, sps = pairs_per_step, seqs_per_step
    width = pps * LANES
    nc = seq // chunk
    assert batch % sps == 0 and (sps == 1 or nc == 1) and row0 % (sps * chunk) == 0
    blk0 = row0 // (sps * chunk)
    tok = pl.BlockSpec((sps * chunk, width), lambda b, p, c: (blk0 + b * nc + c, p))
    out = pl.BlockSpec((sps * chunk, width), lambda b, p, c: (b * nc + c, p))
    par = pl.BlockSpec((1, width), lambda b, p, c: (0, p))
    st = pl.BlockSpec((sps, 2 * pps, HEAD, HEAD), lambda b, p, c: (b, p, 0, 0))
    has_init = s0 is not None
    return pl.pallas_call(
        functools.partial(_wkv_kernel, chunk=chunk, pairs_per_step=pps, seqs_per_step=sps,
                          has_init=has_init),
        grid=(batch // sps, d // width, nc),
        in_specs=[tok] * 5 + [par] * 5 + ([st] if has_init else []),
        out_specs=[out, st],
        out_shape=[jax.ShapeDtypeStruct((batch * seq, d), F32),
                   jax.ShapeDtypeStruct((batch, d // HEAD, HEAD, HEAD), F32)],
        scratch_shapes=[pltpu.VMEM((sps * pps, LANES, LANES), F32)],
        compiler_params=_params(("parallel", "parallel", "arbitrary")),
        name="wkv_l%d" % chunk,
    )(r, k, v, lw, a, kkw, kaw, rk, gnw, gnb, *([s0] if has_init else []))


def _proj_ln_kernel(op_ref, os_ref, g_ref, xa_ref, xb_ref, w_ref, lng_ref, lnb_ref, rw_ref, rb_ref, out_ref,
                    route_ref, cnt_ref, *, alpha, gated, prompt_tiles):
    is_prompt = pl.program_id(0) < prompt_tiles
    o = jnp.where(is_prompt, op_ref[...], os_ref[...])
    if gated:
        o = o * g_ref[...]
    y = _mm(o, w_ref[...])
    x = jnp.where(is_prompt, xa_ref[...], xb_ref[...])
    x1 = _layer_norm(alpha * x + y, lng_ref[...], lnb_ref[...])
    out_ref[...] = x1
    route, chosen = _route(x1, rw_ref, rb_ref)
    route_ref[...] = route

    @pl.when(pl.program_id(0) == 0)
    def _():
        cnt_ref[...] = jnp.zeros_like(cnt_ref)

    cnt_ref[...] += jnp.sum(chosen, axis=0, keepdims=True)


def _proj_ln(o_p, o_s, g, x_parts, w, lng, lnb, router_w, router_b, *, alpha, tm):
    x_a, x_b, sample_block0 = x_parts
    d = x_a.shape[1]
    kin = o_p.shape[1]
    npt = o_p.shape[0] // tm
    t = o_p.shape[0] + o_s.shape[0]
    gated = g is not None
    g_spec = pl.BlockSpec((tm, kin), lambda i: (i, 0))
    if not gated:
        g, g_spec = lng, pl.BlockSpec((1, d), lambda i: (0, 0))
    const = lambda arr: pl.BlockSpec(arr.shape, lambda i: (0, 0))
    return pl.pallas_call(
        functools.partial(_proj_ln_kernel, alpha=alpha, gated=gated, prompt_tiles=npt),
        grid=(t // tm,),
        in_specs=_part_specs(tm, kin, npt) + [g_spec] + _part_specs(tm, d, npt, sample_block0)
                 + [const(w), const(lng), const(lnb), const(router_w), const(router_b)],
        out_specs=[pl.BlockSpec((tm, d), lambda i: (i, 0)),
                   pl.BlockSpec((tm, LANES), lambda i: (i, 0)),
                   pl.BlockSpec((1, LANES), lambda i: (0, 0))],
        out_shape=[jax.ShapeDtypeStruct((t, d), F32),
                   jax.ShapeDtypeStruct((t, LANES), F32),
                   jax.ShapeDtypeStruct((1, LANES), F32)],
        compiler_params=_params(("arbitrary",)),
        name="proj_ln",
    )(o_p, o_s, g, x_a, x_b, w, lng, lnb, router_w, router_b)


def _route(x, w_ref, b_ref):
    ng, epg = EXPERT_GROUPS, EXPERTS_PER_GROUP
    x_hi = x.astype(BF16)
    x_lo = (x - x_hi.astype(F32)).astype(BF16)
    by_hi = jnp.dot(x_hi, w_ref[...], preferred_element_type=F32)
    logits = (by_hi[:, :LANES] + by_hi[:, LANES:]
              + jnp.dot(x_lo, w_ref[:, :LANES], preferred_element_type=F32)
              + b_ref[...])
    lane = _iota2(logits.shape, 1)
    neg = jnp.float32(-jnp.inf)
    big = jnp.int32(1 << 20)

    is_g = lane < ng
    lg = jnp.where(is_g, logits, neg)
    mg = jnp.max(lg, axis=-1, keepdims=True)
    g_sel = jnp.min(jnp.where(lg == mg, lane, big), axis=-1, keepdims=True)
    zg = jnp.sum(jnp.where(is_g, jnp.exp(lg - mg), 0.0), axis=-1, keepdims=True)
    p_group = 1.0 / zg

    e_lane = lane - ng
    in_sel = (e_lane >= g_sel * epg) & (e_lane < (g_sel + 1) * epg)
    le = jnp.where(in_sel, logits, neg)
    m1 = jnp.max(le, axis=-1, keepdims=True)
    i1 = jnp.min(jnp.where(le == m1, lane, big), axis=-1, keepdims=True)
    ze = jnp.sum(jnp.where(in_sel, jnp.exp(le - m1), 0.0), axis=-1, keepdims=True)
    le2 = jnp.where(lane == i1, neg, le)
    m2 = jnp.max(le2, axis=-1, keepdims=True)
    i2 = jnp.min(jnp.where(le2 == m2, lane, big), axis=-1, keepdims=True)
    pe1 = 1.0 / ze
    pe2 = jnp.exp(m2 - m1) / ze
    den = pe1 + pe2
    c1 = pe1 / den * p_group
    c2 = pe2 / den * p_group
    e1 = i1 - ng
    e2 = i2 - ng
    route = jnp.where(lane == 0, e1.astype(F32),
            jnp.where(lane == 1, e2.astype(F32),
            jnp.where(lane == 2, c1, jnp.where(lane == 3, c2, 0.0))))
    return route, ((lane == e1) | (lane == e2)).astype(F32)


def _router_terms(w_rg, b_rg, w_re, b_re):
    pad = LANES - w_rg.shape[1] - w_re.shape[1]
    w = jnp.pad(jnp.concatenate([w_rg, w_re], axis=1), ((0, 0), (0, pad)))
    b = jnp.pad(jnp.concatenate([b_rg, b_re]), (0, pad))[None, :]
    w_hi = w.astype(BF16)
    w_lo = (w - w_hi.astype(F32)).astype(BF16)
    return jnp.concatenate([w_hi, w_lo], axis=1), b


def _slots_kernel(route_ref, cnt_ref, pos_ref, ends_ref, off_sc, *, tile_rows):
    i = pl.program_id(0)
    route = route_ref[...]
    tm = route.shape[0]
    lane = _iota2((tm, LANES), 1)
    oh1 = lane == route[:, 0:1].astype(I32)
    oh2 = lane == route[:, 1:2].astype(I32)
    oh = (oh1 | oh2).astype(F32)

    @pl.when(i == 0)
    def _():
        padded = jnp.ceil(cnt_ref[...] * (1.0 / tile_rows)) * tile_rows
        before = (_iota2((LANES, LANES), 0) < _iota2((LANES, LANES), 1)).astype(F32)
        starts = _mm_sel_r(padded, before)
        off_sc[...] = starts
        ends_ref[...] = (starts + padded).astype(I32)

    earlier = (_iota2((tm, tm), 1) < _iota2((tm, tm), 0)).astype(F32)
    base = off_sc[...] + _mm(earlier, oh)
    p1 = jnp.sum(jnp.where(oh1, base, 0.0), axis=-1, keepdims=True)
    p2 = jnp.sum(jnp.where(oh2, base, 0.0), axis=-1, keepdims=True)
    pos_ref[...] = jnp.where(lane == 0, p1, jnp.where(lane == 1, p2, 0.0)).astype(I32)
    off_sc[...] += jnp.sum(oh, axis=0, keepdims=True)


def _slots(route, counts, tm, tile_rows):
    t = route.shape[0]
    return pl.pallas_call(
        functools.partial(_slots_kernel, tile_rows=tile_rows),
        grid=(t // tm,),
        in_specs=[pl.BlockSpec((tm, LANES), lambda i: (i, 0)),
                  pl.BlockSpec((1, LANES), lambda i: (0, 0))],
        out_specs=[pl.BlockSpec((tm, LANES), lambda i: (i, 0)),
                   pl.BlockSpec((1, LANES), lambda i: (0, 0))],
        out_shape=[jax.ShapeDtypeStruct((t, LANES), I32),
                   jax.ShapeDtypeStruct((1, LANES), I32)],
        scratch_shapes=[pltpu.VMEM((1, LANES), F32)],
        compiler_params=_params(("arbitrary",)),
        name="moe_slots",
    )(route, counts)


WIDE_TILE_FACTOR = 2
DMA_UNROLL = 8
COMBINE_CHUNKS = 8


def _scatter_kernel(pos_ref, te_ref, nu_ref, x_ref, xs_ref, zero_sc, sem, zsem, *, tile_rows):
    i = pl.program_id(0)
    tm = x_ref.shape[0]
    n_tiles = xs_ref.shape[0] // tile_rows

    @pl.when(i == 0)
    def _():
        zero_sc[...] = jnp.zeros_like(zero_sc)

        def zcopy(j):
            return pltpu.make_async_copy(zero_sc, xs_ref.at[pl.ds(j * tile_rows, tile_rows)], zsem)

        def fill(j, n):
            partial = (j >= nu_ref[0] - 1) | (te_ref[j] != te_ref[jnp.minimum(j + 1, n_tiles - 1)])

            @pl.when(partial)
            def _():
                zcopy(j).start()
            return n + partial.astype(I32)

        n_fill = lax.fori_loop(0, n_tiles, fill, 0)

        def drain(_, carry):
            zcopy(0).wait()
            return carry
        lax.fori_loop(0, n_fill, drain, 0)

    def copy(t, c):
        p = pos_ref[2 * (i * tm + t) + c]
        return pltpu.make_async_copy(x_ref.at[pl.ds(t, 1)], xs_ref.at[pl.ds(p, 1)], sem)

    def start(t, carry):
        copy(t, 0).start(priority=0)
        copy(t, 1).start(priority=1)
        return carry

    lax.fori_loop(0, tm, start, 0, unroll=DMA_UNROLL)
    for _ in range(2 * tm):
        copy(0, 0).wait()


def _scatter(pos_flat, tile_expert, n_used, x, n_slots, tm, tile_rows):
    t, d = x.shape
    grid_spec = pltpu.PrefetchScalarGridSpec(
        num_scalar_prefetch=3,
        grid=(t // tm,),
        in_specs=[pl.BlockSpec((tm, d), lambda i, pos, te, nu: (i, 0))],
        out_specs=pl.BlockSpec(memory_space=pl.ANY),
        scratch_shapes=[pltpu.VMEM((tile_rows, d), F32), pltpu.SemaphoreType.DMA(()),
                        pltpu.SemaphoreType.DMA(())],
    )
    return pl.pallas_call(
        functools.partial(_scatter_kernel, tile_rows=tile_rows),
        grid_spec=grid_spec,
        out_shape=jax.ShapeDtypeStruct((n_slots, d), F32),
        compiler_params=_params(("arbitrary",)),
        name="moe_scatter",
    )(pos_flat, tile_expert, n_used, x)


def _experts_kernel(te_ref, nx_ref, nu_ref, xs_ref, wg_ref, wu_ref, wd_ref, out_ref, wg_sc, wu_sc, wd_sc,
                    wg_in, wu_in, wd_in, sem, slot_ref, *, expert0):
    j = pl.program_id(0)
    used = j < nu_ref[0]
    prev = te_ref[jnp.maximum(j - 1, 0)]

    def fetches(e, s):
        return [pltpu.make_async_copy(hbm.at[expert0 + e], dst.at[s], sem.at[s])
                for hbm, dst in ((wg_ref, wg_in), (wu_ref, wu_in), (wd_ref, wd_in))]

    @pl.when(used & (j == 0))
    def _():
        slot_ref[0] = 0
        for cp in fetches(te_ref[0], 0):
            cp.start()

    @pl.when(used & ((j == 0) | (te_ref[j] != prev)))
    def _():
        s = slot_ref[0]
        for cp in fetches(te_ref[j], s):
            cp.wait()
        wg_sc[...] = wg_in[s].astype(BF16)
        wu_sc[...] = wu_in[s].astype(BF16)
        wd_sc[...] = wd_in[s].astype(BF16)
        nxt = nx_ref[j]

        @pl.when(nxt >= 0)
        def _():
            for cp in fetches(nxt, 1 - s):
                cp.start()
        slot_ref[0] = 1 - s

    @pl.when(used)
    def _():
        xb = xs_ref[...].astype(BF16)
        h = _silu(jnp.dot(xb, wg_sc[...], preferred_element_type=F32))
        h = h * jnp.dot(xb, wu_sc[...], preferred_element_type=F32)
        out_ref[...] = jnp.dot(h.astype(BF16), wd_sc[...], preferred_element_type=F32)

    @pl.when(jnp.logical_not(used))
    def _():
        out_ref[...] = jnp.zeros_like(out_ref)


def _experts(tile_expert, next_expert, n_used, xs, wg, wu, wd, tile_rows, expert0):
    n_slots, d = xs.shape
    f = wg.shape[-1]

    def rows(j, te, nx, nu):
        return (jnp.maximum(jnp.minimum(j, nu[0] - 1), 0), 0)

    hbm = pl.BlockSpec(memory_space=pl.ANY)
    grid_spec = pltpu.PrefetchScalarGridSpec(
        num_scalar_prefetch=3,
        grid=(n_slots // tile_rows,),
        in_specs=[pl.BlockSpec((tile_rows, d), rows), hbm, hbm, hbm],
        out_specs=pl.BlockSpec((tile_rows, d), lambda j, te, nx, nu: (j, 0)),
        scratch_shapes=[pltpu.VMEM((d, f), BF16), pltpu.VMEM((d, f), BF16), pltpu.VMEM((f, d), BF16),
                        pltpu.VMEM((2, d, f), F32), pltpu.VMEM((2, d, f), F32), pltpu.VMEM((2, f, d), F32),
                        pltpu.SemaphoreType.DMA((2,)), pltpu.SMEM((1,), I32)],
    )
    return pl.pallas_call(
        functools.partial(_experts_kernel, expert0=expert0),
        grid_spec=grid_spec,
        out_shape=jax.ShapeDtypeStruct((n_slots, d), F32),
        compiler_params=_params(("arbitrary",)),
        name="moe_experts",
    )(tile_expert, next_expert, n_used, xs, wg, wu, wd)


def _combine_kernel(pos_ref, x_ref, route_ref, lng_ref, lnb_ref, ys_ref, *rest, alpha, split_tiles):
    if split_tiles is None:
        out_ref, o_sc, sem = rest
    else:
        out_ref, out_s_ref, o_sc, sem = rest
    i = pl.program_id(0)
    n = pl.num_programs(0)
    tm = x_ref.shape[0]
    slot = i % 2
    spare = 1 - slot
    nxt = jnp.minimum(i + 1, n - 1)
    rows_per_chunk = tm // COMBINE_CHUNKS

    def copy(tile, sl, t, c):
        p = pos_ref[2 * (tile * tm + t) + c]
        return pltpu.make_async_copy(ys_ref.at[pl.ds(p, 1)], o_sc.at[sl, c, pl.ds(t, 1)], sem.at[sl])

    def wait_tile(tile, sl):
        for _ in range(2 * tm):
            copy(tile, sl, 0, 0).wait()

    @pl.when(i == 0)
    def _():
        def body(t, carry):
            copy(0, 0, t, 0).start(priority=0)
            copy(0, 0, t, 1).start(priority=1)
            return carry
        lax.fori_loop(0, tm, body, 0, unroll=DMA_UNROLL)

    wait_tile(i, slot)

    def mix_into(out):
        for k in range(COMBINE_CHUNKS):
            rows = slice(k * rows_per_chunk, (k + 1) * rows_per_chunk)
            for t in range(rows.start, rows.stop):
                copy(nxt, spare, t, 0).start(priority=0)
                copy(nxt, spare, t, 1).start(priority=1)
            y = route_ref[rows, 2:3] * o_sc[slot, 0, rows] + route_ref[rows, 3:4] * o_sc[slot, 1, rows]
            out[rows, :] = _layer_norm(alpha * x_ref[rows, :] + y, lng_ref[...], lnb_ref[...])

    if split_tiles is None:
        mix_into(out_ref)
    else:
        @pl.when(i < split_tiles)
        def _():
            mix_into(out_ref)

        @pl.when(i >= split_tiles)
        def _():
            mix_into(out_s_ref)

    @pl.when(i == n - 1)
    def _():
        wait_tile(nxt, spare)


def _combine(pos_flat, x, route, lng, lnb, ys, *, alpha, tm, split_rows=None):
    t, d = x.shape
    out_specs = pl.BlockSpec((tm, d), lambda i, pos: (i, 0))
    out_shape = jax.ShapeDtypeStruct((t, d), F32)
    npt = None
    if split_rows is not None:
        npt = split_rows // tm
        out_specs = [pl.BlockSpec((tm, d), lambda i, pos: (jnp.minimum(i, npt - 1), 0)),
                     pl.BlockSpec((tm, d), lambda i, pos: (jnp.maximum(i - npt, 0), 0))]
        out_shape = [jax.ShapeDtypeStruct((split_rows, d), F32),
                     jax.ShapeDtypeStruct((t - split_rows, d), F32)]
    grid_spec = pltpu.PrefetchScalarGridSpec(
        num_scalar_prefetch=1,
        grid=(t // tm,),
        in_specs=[pl.BlockSpec((tm, d), lambda i, pos: (i, 0)),
                  pl.BlockSpec((tm, LANES), lambda i, pos: (i, 0)),
                  pl.BlockSpec((1, d), lambda i, pos: (0, 0)),
                  pl.BlockSpec((1, d), lambda i, pos: (0, 0)),
                  pl.BlockSpec(memory_space=pl.ANY)],
        out_specs=out_specs,
        scratch_shapes=[pltpu.VMEM((2, 2, tm, d), F32), pltpu.SemaphoreType.DMA((2,))],
    )
    return pl.pallas_call(
        functools.partial(_combine_kernel, alpha=alpha, split_tiles=npt),
        grid_spec=grid_spec,
        out_shape=out_shape,
        compiler_params=_params(("arbitrary",)),
        name="moe_combine",
    )(pos_flat, x, route, lng, lnb, ys)


def _hier_moe_ln(x, route, counts, w_gate, w_up, w_down, lng, lnb, *, alpha, tm, tile_rows,
                 layer=0, split_rows=None):
    t, d = x.shape
    n_exp = EXPERT_GROUPS * EXPERTS_PER_GROUP
    pos, ends = _slots(route, counts, WIDE_TILE_FACTOR * tm, tile_rows)
    pos_flat = pos[:, :2].reshape(-1)
    n_slots = 2 * t + n_exp * tile_rows
    n_tiles = n_slots // tile_rows
    seg_end = ends[0, :n_exp]
    tile_start = jnp.arange(n_tiles, dtype=I32) * tile_rows
    tile_expert = jnp.minimum(jnp.sum(tile_start[:, None] >= seg_end[None, :], axis=1), n_exp - 1).astype(I32)
    n_used = (seg_end[n_exp - 1:] // tile_rows).astype(I32)
    ids = jnp.arange(n_exp, dtype=I32)
    nonempty = seg_end > jnp.concatenate([jnp.zeros((1,), I32), seg_end[:-1]])
    later = jnp.where((ids[None, :] > tile_expert[:, None]) & nonempty[None, :], ids[None, :], n_exp)
    next_expert = jnp.min(later, axis=1)
    next_expert = jnp.where(next_expert < n_exp, next_expert, -1).astype(I32)
    xs = _scatter(pos_flat, tile_expert, n_used, x, n_slots, tm, tile_rows)
    f = w_gate.shape[-1]
    ys = _experts(tile_expert, next_expert, n_used, xs, w_gate.reshape(-1, d, f), w_up.reshape(-1, d, f),
                  w_down.reshape(-1, f, d), tile_rows, layer * n_exp)
    return _combine(pos_flat, x, route, lng, lnb, ys, alpha=alpha, tm=tm, split_rows=split_rows)


def _ssd_in_kernel(x_ref, wz_ref, wx_ref, wdt_ref, z_ref, xbc_ref, dt_ref):
    xb = x_ref[...].astype(BF16)
    z_ref[...] = jnp.dot(xb, wz_ref[...], preferred_element_type=F32)
    xbc_ref[...] = jnp.dot(xb, wx_ref[...], preferred_element_type=F32)
    dt_ref[...] = jnp.dot(xb, wdt_ref[...], preferred_element_type=F32)


def _ssd_in(x, wz, wx, wdt, tm):
    t, d = x.shape
    outs = [wz.shape[1], wx.shape[1], wdt.shape[1]]
    return pl.pallas_call(
        _ssd_in_kernel,
        grid=(t // tm,),
        in_specs=[pl.BlockSpec((tm, d), lambda i: (i, 0))]
                 + [pl.BlockSpec(w.shape, lambda i: (0, 0)) for w in (wz, wx, wdt)],
        out_specs=[pl.BlockSpec((tm, n), lambda i: (i, 0)) for n in outs],
        out_shape=[jax.ShapeDtypeStruct((t, n), F32) for n in outs],
        compiler_params=_params(("parallel",)),
        name="ssd_in",
    )(x, wz, wx, wdt)


def _ssd_kernel(xbc_ref, dt_ref, z_ref, conv0_ref, h0_ref, cw_ref, cb_ref, dtb_ref, alog_ref,
                dexp_ref, nw_ref, expand_ref, y_ref, convo_ref, hout_ref, pad_sc, ht_sc, *, chunk):
    c = chunk
    ci = pl.program_id(1)
    d_inner = z_ref.shape[1]
    gn = SSD_GROUPS * D_STATE
    per_group = d_inner // SSD_GROUPS
    tail = SUBLANES

    @pl.when(ci == 0)
    def _():
        pad_sc[0:tail, :] = conv0_ref[...]
        ht_sc[...] = h0_ref[...].T

    pad_sc[tail:tail + c, :] = xbc_ref[...]
    acc = cb_ref[...]
    for i in range(CONV_WIDTH):
        off = tail - (CONV_WIDTH - 1) + i
        acc = acc + pad_sc[off:off + c, :] * cw_ref[i:i + 1, :]
    xbc = _silu(acc)
    last_rows = pad_sc[c:c + tail, :]
    pad_sc[0:tail, :] = last_rows
    xs = xbc[:, :d_inner]

    dt = _softplus(dt_ref[...] + dtb_ref[...])
    da = dt * (-jnp.exp(alog_ref[...]))
    tril_incl = (_iota2((c, c), 1) <= _iota2((c, c), 0)).astype(F32)
    acum = _mm_sel_l(tril_incl, da)
    a_last = acum[c - 1:c, :]

    def to_cols(t):
        if c < LANES:
            t = jnp.concatenate([t, jnp.zeros((LANES - c, LANES), F32)], axis=0)
        return t.T[:, :c]

    acum_t = to_cols(acum)
    expand = expand_ref[...]
    per_head = jnp.concatenate([jnp.exp(acum), jnp.exp(a_last - acum) * dt, dt], axis=0)
    per_chan = _mm_sel_r(per_head, expand)
    ea, dd = per_chan[:c], per_chan[c:2 * c]
    x_dt = xs * per_chan[2 * c:]
    ea_last = _mm_sel_r(jnp.exp(a_last), expand)
    causal = _iota2((c, c), 1) <= _iota2((c, c), 0)
    lane = _iota2((1, LANES), 1)
    m0 = (lane < HEAD).astype(F32)
    m1 = 1.0 - m0

    ys = []
    for g in range(SSD_GROUPS):
        b_g = xbc[:, d_inner + g * D_STATE:d_inner + (g + 1) * D_STATE]
        c_g = xbc[:, d_inner + gn + g * D_STATE:d_inner + gn + (g + 1) * D_STATE]
        cb = _mm_nt(c_g, b_g)
        gsl = slice(g * per_group, (g + 1) * per_group)
        ht_g = ht_sc[:, gsl]
        y_off = _mm(c_g, ht_g) * ea[:, gsl]
        ht_sc[:, gsl] = ht_g * ea_last[:, gsl] + _mm_tn(b_g, xs[:, gsl] * dd[:, gsl])
        pieces = []
        for q in range(per_group // LANES):
            j0 = (g * per_group) // HEAD + 2 * q
            ms = []
            for j in (j0, j0 + 1):
                seg = acum[:, j:j + 1] - acum_t[j:j + 1, :]
                ms.append(jnp.where(causal, cb * jnp.exp(seg), 0.0))
            lo = g * per_group + q * LANES
            x_pair = x_dt[:, lo:lo + LANES]
            x_stack = jnp.concatenate([x_pair * m0, x_pair * m1], axis=0)
            pieces.append(_mm(jnp.concatenate(ms, axis=1), x_stack))
        y_g = jnp.concatenate(pieces, axis=1) + y_off + xs[:, gsl] * dexp_ref[:, gsl]
        y_g = y_g * _silu(z_ref[:, gsl])
        y_g = y_g * lax.rsqrt(jnp.mean(y_g * y_g, axis=-1, keepdims=True) + RMS_EPS)
        ys.append(y_g * nw_ref[:, gsl])
    y_ref[...] = jnp.concatenate(ys, axis=1)

    @pl.when(ci == pl.num_programs(1) - 1)
    def _():
        convo_ref[...] = last_rows
        hout_ref[...] = ht_sc[...].T


def _ssd(xbc, dt, z, conv0, h0, cw, cb, dtb, alog, dexp, nw, expand, *, batch, seq, row0, chunk):
    d_inner = z.shape[1]
    conv_dim = xbc.shape[1]
    nc = seq // chunk
    blk0 = row0 // chunk

    def tok(n):
        return pl.BlockSpec((chunk, n), lambda b, c: (blk0 + b * nc + c, 0))

    def par(arr):
        return pl.BlockSpec(arr.shape, lambda b, c: (0, 0))

    return pl.pallas_call(
        functools.partial(_ssd_kernel, chunk=chunk),
        grid=(batch, nc),
        in_specs=[tok(conv_dim), tok(LANES), tok(d_inner),
                  pl.BlockSpec((None, SUBLANES, conv_dim), lambda b, c: (b, 0, 0)),
                  pl.BlockSpec((None, d_inner, D_STATE), lambda b, c: (b, 0, 0)),
                  par(cw), par(cb), par(dtb), par(alog), par(dexp), par(nw), par(expand)],
        out_specs=[pl.BlockSpec((chunk, d_inner), lambda b, c: (b * nc + c, 0)),
                   pl.BlockSpec((None, SUBLANES, conv_dim), lambda b, c: (b, 0, 0)),
                   pl.BlockSpec((None, d_inner, D_STATE), lambda b, c: (b, 0, 0))],
        out_shape=[jax.ShapeDtypeStruct((batch * seq, d_inner), F32),
                   jax.ShapeDtypeStruct((batch, SUBLANES, conv_dim), F32),
                   jax.ShapeDtypeStruct((batch, d_inner, D_STATE), F32)],
        scratch_shapes=[pltpu.VMEM((chunk + SUBLANES, conv_dim), F32), pltpu.VMEM((D_STATE, d_inner), F32)],
        compiler_params=_params(("parallel", "arbitrary")),
        name="ssd_l%d" % chunk,
    )(xbc, dt, z, conv0, h0, cw, cb, dtb, alog, dexp, nw, expand)


def _rwkv_layer(xp3, xs3, shift_s, wkv_s, prm, lng, lnb, router, *, alpha, tm):
    (mu, w_rkv, w0, w1, w2, a0, a1, a2, g1, g2, k_k, k_a, r_k, gn_w, gn_b, w_o) = prm
    bp, sp, d = xp3.shape
    bs, ss, _ = xs3.shape
    tp = bp * sp
    heads = d // HEAD
    x_p = xp3.reshape(tp, d)
    x_s = xs3.reshape(bs * ss, d)
    prev_s = jnp.concatenate([shift_s[:, None, :], xs3[:, :-1]], axis=1).reshape(bs * ss, d)
    row = lambda v: v.reshape(1, d)
    r, k, v, lw, a, g = _rwkv_pre(x_p, x_s, prev_s, sp, mu, w_rkv.astype(BF16), row(w0), w1.astype(BF16),
                                  w2.astype(BF16), row(a0), a1.astype(BF16), a2.astype(BF16),
                                  g1.astype(BF16), g2.astype(BF16), tm)
    pv = (row(k_k), row(k_a), row(r_k), row(gn_w), row(gn_b))
    o_p, sp_out = _wkv(r, k, v, lw, a, *pv, None, batch=bp, seq=sp, row0=0,
                       chunk=min(WKV_CHUNK, sp), pairs_per_step=min(WKV_PAIRS_PER_STEP, heads // 2))
    chunk_s = min(WKV_CHUNK, ss)
    sps = WKV_SHORT_SEQS_PER_STEP
    if ss != chunk_s or bs % sps or tp % (sps * chunk_s):
        sps = 1
    o_s, ss_out = _wkv(r, k, v, lw, a, *pv, wkv_s, batch=bs, seq=ss, row0=tp, chunk=chunk_s,
                       pairs_per_step=heads // 2, seqs_per_step=sps)
    routed = _proj_ln(o_p, o_s, g, (x_p, x_s, 0), w_o.astype(BF16), lng, lnb, *router, alpha=alpha,
                      tm=WIDE_TILE_FACTOR * tm)
    return routed, sp_out, ss_out


def _ssd_layer(x, bp, sp, bs, ss, conv_s, ssm_s, prm, lng, lnb, router, *, alpha, tm):
    (w_in, conv_w, conv_b, dt_bias, a_log, d_skip, norm_w, w_out) = prm
    t, d = x.shape
    tp = bp * sp
    heads = a_log.shape[0]
    d_inner = heads * HEAD
    conv_dim = conv_w.shape[1]
    wz = w_in[:, :d_inner].astype(BF16)
    wx = w_in[:, d_inner:d_inner + conv_dim].astype(BF16)
    wdt = jnp.pad(w_in[:, d_inner + conv_dim:], ((0, 0), (0, LANES - heads))).astype(BF16)
    wide = WIDE_TILE_FACTOR * tm
    z, xbc, dt = _ssd_in(x, wz, wx, wdt, wide)
    lane_pad = lambda v: jnp.pad(v, (0, LANES - heads))[None, :]
    expand = (jnp.arange(LANES)[:, None] == (jnp.arange(d_inner) // HEAD)[None, :]).astype(F32)
    common = (conv_w, conv_b[None, :], lane_pad(dt_bias), lane_pad(a_log),
              jnp.repeat(d_skip, HEAD)[None, :], norm_w[None, :], expand)
    conv_tail = CONV_WIDTH - 1
    pad_conv = lambda cs: jnp.pad(cs, ((0, 0), (SUBLANES - conv_tail, 0), (0, 0)))
    y_p, conv_p, h_p = _ssd(xbc, dt, z, jnp.zeros((bp, SUBLANES, conv_dim), F32),
                            jnp.zeros((bp, d_inner, D_STATE), F32), *common,
                            batch=bp, seq=sp, row0=0, chunk=SSD_CHUNK if sp % SSD_CHUNK == 0 else sp)
    y_s, conv_s_out, h_s = _ssd(xbc, dt, z, pad_conv(conv_s), ssm_s.reshape(bs, d_inner, D_STATE), *common,
                                batch=bs, seq=ss, row0=tp, chunk=SSD_CHUNK if ss % SSD_CHUNK == 0 else ss)
    routed = _proj_ln(y_p, y_s, None, (x, x, tp // wide), w_out.astype(BF16), lng, lnb, *router, alpha=alpha,
                      tm=wide)
    shape_h = lambda h, b: h.reshape(b, heads, HEAD, D_STATE)
    return (routed, conv_p[:, SUBLANES - conv_tail:], conv_s_out[:, SUBLANES - conv_tail:], shape_h(h_p, bp),
            shape_h(h_s, bs))


def kernel(x_prompt, x_sample, state_rwkv_wkv, state_rwkv_shift, state_ssd_ssm, state_ssd_conv,
           rwkv_mu, rwkv_w_rkv, rwkv_w0, rwkv_w1, rwkv_w2, rwkv_a0, rwkv_a1, rwkv_a2,
           rwkv_g1, rwkv_g2, rwkv_k_k, rwkv_k_a, rwkv_r_k, rwkv_gn_w, rwkv_gn_b, rwkv_w_o,
           ssd_w_in, ssd_conv_w, ssd_conv_b, ssd_dt_bias, ssd_a_log, ssd_d, ssd_norm_w, ssd_w_out,
           ln_gain, ln_bias, moe_w_rg, moe_b_rg, moe_w_re, moe_b_re, moe_w_gate, moe_w_up, moe_w_down):
    bp, sp, d = x_prompt.shape
    bs, ss, _ = x_sample.shape
    depth = ln_gain.shape[0]
    alpha = (2 * depth) ** 0.25
    tp, ts = bp * sp, bs * ss
    tm = 256
    tile_rows = 256
    x = None
    xp3, xs3 = x_prompt, x_sample
    wkv_p, wkv_s, sh_p, sh_s, ssm_p, ssm_s, cv_p, cv_s = [], [], [], [], [], [], [], []
    for i in range(depth):
        j = i // 2
        lng = lambda n: ln_gain[i, n][None, :]
        lnb = lambda n: ln_bias[i, n][None, :]
        router = _router_terms(moe_w_rg[i], moe_b_rg[i], moe_w_re[i], moe_b_re[i])
        if i % 2 == 0:
            if x is not None:
                xp3, xs3 = x[:tp].reshape(bp, sp, d), x[tp:].reshape(bs, ss, d)
            prm = (rwkv_mu[j], rwkv_w_rkv[j], rwkv_w0[j], rwkv_w1[j], rwkv_w2[j], rwkv_a0[j],
                   rwkv_a1[j], rwkv_a2[j], rwkv_g1[j], rwkv_g2[j], rwkv_k_k[j], rwkv_k_a[j],
                   rwkv_r_k[j].reshape(-1), rwkv_gn_w[j], rwkv_gn_b[j], rwkv_w_o[j])
            sh_p.append(xp3[:, -1])
            sh_s.append(xs3[:, -1])
            routed, w_p, w_s = _rwkv_layer(xp3, xs3, state_rwkv_shift[j], state_rwkv_wkv[j], prm,
                                           lng(0), lnb(0), router, alpha=alpha, tm=tm)
            wkv_p.append(w_p)
            wkv_s.append(w_s)
        else:
            prm = (ssd_w_in[j], ssd_conv_w[j], ssd_conv_b[j], ssd_dt_bias[j], ssd_a_log[j],
                   ssd_d[j], ssd_norm_w[j], ssd_w_out[j])
            routed, c_p, c_s, h_p, h_s = _ssd_layer(x, bp, sp, bs, ss, state_ssd_conv[j], state_ssd_ssm[j],
                                                    prm, lng(0), lnb(0), router, alpha=alpha, tm=tm)
            cv_p.append(c_p)
            cv_s.append(c_s)
            ssm_p.append(h_p)
            ssm_s.append(h_s)
        x = _hier_moe_ln(*routed, moe_w_gate, moe_w_up, moe_w_down, lng(1), lnb(1), alpha=alpha, tm=tm,
                         tile_rows=tile_rows, layer=i, split_rows=tp if i == depth - 1 else None)
    yp = x[0].reshape(bp, sp, d)
    ys = x[1].reshape(bs, ss, d)
    return (yp, ys, jnp.stack(wkv_p), jnp.stack(wkv_s), jnp.stack(sh_p), jnp.stack(sh_s),
            jnp.stack(ssm_p), jnp.stack(ssm_s), jnp.stack(cv_p), jnp.stack(cv_s))
```

```python
import functools
import math

import jax
import jax.numpy as jnp
from jax import lax
from jax.experimental import pallas as pl
from jax.experimental.pallas import tpu as pltpu

F32 = jnp.float32
BF16 = jnp.bfloat16
I32 = jnp.int32

LANES = 128
SUBLANES = 8
HEAD = 64
GN_EPS = HEAD * 1e-5
RMS_EPS = 1e-5
LN_EPS = 1e-5
WKV_CHUNK = 64
WKV_PAIRS_PER_STEP = 8
WKV_SHORT_SEQS_PER_STEP = 8
SSD_CHUNK = 128
SSD_GROUPS = 4
D_STATE = 128
CONV_WIDTH = 4
EXPERT_GROUPS = 4
EXPERTS_PER_GROUP = 8
VMEM_LIMIT = 56 * 1024 * 1024


def _params(sem):
    return pltpu.CompilerParams(dimension_semantics=sem, vmem_limit_bytes=VMEM_LIMIT)


def _mm(a, b):
    return jnp.dot(a.astype(BF16), b.astype(BF16), preferred_element_type=F32)


def _mm_nt(a, b):
    return lax.dot_general(a.astype(BF16), b.astype(BF16), (((1,), (1,)), ((), ())),
                           preferred_element_type=F32)


def _mm_tn(a, b):
    return lax.dot_general(a.astype(BF16), b.astype(BF16), (((0,), (0,)), ((), ())),
                           preferred_element_type=F32)


def _split3(x):
    hi = x.astype(BF16)
    r1 = x - hi.astype(F32)
    mid = r1.astype(BF16)
    lo = (r1 - mid.astype(F32)).astype(BF16)
    return hi, mid, lo


def _mm_sel_l(sel, x):
    k = x.shape[1]
    parts = jnp.concatenate([p.astype(F32) for p in _split3(x)], axis=1).astype(BF16)
    y = jnp.dot(sel.astype(BF16), parts, preferred_element_type=F32)
    return y[:, :k] + y[:, k:2 * k] + y[:, 2 * k:]


def _mm_sel_r(x, sel):
    n = x.shape[0]
    if n % SUBLANES:
        return sum(jnp.dot(p, sel.astype(BF16), preferred_element_type=F32) for p in _split3(x))
    parts = jnp.concatenate([p.astype(F32) for p in _split3(x)], axis=0).astype(BF16)
    y = jnp.dot(parts, sel.astype(BF16), preferred_element_type=F32)
    return y[:n] + y[n:2 * n] + y[2 * n:]


def _sigmoid(x):
    return 1.0 / (1.0 + jnp.exp(-x))


def _silu(x):
    return x * _sigmoid(x)


def _softplus(x):
    return jnp.maximum(x, 0.0) + jnp.log(1.0 + jnp.exp(-jnp.abs(x)))


def _layer_norm(x, g, b):
    mu = jnp.mean(x, axis=-1, keepdims=True)
    xc = x - mu
    var = jnp.mean(xc * xc, axis=-1, keepdims=True)
    return xc * lax.rsqrt(var + LN_EPS) * g + b


def _iota2(shape, dim):
    return lax.broadcasted_iota(I32, shape, dim)


def _part_specs(tm, width, prompt_tiles, sample_block0=0):
    return [pl.BlockSpec((tm, width), lambda i: (jnp.minimum(i, prompt_tiles - 1), 0)),
            pl.BlockSpec((tm, width), lambda i: (sample_block0 + jnp.maximum(i - prompt_tiles, 0), 0))]


def _rwkv_pre_kernel(xa_ref, xb_ref, tail_ref, xprev_s_ref, mu_ref, wrkv_ref, w0_ref, w1_ref, w2_ref,
                     a0_ref, a1_ref, a2_ref, g1_ref, g2_ref, r_ref, k_ref, v_ref, lw_ref, a_ref, g_ref,
                     *, prompt_tiles, seq):
    i = pl.program_id(0)
    tm = xa_ref.shape[0]
    is_prompt = i < prompt_tiles
    xa = xa_ref[...]
    x = jnp.where(is_prompt, xa, xb_ref[...])
    first = jnp.where((i * tm) % seq == 0, 0.0, tail_ref[SUBLANES - 1:SUBLANES, :])
    shifted = jnp.where(_iota2(xa.shape, 0) == 0, first, pltpu.roll(xa, 1, axis=0))
    xx = jnp.where(is_prompt, shifted, xprev_s_ref[...]) - x

    def mix(j):
        return x + xx * mu_ref[j:j + 1, :]

    r_ref[...] = _mm(mix(0), wrkv_ref[0])
    k_ref[...] = _mm(mix(1), wrkv_ref[1])
    v_ref[...] = _mm(mix(2), wrkv_ref[2])
    ww = w0_ref[...] + _mm(jnp.tanh(_mm(mix(3), w1_ref[...])), w2_ref[...])
    w_raw = -_softplus(-ww) - 0.5
    lw_ref[...] = -jnp.exp(w_raw)
    a_ref[...] = _sigmoid(a0_ref[...] + _mm(_mm(mix(4), a1_ref[...]), a2_ref[...]))
    g_ref[...] = _mm(_sigmoid(_mm(mix(5), g1_ref[...])), g2_ref[...])


def _rwkv_pre(x_p, x_s, xprev_s, seq, mu, wrkv, w0, w1, w2, a0, a1, a2, g1, g2, tm):
    (tp, d), ts = x_p.shape, x_s.shape[0]
    assert seq % tm == 0 and ts % tm == 0 and tm % SUBLANES == 0
    npt = tp // tm
    row = pl.BlockSpec((tm, d), lambda i: (i, 0))
    tail = pl.BlockSpec((SUBLANES, d),
                        lambda i: (jnp.maximum(jnp.minimum(i, npt - 1) * (tm // SUBLANES) - 1, 0), 0))

    def full(arr):
        nd = arr.ndim
        return pl.BlockSpec(arr.shape, lambda i, _n=nd: (0,) * _n)

    weights = (mu, wrkv, w0, w1, w2, a0, a1, a2, g1, g2)
    return pl.pallas_call(
        functools.partial(_rwkv_pre_kernel, prompt_tiles=npt, seq=seq),
        grid=((tp + ts) // tm,),
        in_specs=_part_specs(tm, d, npt) + [tail, _part_specs(tm, d, npt)[1]] + [full(w) for w in weights],
        out_specs=[row] * 6,
        out_shape=[jax.ShapeDtypeStruct((tp + ts, d), F32)] * 6,
        compiler_params=_params(("parallel",)),
        name="rwkv_pre",
    )(x_p, x_s, x_p, xprev_s, *weights)


def _wkv_kernel(r_ref, k_ref, v_ref, lw_ref, a_ref, kkw_ref, kaw_ref, rk_ref, gnw_ref, gnb_ref,
                *rest, chunk, pairs_per_step, seqs_per_step, has_init):
    if has_init:
        s0_ref, o_ref, sout_ref, s_sc = rest
    else:
        o_ref, sout_ref, s_sc = rest
    L = chunk
    c = pl.program_id(2)
    units = [(s, q) for s in range(seqs_per_step) for q in range(pairs_per_step)]

    @pl.when(c == 0)
    def _():
        if has_init:
            zero = jnp.zeros((HEAD, HEAD), F32)
            for u, (s, q) in enumerate(units):
                top = jnp.concatenate([s0_ref[s, 2 * q], zero], axis=1)
                bot = jnp.concatenate([zero, s0_ref[s, 2 * q + 1]], axis=1)
                s_sc[u] = jnp.concatenate([top, bot], axis=0)
        else:
            s_sc[...] = jnp.zeros_like(s_sc)

    lane = _iota2((1, LANES), 1)
    m0 = (lane < HEAD).astype(F32)
    m1 = 1.0 - m0
    same_head = (_iota2((LANES, LANES), 0) // HEAD == _iota2((LANES, LANES), 1) // HEAD).astype(F32)
    tril_incl = (_iota2((L, L), 1) <= _iota2((L, L), 0)).astype(F32)
    n2 = 2 * L
    row = _iota2((n2, n2), 0)
    col = _iota2((n2, n2), 1)
    strict = col < row
    incl = col <= row
    steps = int(math.log2(L))

    def headsum(t):
        return _mm_sel_r(t, same_head)

    def stack(t):
        return jnp.concatenate([t * m0, t * m1], axis=0)

    pairs = range(len(units))
    rows = [slice(s * L, (s + 1) * L) for s, _ in units]
    lanes = [slice(q * LANES, (q + 1) * LANES) for _, q in units]

    def prepare(rs, ls):
        r = r_ref[rs, ls]
        k = k_ref[rs, ls]
        v = v_ref[rs, ls]
        lw = lw_ref[rs, ls]
        a = a_ref[rs, ls]
        kk = k * kkw_ref[:, ls]
        k2 = k * (1.0 + (a - 1.0) * kaw_ref[:, ls])
        sums = headsum(jnp.concatenate([kk * kk, r * k2 * rk_ref[:, ls]], axis=0))
        kk = kk / jnp.maximum(jnp.sqrt(sums[:L]), 1e-12)
        bonus = sums[L:] * v
        bvec = kk * a
        cl = _mm_sel_l(tril_incl, lw)
        total = cl[L - 1:L, :]
        w_in = jnp.exp(cl)
        w_inv = jnp.exp(-cl)
        w_end = jnp.exp(total - cl)
        return dict(
            ar=jnp.concatenate([stack(-kk * jnp.exp(cl - lw)), stack(r * w_in)], axis=0).astype(BF16),
            bk=jnp.concatenate([stack(bvec * w_inv), stack(k2 * w_inv)], axis=0).astype(BF16),
            bk_end=jnp.concatenate([stack(bvec * w_end), stack(k2 * w_end)], axis=0),
            v_s=stack(v), total=total, bonus=bonus)

    ops = [prepare(rs, ls) for rs, ls in zip(rows, lanes)]
    states = [s_sc[q] for q in pairs]
    cross = [_mm_nt(o["ar"], o["bk"]) for o in ops]
    from_s = [_mm_nt(o["ar"], s) for o, s in zip(ops, states)]
    n_ab = [jnp.where(strict, x[:n2, :n2], 0.0) for x in cross]
    a_rb = [jnp.where(incl, x[n2:, :n2], 0.0) for x in cross]
    to_v = [jnp.concatenate([jnp.where(strict, x[:n2, n2:], 0.0), jnp.where(incl, x[n2:, n2:], 0.0)], axis=0)
            for x in cross]
    from_v = [_mm(m, o["v_s"]) for m, o in zip(to_v, ops)]

    u = [fs[:n2] + fv[:n2] for fs, fv in zip(from_s, from_v)]
    p = n_ab
    for j in range(steps):
        if j + 1 < steps:
            both = [_mm(pi, jnp.concatenate([ui, pi], axis=1)) for ui, pi in zip(u, p)]
            u = [ui + b[:, :LANES] for ui, b in zip(u, both)]
            p = [b[:, LANES:] for b in both]
        else:
            u = [ui + _mm(pi, ui) for ui, pi in zip(u, p)]

    y_s = [fs[n2:] + fv[n2:] + _mm(mb, ui) for fs, fv, mb, ui in zip(from_s, from_v, a_rb, u)]
    s_new = [s * jnp.exp(o["total"]) + _mm_tn(jnp.concatenate([ui, o["v_s"]], axis=0), o["bk_end"])
             for o, s, ui in zip(ops, states, u)]
    for q in pairs:
        s_sc[q] = s_new[q]
        y = y_s[q][:L] + y_s[q][L:]
        ym = headsum(y) * (1.0 / HEAD)
        yc = y - ym
        yv = headsum(yc * yc) * (1.0 / HEAD)
        rs, ls = rows[q], lanes[q]
        o_ref[rs, ls] = yc * lax.rsqrt(yv + GN_EPS) * gnw_ref[:, ls] + gnb_ref[:, ls] + ops[q]["bonus"]

    @pl.when(c == pl.num_programs(2) - 1)
    def _():
        for u, (s, q) in enumerate(units):
            sout_ref[s, 2 * q] = s_new[u][:HEAD, :HEAD]
            sout_ref[s, 2 * q + 1] = s_new[u][HEAD:, HEAD:]


def _wkv(r, k, v, lw, a, kkw, kaw, rk, gnw, gnb, s0, *, batch, seq, row0, chunk, pairs_per_step,
         seqs_per_step=1):
    d = r.shape[1]
    pps, sps = pairs_per_step, seqs_per_step
    width = pps * LANES
    nc = seq // chunk
    assert batch % sps == 0 and (sps == 1 or nc == 1) and row0 % (sps * chunk) == 0
    blk0 = row0 // (sps * chunk)
    tok = pl.BlockSpec((sps * chunk, width), lambda b, p, c: (blk0 + b * nc + c, p))
    out = pl.BlockSpec((sps * chunk, width), lambda b, p, c: (b * nc + c, p))
    par = pl.BlockSpec((1, width), lambda b, p, c: (0, p))
    st = pl.BlockSpec((sps, 2 * pps, HEAD, HEAD), lambda b, p, c: (b, p, 0, 0))
    has_init = s0 is not None
    return pl.pallas_call(
        functools.partial(_wkv_kernel, chunk=chunk, pairs_per_step=pps, seqs_per_step=sps,
                          has_init=has_init),
        grid=(batch // sps, d // width, nc),
        in_specs=[tok] * 5 + [par] * 5 + ([st] if has_init else []),
        out_specs=[out, st],
        out_shape=[jax.ShapeDtypeStruct((batch * seq, d), F32),
                   jax.ShapeDtypeStruct((batch, d // HEAD, HEAD, HEAD), F32)],
        scratch_shapes=[pltpu.VMEM((sps * pps, LANES, LANES), F32)],
        compiler_params=_params(("parallel", "parallel", "arbitrary")),
        name="wkv_l%d" % chunk,
    )(r, k, v, lw, a, kkw, kaw, rk, gnw, gnb, *([s0] if has_init else []))


def _proj_ln_kernel(op_ref, os_ref, g_ref, xa_ref, xb_ref, w_ref, lng_ref, lnb_ref, rw_ref, rb_ref, out_ref,
                    route_ref, cnt_ref, *, alpha, gated, prompt_tiles):
    is_prompt = pl.program_id(0) < prompt_tiles
    o = jnp.where(is_prompt, op_ref[...], os_ref[...].astype(op_ref.dtype))
    if gated:
        o = o * g_ref[...]
    y = _mm(o, w_ref[...])
    x = jnp.where(is_prompt, xa_ref[...], xb_ref[...])
    x1 = _layer_norm(alpha * x + y, lng_ref[...], lnb_ref[...])
    out_ref[...] = x1
    route, chosen = _route(x1, rw_ref, rb_ref)
    route_ref[...] = route

    @pl.when(pl.program_id(0) == 0)
    def _():
        cnt_ref[...] = jnp.zeros_like(cnt_ref)

    cnt_ref[...] += jnp.sum(chosen, axis=0, keepdims=True)


def _proj_ln(o_p, o_s, g, x_parts, w, lng, lnb, router_w, router_b, *, alpha, tm):
    x_a, x_b, sample_block0 = x_parts
    d = x_a.shape[1]
    kin = o_p.shape[1]
    npt = o_p.shape[0] // tm
    t = o_p.shape[0] + o_s.shape[0]
    gated = g is not None
    g_spec = pl.BlockSpec((tm, kin), lambda i: (i, 0))
    if not gated:
        g, g_spec = lng, pl.BlockSpec((1, d), lambda i: (0, 0))
    const = lambda arr: pl.BlockSpec(arr.shape, lambda i: (0, 0))
    return pl.pallas_call(
        functools.partial(_proj_ln_kernel, alpha=alpha, gated=gated, prompt_tiles=npt),
        grid=(t // tm,),
        in_specs=_part_specs(tm, kin, npt) + [g_spec] + _part_specs(tm, d, npt, sample_block0)
                 + [const(w), const(lng), const(lnb), const(router_w), const(router_b)],
        out_specs=[pl.BlockSpec((tm, d), lambda i: (i, 0)),
                   pl.BlockSpec((tm, LANES), lambda i: (i, 0)),
                   pl.BlockSpec((1, LANES), lambda i: (0, 0))],
        out_shape=[jax.ShapeDtypeStruct((t, d), F32),
                   jax.ShapeDtypeStruct((t, LANES), F32),
                   jax.ShapeDtypeStruct((1, LANES), F32)],
        compiler_params=_params(("arbitrary",)),
        name="proj_ln",
    )(o_p, o_s, g, x_a, x_b, w, lng, lnb, router_w, router_b)


def _route(x, w_ref, b_ref):
    ng, epg = EXPERT_GROUPS, EXPERTS_PER_GROUP
    x_hi = x.astype(BF16)
    x_lo = (x - x_hi.astype(F32)).astype(BF16)
    by_hi = jnp.dot(x_hi, w_ref[...], preferred_element_type=F32)
    logits = (by_hi[:, :LANES] + by_hi[:, LANES:]
              + jnp.dot(x_lo, w_ref[:, :LANES], preferred_element_type=F32)
              + b_ref[...])
    lane = _iota2(logits.shape, 1)
    neg = jnp.float32(-jnp.inf)
    big = jnp.int32(1 << 20)

    is_g = lane < ng
    lg = jnp.where(is_g, logits, neg)
    mg = jnp.max(lg, axis=-1, keepdims=True)
    g_sel = jnp.min(jnp.where(lg == mg, lane, big), axis=-1, keepdims=True)
    zg = jnp.sum(jnp.where(is_g, jnp.exp(lg - mg), 0.0), axis=-1, keepdims=True)
    p_group = 1.0 / zg

    e_lane = lane - ng
    in_sel = (e_lane >= g_sel * epg) & (e_lane < (g_sel + 1) * epg)
    le = jnp.where(in_sel, logits, neg)
    m1 = jnp.max(le, axis=-1, keepdims=True)
    i1 = jnp.min(jnp.where(le == m1, lane, big), axis=-1, keepdims=True)
    ze = jnp.sum(jnp.where(in_sel, jnp.exp(le - m1), 0.0), axis=-1, keepdims=True)
    le2 = jnp.where(lane == i1, neg, le)
    m2 = jnp.max(le2, axis=-1, keepdims=True)
    i2 = jnp.min(jnp.where(le2 == m2, lane, big), axis=-1, keepdims=True)
    pe1 = 1.0 / ze
    pe2 = jnp.exp(m2 - m1) / ze
    den = pe1 + pe2
    c1 = pe1 / den * p_group
    c2 = pe2 / den * p_group
    e1 = i1 - ng
    e2 = i2 - ng
    route = jnp.where(lane == 0, e1.astype(F32),
            jnp.where(lane == 1, e2.astype(F32),
            jnp.where(lane == 2, c1, jnp.where(lane == 3, c2, 0.0))))
    return route, ((lane == e1) | (lane == e2)).astype(F32)


def _router_terms(w_rg, b_rg, w_re, b_re):
    pad = LANES - w_rg.shape[1] - w_re.shape[1]
    w = jnp.pad(jnp.concatenate([w_rg, w_re], axis=1), ((0, 0), (0, pad)))
    b = jnp.pad(jnp.concatenate([b_rg, b_re]), (0, pad))[None, :]
    w_hi = w.astype(BF16)
    w_lo = (w - w_hi.astype(F32)).astype(BF16)
    return jnp.concatenate([w_hi, w_lo], axis=1), b


def _slots_kernel(route_ref, cnt_ref, pos_ref, ends_ref, off_sc, *, tile_rows):
    i = pl.program_id(0)
    route = route_ref[...]
    tm = route.shape[0]
    lane = _iota2((tm, LANES), 1)
    oh1 = lane == route[:, 0:1].astype(I32)
    oh2 = lane == route[:, 1:2].astype(I32)
    oh = (oh1 | oh2).astype(F32)

    @pl.when(i == 0)
    def _():
        padded = jnp.ceil(cnt_ref[...] * (1.0 / tile_rows)) * tile_rows
        before = (_iota2((LANES, LANES), 0) < _iota2((LANES, LANES), 1)).astype(F32)
        starts = _mm_sel_r(padded, before)
        off_sc[...] = starts
        ends_ref[...] = (starts + padded).astype(I32)

    earlier = (_iota2((tm, tm), 1) < _iota2((tm, tm), 0)).astype(F32)
    base = off_sc[...] + _mm(earlier, oh)
    p1 = jnp.sum(jnp.where(oh1, base, 0.0), axis=-1, keepdims=True)
    p2 = jnp.sum(jnp.where(oh2, base, 0.0), axis=-1, keepdims=True)
    pos_ref[...] = jnp.where(lane == 0, p1, jnp.where(lane == 1, p2, 0.0)).astype(I32)
    off_sc[...] += jnp.sum(oh, axis=0, keepdims=True)


def _slots(route, counts, tm, tile_rows):
    t = route.shape[0]
    return pl.pallas_call(
        functools.partial(_slots_kernel, tile_rows=tile_rows),
        grid=(t // tm,),
        in_specs=[pl.BlockSpec((tm, LANES), lambda i: (i, 0)),
                  pl.BlockSpec((1, LANES), lambda i: (0, 0))],
        out_specs=[pl.BlockSpec((tm, LANES), lambda i: (i, 0)),
                   pl.BlockSpec((1, LANES), lambda i: (0, 0))],
        out_shape=[jax.ShapeDtypeStruct((t, LANES), I32),
                   jax.ShapeDtypeStruct((1, LANES), I32)],
        scratch_shapes=[pltpu.VMEM((1, LANES), F32)],
        compiler_params=_params(("arbitrary",)),
        name="moe_slots",
    )(route, counts)


WIDE_TILE_FACTOR = 2
DMA_UNROLL = 8
COMBINE_CHUNKS = 8


def _scatter_kernel(pos_ref, te_ref, nu_ref, x_ref, xs_ref, zero_sc, sem, zsem, *, tile_rows):
    i = pl.program_id(0)
    tm = x_ref.shape[0]
    n_tiles = xs_ref.shape[0] // tile_rows

    @pl.when(i == 0)
    def _():
        zero_sc[...] = jnp.zeros_like(zero_sc)

        def zcopy(j):
            return pltpu.make_async_copy(zero_sc, xs_ref.at[pl.ds(j * tile_rows, tile_rows)], zsem)

        def fill(j, n):
            partial = (j >= nu_ref[0] - 1) | (te_ref[j] != te_ref[jnp.minimum(j + 1, n_tiles - 1)])

            @pl.when(partial)
            def _():
                zcopy(j).start()
            return n + partial.astype(I32)

        n_fill = lax.fori_loop(0, n_tiles, fill, 0)

        def drain(_, carry):
            zcopy(0).wait()
            return carry
        lax.fori_loop(0, n_fill, drain, 0)

    def copy(t, c):
        p = pos_ref[2 * (i * tm + t) + c]
        return pltpu.make_async_copy(x_ref.at[pl.ds(t, 1)], xs_ref.at[pl.ds(p, 1)], sem)

    def start(t, carry):
        copy(t, 0).start(priority=0)
        copy(t, 1).start(priority=1)
        return carry

    lax.fori_loop(0, tm, start, 0, unroll=DMA_UNROLL)
    for _ in range(2 * tm):
        copy(0, 0).wait()


def _scatter(pos_flat, tile_expert, n_used, x, n_slots, tm, tile_rows):
    t, d = x.shape
    grid_spec = pltpu.PrefetchScalarGridSpec(
        num_scalar_prefetch=3,
        grid=(t // tm,),
        in_specs=[pl.BlockSpec((tm, d), lambda i, pos, te, nu: (i, 0))],
        out_specs=pl.BlockSpec(memory_space=pl.ANY),
        scratch_shapes=[pltpu.VMEM((tile_rows, d), F32), pltpu.SemaphoreType.DMA(()),
                        pltpu.SemaphoreType.DMA(())],
    )
    return pl.pallas_call(
        functools.partial(_scatter_kernel, tile_rows=tile_rows),
        grid_spec=grid_spec,
        out_shape=jax.ShapeDtypeStruct((n_slots, d), F32),
        compiler_params=_params(("arbitrary",)),
        name="moe_scatter",
    )(pos_flat, tile_expert, n_used, x)


def _experts_kernel(te_ref, nx_ref, nu_ref, xs_ref, wg_ref, wu_ref, wd_ref, out_ref, wg_sc, wu_sc, wd_sc,
                    wg_in, wu_in, wd_in, sem, slot_ref, *, expert0):
    j = pl.program_id(0)
    used = j < nu_ref[0]
    prev = te_ref[jnp.maximum(j - 1, 0)]

    def fetches(e, s):
        return [pltpu.make_async_copy(hbm.at[expert0 + e], dst.at[s], sem.at[s])
                for hbm, dst in ((wg_ref, wg_in), (wu_ref, wu_in), (wd_ref, wd_in))]

    @pl.when(used & (j == 0))
    def _():
        slot_ref[0] = 0
        for cp in fetches(te_ref[0], 0):
            cp.start()

    @pl.when(used & ((j == 0) | (te_ref[j] != prev)))
    def _():
        s = slot_ref[0]
        for cp in fetches(te_ref[j], s):
            cp.wait()
        wg_sc[...] = wg_in[s].astype(BF16)
        wu_sc[...] = wu_in[s].astype(BF16)
        wd_sc[...] = wd_in[s].astype(BF16)
        nxt = nx_ref[j]

        @pl.when(nxt >= 0)
        def _():
            for cp in fetches(nxt, 1 - s):
                cp.start()
        slot_ref[0] = 1 - s

    @pl.when(used)
    def _():
        xb = xs_ref[...].astype(BF16)
        h = _silu(jnp.dot(xb, wg_sc[...], preferred_element_type=F32))
        h = h * jnp.dot(xb, wu_sc[...], preferred_element_type=F32)
        out_ref[...] = jnp.dot(h.astype(BF16), wd_sc[...], preferred_element_type=F32)

    @pl.when(jnp.logical_not(used))
    def _():
        out_ref[...] = jnp.zeros_like(out_ref)


def _experts(tile_expert, next_expert, n_used, xs, wg, wu, wd, tile_rows, expert0):
    n_slots, d = xs.shape
    f = wg.shape[-1]

    def rows(j, te, nx, nu):
        return (jnp.maximum(jnp.minimum(j, nu[0] - 1), 0), 0)

    hbm = pl.BlockSpec(memory_space=pl.ANY)
    grid_spec = pltpu.PrefetchScalarGridSpec(
        num_scalar_prefetch=3,
        grid=(n_slots // tile_rows,),
        in_specs=[pl.BlockSpec((tile_rows, d), rows), hbm, hbm, hbm],
        out_specs=pl.BlockSpec((tile_rows, d), lambda j, te, nx, nu: (j, 0)),
        scratch_shapes=[pltpu.VMEM((d, f), BF16), pltpu.VMEM((d, f), BF16), pltpu.VMEM((f, d), BF16),
                        pltpu.VMEM((2, d, f), F32), pltpu.VMEM((2, d, f), F32), pltpu.VMEM((2, f, d), F32),
                        pltpu.SemaphoreType.DMA((2,)), pltpu.SMEM((1,), I32)],
    )
    return pl.pallas_call(
        functools.partial(_experts_kernel, expert0=expert0),
        grid_spec=grid_spec,
        out_shape=jax.ShapeDtypeStruct((n_slots, d), F32),
        compiler_params=_params(("arbitrary",)),
        name="moe_experts",
    )(tile_expert, next_expert, n_used, xs, wg, wu, wd)


def _combine_kernel(pos_ref, x_ref, route_ref, lng_ref, lnb_ref, ys_ref, *rest, alpha, split_tiles):
    if split_tiles is None:
        out_ref, o_sc, sem = rest
    else:
        out_ref, out_s_ref, o_sc, sem = rest
    i = pl.program_id(0)
    n = pl.num_programs(0)
    tm = x_ref.shape[0]
    slot = i % 2
    spare = 1 - slot
    nxt = jnp.minimum(i + 1, n - 1)
    rows_per_chunk = tm // COMBINE_CHUNKS

    def copy(tile, sl, t, c):
        p = pos_ref[2 * (tile * tm + t) + c]
        return pltpu.make_async_copy(ys_ref.at[pl.ds(p, 1)], o_sc.at[sl, c, pl.ds(t, 1)], sem.at[sl])

    def wait_tile(tile, sl):
        for _ in range(2 * tm):
            copy(tile, sl, 0, 0).wait()

    @pl.when(i == 0)
    def _():
        def body(t, carry):
            copy(0, 0, t, 0).start(priority=0)
            copy(0, 0, t, 1).start(priority=1)
            return carry
        lax.fori_loop(0, tm, body, 0, unroll=DMA_UNROLL)

    wait_tile(i, slot)

    def mix_into(out):
        for k in range(COMBINE_CHUNKS):
            rows = slice(k * rows_per_chunk, (k + 1) * rows_per_chunk)
            for t in range(rows.start, rows.stop):
                copy(nxt, spare, t, 0).start(priority=0)
                copy(nxt, spare, t, 1).start(priority=1)
            y = route_ref[rows, 2:3] * o_sc[slot, 0, rows] + route_ref[rows, 3:4] * o_sc[slot, 1, rows]
            out[rows, :] = _layer_norm(alpha * x_ref[rows, :] + y, lng_ref[...], lnb_ref[...])

    if split_tiles is None:
        mix_into(out_ref)
    else:
        @pl.when(i < split_tiles)
        def _():
            mix_into(out_ref)

        @pl.when(i >= split_tiles)
        def _():
            mix_into(out_s_ref)

    @pl.when(i == n - 1)
    def _():
        wait_tile(nxt, spare)


def _combine(pos_flat, x, route, lng, lnb, ys, *, alpha, tm, split_rows=None):
    t, d = x.shape
    out_specs = pl.BlockSpec((tm, d), lambda i, pos: (i, 0))
    out_shape = jax.ShapeDtypeStruct((t, d), F32)
    npt = None
    if split_rows is not None:
        npt = split_rows // tm
        out_specs = [pl.BlockSpec((tm, d), lambda i, pos: (jnp.minimum(i, npt - 1), 0)),
                     pl.BlockSpec((tm, d), lambda i, pos: (jnp.maximum(i - npt, 0), 0))]
        out_shape = [jax.ShapeDtypeStruct((split_rows, d), F32),
                     jax.ShapeDtypeStruct((t - split_rows, d), F32)]
    grid_spec = pltpu.PrefetchScalarGridSpec(
        num_scalar_prefetch=1,
        grid=(t // tm,),
        in_specs=[pl.BlockSpec((tm, d), lambda i, pos: (i, 0)),
                  pl.BlockSpec((tm, LANES), lambda i, pos: (i, 0)),
                  pl.BlockSpec((1, d), lambda i, pos: (0, 0)),
                  pl.BlockSpec((1, d), lambda i, pos: (0, 0)),
                  pl.BlockSpec(memory_space=pl.ANY)],
        out_specs=out_specs,
        scratch_shapes=[pltpu.VMEM((2, 2, tm, d), F32), pltpu.SemaphoreType.DMA((2,))],
    )
    return pl.pallas_call(
        functools.partial(_combine_kernel, alpha=alpha, split_tiles=npt),
        grid_spec=grid_spec,
        out_shape=out_shape,
        compiler_params=_params(("arbitrary",)),
        name="moe_combine",
    )(pos_flat, x, route, lng, lnb, ys)


def _hier_moe_ln(x, route, counts, w_gate, w_up, w_down, lng, lnb, *, alpha, tm, tile_rows,
                 layer=0, split_rows=None):
    t, d = x.shape
    n_exp = EXPERT_GROUPS * EXPERTS_PER_GROUP
    pos, ends = _slots(route, counts, WIDE_TILE_FACTOR * tm, tile_rows)
    pos_flat = pos[:, :2].reshape(-1)
    n_slots = 2 * t + n_exp * tile_rows
    n_tiles = n_slots // tile_rows
    seg_end = ends[0, :n_exp]
    tile_start = jnp.arange(n_tiles, dtype=I32) * tile_rows
    tile_expert = jnp.minimum(jnp.sum(tile_start[:, None] >= seg_end[None, :], axis=1), n_exp - 1).astype(I32)
    n_used = (seg_end[n_exp - 1:] // tile_rows).astype(I32)
    ids = jnp.arange(n_exp, dtype=I32)
    nonempty = seg_end > jnp.concatenate([jnp.zeros((1,), I32), seg_end[:-1]])
    later = jnp.where((ids[None, :] > tile_expert[:, None]) & nonempty[None, :], ids[None, :], n_exp)
    next_expert = jnp.min(later, axis=1)
    next_expert = jnp.where(next_expert < n_exp, next_expert, -1).astype(I32)
    xs = _scatter(pos_flat, tile_expert, n_used, x, n_slots, tm, tile_rows)
    f = w_gate.shape[-1]
    ys = _experts(tile_expert, next_expert, n_used, xs, w_gate.reshape(-1, d, f), w_up.reshape(-1, d, f),
                  w_down.reshape(-1, f, d), tile_rows, layer * n_exp)
    return _combine(pos_flat, x, route, lng, lnb, ys, alpha=alpha, tm=tm, split_rows=split_rows)


def _ssd_in_kernel(x_ref, wz_ref, wx_ref, wdt_ref, z_ref, xbc_ref, dt_ref):
    xb = x_ref[...].astype(BF16)
    z_ref[...] = jnp.dot(xb, wz_ref[...], preferred_element_type=F32)
    xbc_ref[...] = jnp.dot(xb, wx_ref[...], preferred_element_type=F32)
    dt_ref[...] = jnp.dot(xb, wdt_ref[...], preferred_element_type=F32)


def _ssd_in(x, wz, wx, wdt, tm):
    t, d = x.shape
    outs = [wz.shape[1], wx.shape[1], wdt.shape[1]]
    return pl.pallas_call(
        _ssd_in_kernel,
        grid=(t // tm,),
        in_specs=[pl.BlockSpec((tm, d), lambda i: (i, 0))]
                 + [pl.BlockSpec(w.shape, lambda i: (0, 0)) for w in (wz, wx, wdt)],
        out_specs=[pl.BlockSpec((tm, n), lambda i: (i, 0)) for n in outs],
        out_shape=[jax.ShapeDtypeStruct((t, n), F32) for n in outs],
        compiler_params=_params(("parallel",)),
        name="ssd_in",
    )(x, wz, wx, wdt)


def _ssd_kernel(xbc_ref, dt_ref, z_ref, conv0_ref, h0_ref, cw_ref, cb_ref, dtb_ref, alog_ref,
                dexp_ref, nw_ref, expand_ref, y_ref, convo_ref, hout_ref, pad_sc, ht_sc, *, chunk):
    c = chunk
    ci = pl.program_id(1)
    d_inner = z_ref.shape[1]
    gn = SSD_GROUPS * D_STATE
    per_group = d_inner // SSD_GROUPS
    tail = SUBLANES

    @pl.when(ci == 0)
    def _():
        pad_sc[0:tail, :] = conv0_ref[...]
        ht_sc[...] = h0_ref[...].T

    pad_sc[tail:tail + c, :] = xbc_ref[...]
    acc = cb_ref[...]
    for i in range(CONV_WIDTH):
        off = tail - (CONV_WIDTH - 1) + i
        acc = acc + pad_sc[off:off + c, :] * cw_ref[i:i + 1, :]
    xbc = _silu(acc)
    last_rows = pad_sc[c:c + tail, :]
    pad_sc[0:tail, :] = last_rows
    xs = xbc[:, :d_inner]

    dt = _softplus(dt_ref[...] + dtb_ref[...])
    da = dt * (-jnp.exp(alog_ref[...]))
    tril_incl = (_iota2((c, c), 1) <= _iota2((c, c), 0)).astype(F32)
    acum = _mm_sel_l(tril_incl, da)
    a_last = acum[c - 1:c, :]

    def to_cols(t):
        if c < LANES:
            t = jnp.concatenate([t, jnp.zeros((LANES - c, LANES), F32)], axis=0)
        return t.T[:, :c]

    acum_t = to_cols(acum)
    expand = expand_ref[...]
    per_head = jnp.concatenate([jnp.exp(acum), jnp.exp(a_last - acum) * dt, dt], axis=0)
    per_chan = _mm_sel_r(per_head, expand)
    ea, dd = per_chan[:c], per_chan[c:2 * c]
    x_dt = xs * per_chan[2 * c:]
    ea_last = _mm_sel_r(jnp.exp(a_last), expand)
    causal = _iota2((c, c), 1) <= _iota2((c, c), 0)
    lane = _iota2((1, LANES), 1)
    m0 = (lane < HEAD).astype(F32)
    m1 = 1.0 - m0

    ys = []
    for g in range(SSD_GROUPS):
        b_g = xbc[:, d_inner + g * D_STATE:d_inner + (g + 1) * D_STATE]
        c_g = xbc[:, d_inner + gn + g * D_STATE:d_inner + gn + (g + 1) * D_STATE]
        cb = _mm_nt(c_g, b_g)
        gsl = slice(g * per_group, (g + 1) * per_group)
        ht_g = ht_sc[:, gsl]
        y_off = _mm(c_g, ht_g) * ea[:, gsl]
        ht_sc[:, gsl] = ht_g * ea_last[:, gsl] + _mm_tn(b_g, xs[:, gsl] * dd[:, gsl])
        pieces = []
        for q in range(per_group // LANES):
            j0 = (g * per_group) // HEAD + 2 * q
            ms = []
            for j in (j0, j0 + 1):
                seg = acum[:, j:j + 1] - acum_t[j:j + 1, :]
                ms.append(jnp.where(causal, cb * jnp.exp(seg), 0.0))
            lo = g * per_group + q * LANES
            x_pair = x_dt[:, lo:lo + LANES]
            x_stack = jnp.concatenate([x_pair * m0, x_pair * m1], axis=0)
            pieces.append(_mm(jnp.concatenate(ms, axis=1), x_stack))
        y_g = jnp.concatenate(pieces, axis=1) + y_off + xs[:, gsl] * dexp_ref[:, gsl]
        y_g = y_g * _silu(z_ref[:, gsl])
        y_g = y_g * lax.rsqrt(jnp.mean(y_g * y_g, axis=-1, keepdims=True) + RMS_EPS)
        ys.append(y_g * nw_ref[:, gsl])
    y_ref[...] = jnp.concatenate(ys, axis=1).astype(y_ref.dtype)

    @pl.when(ci == pl.num_programs(1) - 1)
    def _():
        convo_ref[...] = last_rows
        hout_ref[...] = ht_sc[...].T


def _ssd(xbc, dt, z, conv0, h0, cw, cb, dtb, alog, dexp, nw, expand, *, batch, seq, row0, chunk):
    y_dtype = BF16 if chunk % (2 * SUBLANES) == 0 else F32
    d_inner = z.shape[1]
    conv_dim = xbc.shape[1]
    nc = seq // chunk
    blk0 = row0 // chunk

    def tok(n):
        return pl.BlockSpec((chunk, n), lambda b, c: (blk0 + b * nc + c, 0))

    def par(arr):
        return pl.BlockSpec(arr.shape, lambda b, c: (0, 0))

    return pl.pallas_call(
        functools.partial(_ssd_kernel, chunk=chunk),
        grid=(batch, nc),
        in_specs=[tok(conv_dim), tok(LANES), tok(d_inner),
                  pl.BlockSpec((None, SUBLANES, conv_dim), lambda b, c: (b, 0, 0)),
                  pl.BlockSpec((None, d_inner, D_STATE), lambda b, c: (b, 0, 0)),
                  par(cw), par(cb), par(dtb), par(alog), par(dexp), par(nw), par(expand)],
        out_specs=[pl.BlockSpec((chunk, d_inner), lambda b, c: (b * nc + c, 0)),
                   pl.BlockSpec((None, SUBLANES, conv_dim), lambda b, c: (b, 0, 0)),
                   pl.BlockSpec((None, d_inner, D_STATE), lambda b, c: (b, 0, 0))],
        out_shape=[jax.ShapeDtypeStruct((batch * seq, d_inner), y_dtype),
                   jax.ShapeDtypeStruct((batch, SUBLANES, conv_dim), F32),
                   jax.ShapeDtypeStruct((batch, d_inner, D_STATE), F32)],
        scratch_shapes=[pltpu.VMEM((chunk + SUBLANES, conv_dim), F32), pltpu.VMEM((D_STATE, d_inner), F32)],
        compiler_params=_params(("parallel", "arbitrary")),
        name="ssd_l%d" % chunk,
    )(xbc, dt, z, conv0, h0, cw, cb, dtb, alog, dexp, nw, expand)


def _rwkv_layer(xp3, xs3, shift_s, wkv_s, prm, lng, lnb, router, *, alpha, tm):
    (mu, w_rkv, w0, w1, w2, a0, a1, a2, g1, g2, k_k, k_a, r_k, gn_w, gn_b, w_o) = prm
    bp, sp, d = xp3.shape
    bs, ss, _ = xs3.shape
    tp = bp * sp
    heads = d // HEAD
    x_p = xp3.reshape(tp, d)
    x_s = xs3.reshape(bs * ss, d)
    prev_s = jnp.concatenate([shift_s[:, None, :], xs3[:, :-1]], axis=1).reshape(bs * ss, d)
    row = lambda v: v.reshape(1, d)
    r, k, v, lw, a, g = _rwkv_pre(x_p, x_s, prev_s, sp, mu, w_rkv.astype(BF16), row(w0), w1.astype(BF16),
                                  w2.astype(BF16), row(a0), a1.astype(BF16), a2.astype(BF16),
                                  g1.astype(BF16), g2.astype(BF16), tm)
    pv = (row(k_k), row(k_a), row(r_k), row(gn_w), row(gn_b))
    o_p, sp_out = _wkv(r, k, v, lw, a, *pv, None, batch=bp, seq=sp, row0=0,
                       chunk=min(WKV_CHUNK, sp), pairs_per_step=min(WKV_PAIRS_PER_STEP, heads // 2))
    chunk_s = min(WKV_CHUNK, ss)
    sps = WKV_SHORT_SEQS_PER_STEP
    if ss != chunk_s or bs % sps or tp % (sps * chunk_s):
        sps = 1
    o_s, ss_out = _wkv(r, k, v, lw, a, *pv, wkv_s, batch=bs, seq=ss, row0=tp, chunk=chunk_s,
                       pairs_per_step=heads // 2, seqs_per_step=sps)
    routed = _proj_ln(o_p, o_s, g, (x_p, x_s, 0), w_o.astype(BF16), lng, lnb, *router, alpha=alpha,
                      tm=WIDE_TILE_FACTOR * tm)
    return routed, sp_out, ss_out


def _ssd_layer(x, bp, sp, bs, ss, conv_s, ssm_s, prm, lng, lnb, router, *, alpha, tm):
    (w_in, conv_w, conv_b, dt_bias, a_log, d_skip, norm_w, w_out) = prm
    t, d = x.shape
    tp = bp * sp
    heads = a_log.shape[0]
    d_inner = heads * HEAD
    conv_dim = conv_w.shape[1]
    wz = w_in[:, :d_inner].astype(BF16)
    wx = w_in[:, d_inner:d_inner + conv_dim].astype(BF16)
    wdt = jnp.pad(w_in[:, d_inner + conv_dim:], ((0, 0), (0, LANES - heads))).astype(BF16)
    wide = WIDE_TILE_FACTOR * tm
    z, xbc, dt = _ssd_in(x, wz, wx, wdt, wide)
    lane_pad = lambda v: jnp.pad(v, (0, LANES - heads))[None, :]
    expand = (jnp.arange(LANES)[:, None] == (jnp.arange(d_inner) // HEAD)[None, :]).astype(F32)
    common = (conv_w, conv_b[None, :], lane_pad(dt_bias), lane_pad(a_log),
              jnp.repeat(d_skip, HEAD)[None, :], norm_w[None, :], expand)
    conv_tail = CONV_WIDTH - 1
    pad_conv = lambda cs: jnp.pad(cs, ((0, 0), (SUBLANES - conv_tail, 0), (0, 0)))
    y_p, conv_p, h_p = _ssd(xbc, dt, z, jnp.zeros((bp, SUBLANES, conv_dim), F32),
                            jnp.zeros((bp, d_inner, D_STATE), F32), *common,
                            batch=bp, seq=sp, row0=0, chunk=SSD_CHUNK if sp % SSD_CHUNK == 0 else sp)
    y_s, conv_s_out, h_s = _ssd(xbc, dt, z, pad_conv(conv_s), ssm_s.reshape(bs, d_inner, D_STATE), *common,
                                batch=bs, seq=ss, row0=tp, chunk=SSD_CHUNK if ss % SSD_CHUNK == 0 else ss)
    routed = _proj_ln(y_p, y_s, None, (x, x, tp // wide), w_out.astype(BF16), lng, lnb, *router, alpha=alpha,
                      tm=wide)
    shape_h = lambda h, b: h.reshape(b, heads, HEAD, D_STATE)
    return (routed, conv_p[:, SUBLANES - conv_tail:], conv_s_out[:, SUBLANES - conv_tail:], shape_h(h_p, bp),
            shape_h(h_s, bs))


def kernel(x_prompt, x_sample, state_rwkv_wkv, state_rwkv_shift, state_ssd_ssm, state_ssd_conv,
           rwkv_mu, rwkv_w_rkv, rwkv_w0, rwkv_w1, rwkv_w2, rwkv_a0, rwkv_a1, rwkv_a2,
           rwkv_g1, rwkv_g2, rwkv_k_k, rwkv_k_a, rwkv_r_k, rwkv_gn_w, rwkv_gn_b, rwkv_w_o,
           ssd_w_in, ssd_conv_w, ssd_conv_b, ssd_dt_bias, ssd_a_log, ssd_d, ssd_norm_w, ssd_w_out,
           ln_gain, ln_bias, moe_w_rg, moe_b_rg, moe_w_re, moe_b_re, moe_w_gate, moe_w_up, moe_w_down):
    bp, sp, d = x_prompt.shape
    bs, ss, _ = x_sample.shape
    depth = ln_gain.shape[0]
    alpha = (2 * depth) ** 0.25
    tp, ts = bp * sp, bs * ss
    tm = 256
    tile_rows = 256
    x = None
    xp3, xs3 = x_prompt, x_sample
    wkv_p, wkv_s, sh_p, sh_s, ssm_p, ssm_s, cv_p, cv_s = [], [], [], [], [], [], [], []
    for i in range(depth):
        j = i // 2
        lng = lambda n: ln_gain[i, n][None, :]
        lnb = lambda n: ln_bias[i, n][None, :]
        router = _router_terms(moe_w_rg[i], moe_b_rg[i], moe_w_re[i], moe_b_re[i])
        if i % 2 == 0:
            if x is not None:
                xp3, xs3 = x[:tp].reshape(bp, sp, d), x[tp:].reshape(bs, ss, d)
            prm = (rwkv_mu[j], rwkv_w_rkv[j], rwkv_w0[j], rwkv_w1[j], rwkv_w2[j], rwkv_a0[j],
                   rwkv_a1[j], rwkv_a2[j], rwkv_g1[j], rwkv_g2[j], rwkv_k_k[j], rwkv_k_a[j],
                   rwkv_r_k[j].reshape(-1), rwkv_gn_w[j], rwkv_gn_b[j], rwkv_w_o[j])
            sh_p.append(xp3[:, -1])
            sh_s.append(xs3[:, -1])
            routed, w_p, w_s = _rwkv_layer(xp3, xs3, state_rwkv_shift[j], state_rwkv_wkv[j], prm,
                                           lng(0), lnb(0), router, alpha=alpha, tm=tm)
            wkv_p.append(w_p)
            wkv_s.append(w_s)
        else:
            prm = (ssd_w_in[j], ssd_conv_w[j], ssd_conv_b[j], ssd_dt_bias[j], ssd_a_log[j],
                   ssd_d[j], ssd_norm_w[j], ssd_w_out[j])
            routed, c_p, c_s, h_p, h_s = _ssd_layer(x, bp, sp, bs, ss, state_ssd_conv[j], state_ssd_ssm[j],
                                                    prm, lng(0), lnb(0), router, alpha=alpha, tm=tm)
            cv_p.append(c_p)
            cv_s.append(c_s)
            ssm_p.append(h_p)
            ssm_s.append(h_s)
        x = _hier_moe_ln(*routed, moe_w_gate, moe_w_up, moe_w_down, lng(1), lnb(1), alpha=alpha, tm=tm,
                         tile_rows=tile_rows, layer=i, split_rows=tp if i == depth - 1 else None)
    yp = x[0].reshape(bp, sp, d)
    ys = x[1].reshape(bs, ss, d)
    return (yp, ys, jnp.stack(wkv_p), jnp.stack(wkv_s), jnp.stack(sh_p), jnp.stack(sh_s),
            jnp.stack(ssm_p), jnp.stack(ssm_s), jnp.stack(cv_p), jnp.stack(cv_s))
```
